```python
import math
import jax, jax.numpy as jnp
from jax import lax
import numpy as np

D_MODEL = 1024
BATCH = 32
SEQ = 2048
DEPTH = 1

N_HEADS = 8
HEAD_DK = 64
HEAD_DV = 2 * HEAD_DK
Q_COLS = N_HEADS * 2 * HEAD_DK
ATTN_W = N_HEADS * HEAD_DV
Q_BLOCK = 128
N_BUCKETS = 32
MAX_DISTANCE = 128
POOL_WINDOWS = (2, 4, 8, 16)
N_POOL_GROUPS = 4
POOL_W = D_MODEL
POOL_GC = POOL_W // N_POOL_GROUPS
IN_COLS = 2 * Q_COLS + ATTN_W + POOL_W + 2 * D_MODEL
N_EXPERTS = 256
TOP_K = 8
N_EXPERT_GROUPS = 8
TOPK_GROUPS = 4
D_EXPERT = 256
D_SHARED = 256
ROUTED_SCALE = 2.5
MOE_BLOCK = 256
EPS = 1e-6

kernel_name = "hybrid_diffattn_pool_moe_encoder"


def rmsnorm(x, g):
    xf = x.astype(jnp.float32)
    y = xf * lax.rsqrt(jnp.mean(xf * xf, axis=-1, keepdims=True) + EPS) * g.astype(jnp.float32)
    return y.astype(x.dtype)


def t5_buckets(rel):
    nb = N_BUCKETS // 2
    max_exact = nb // 2
    n = jnp.abs(rel)
    large = max_exact + (jnp.log(jnp.maximum(n, 1).astype(jnp.float32) / max_exact)
                         / math.log(MAX_DISTANCE / max_exact) * (nb - max_exact)).astype(jnp.int32)
    large = jnp.minimum(large, nb - 1)
    return jnp.where(rel > 0, nb, 0) + jnp.where(n < max_exact, n, large)


def diff_attention(q, k, v, rel_table, lam):
    B, S = q.shape[0], q.shape[1]
    nqb = S // Q_BLOCK
    qb = q.reshape(B, nqb, Q_BLOCK, N_HEADS, 2, HEAD_DK).transpose(1, 0, 2, 3, 4, 5)
    kpos = jnp.arange(S)

    def block(args):
        qblk, q0 = args
        qpos = q0 + jnp.arange(Q_BLOCK)
        bias = rel_table[t5_buckets(kpos[None, :] - qpos[:, None])]
        bias = bias.transpose(2, 0, 1).astype(jnp.float32)
        s = jnp.einsum('bqhcd,bkhcd->bchqk', qblk, k).astype(jnp.float32) + bias
        p = jax.nn.softmax(s, axis=-1)
        a = (p[:, 0] - lam * p[:, 1]).astype(v.dtype)
        return jnp.einsum('bhqk,bkhd->bqhd', a, v)

    out = lax.map(block, (qb, jnp.arange(nqb) * Q_BLOCK))
    return out.transpose(1, 0, 2, 3, 4).reshape(B, S, N_HEADS, HEAD_DV)


def multiscale_pool(p, pool_w, pool_scale):
    B, S, _ = p.shape
    pg = p.astype(jnp.float32).reshape(B, S, N_POOL_GROUPS, POOL_GC)
    cs = jnp.concatenate([jnp.zeros_like(pg[:, :1]), jnp.cumsum(pg, axis=1)], axis=1)
    pos = jnp.arange(S)[:, None]
    half = jnp.array([w // 2 for w in POOL_WINDOWS], dtype=jnp.int32)[None, :]
    lo = jnp.clip(pos - half, 0, S - 1)
    hi = jnp.clip(pos + half - 1, 0, S - 1)
    grp = jnp.arange(N_POOL_GROUPS)[None, :]
    win_sum = cs[:, hi + 1, grp] - cs[:, lo, grp]
    cnt = (hi - lo + 1).astype(jnp.float32)[None, :, :, None]
    mixed = (win_sum / cnt - pg).astype(p.dtype)
    out = jnp.einsum('bsgc,gcd->bsgd', mixed, pool_w).reshape(B, S, POOL_W)
    return out * pool_scale


def moe_ffn(h, w_router, b_router, w_eg, w_eu, w_ed, w_sg, w_su, w_sd):
    T, D = h.shape
    scores = jax.nn.sigmoid((h @ w_router).astype(jnp.float32))
    biased = scores + b_router.astype(jnp.float32)
    per_grp = N_EXPERTS // N_EXPERT_GROUPS
    grp_score = lax.top_k(biased.reshape(T, N_EXPERT_GROUPS, per_grp), 2)[0].sum(-1)
    _, grp_idx = lax.top_k(grp_score, TOPK_GROUPS)
    grp_mask = jax.nn.one_hot(grp_idx, N_EXPERT_GROUPS, dtype=jnp.float32).sum(1) > 0
    exp_mask = jnp.repeat(grp_mask, per_grp, axis=1)
    _, eidx = lax.top_k(jnp.where(exp_mask, biased, -jnp.inf), TOP_K)
    wts = jnp.take_along_axis(scores, eidx, axis=-1)
    wts = wts / jnp.sum(wts, axis=-1, keepdims=True) * ROUTED_SCALE

    n_rows = T * TOP_K
    e_flat = eidx.reshape(n_rows)
    tok_flat = jnp.repeat(jnp.arange(T, dtype=jnp.int32), TOP_K)
    w_flat = wts.reshape(n_rows)
    order = jnp.argsort(e_flat)
    e_s, tok_s, w_s = e_flat[order], tok_flat[order], w_flat[order]
    counts = jnp.bincount(e_flat, length=N_EXPERTS)
    start = jnp.cumsum(counts) - counts
    nblk = (counts + MOE_BLOCK - 1) // MOE_BLOCK
    blk_end = jnp.cumsum(nblk)
    blk_start = blk_end - nblk
    dest = blk_start[e_s] * MOE_BLOCK + (jnp.arange(n_rows) - start[e_s])
    n_blocks = -(-n_rows // MOE_BLOCK) + N_EXPERTS
    tok_buf = jnp.zeros((n_blocks * MOE_BLOCK,), jnp.int32).at[dest].set(tok_s)
    w_buf = jnp.zeros((n_blocks * MOE_BLOCK,), jnp.float32).at[dest].set(w_s)
    blk_expert = jnp.clip(jnp.searchsorted(blk_end, jnp.arange(n_blocks), side='right'),
                          0, N_EXPERTS - 1)

    def body(acc, blk):
        tok, wt, e = blk
        xb = h[tok]
        yb = (jax.nn.silu(xb @ w_eg[e]) * (xb @ w_eu[e])) @ w_ed[e]
        return acc.at[tok].add(yb.astype(jnp.float32) * wt[:, None]), None

    acc, _ = lax.scan(body, jnp.zeros((T, D), jnp.float32),
                      (tok_buf.reshape(n_blocks, MOE_BLOCK), w_buf.reshape(n_blocks, MOE_BLOCK), blk_expert))
    shared = (jax.nn.silu(h @ w_sg) * (h @ w_su)) @ w_sd
    return acc.astype(h.dtype) + shared


def setup_inputs(seed: int = 0) -> dict:
    key = jax.random.key(seed)
    ks = jax.random.split(key, 32)
    f32 = jnp.float32
    D, L = D_MODEL, DEPTH

    def nrm(k, shape, scale):
        return jax.random.normal(k, shape, f32) * scale

    def gain(k, shape):
        return 1.0 + 0.1 * jax.random.normal(k, shape, f32)

    return {
        'x': nrm(ks[0], (BATCH, SEQ, D), 1.0),
        'c': nrm(ks[1], (BATCH, D), 1.0),
        'rel_bias_table': nrm(ks[2], (N_BUCKETS, N_HEADS), 0.5),
        'w_ada': nrm(ks[3], (L, D, 6 * D), 0.5 * D ** -0.5),
        'b_ada': nrm(ks[4], (L, 6 * D), 0.02),
        'norm1_g': gain(ks[5], (L, D)),
        'w_in': nrm(ks[6], (L, D, IN_COLS), D ** -0.5),
        'q_norm_g': gain(ks[7], (L, HEAD_DK)),
        'k_norm_g': gain(ks[8], (L, HEAD_DK)),
        'lambda_q1': nrm(ks[9], (L, HEAD_DK), 0.1),
        'lambda_k1': nrm(ks[10], (L, HEAD_DK), 0.1),
        'lambda_q2': nrm(ks[11], (L, HEAD_DK), 0.1),
        'lambda_k2': nrm(ks[12], (L, HEAD_DK), 0.1),
        'subln_g': gain(ks[13], (L, HEAD_DV)),
        'pool_w': nrm(ks[14], (L, N_POOL_GROUPS, POOL_GC, POOL_GC), POOL_GC ** -0.5),
        'pool_scale': gain(ks[15], (L, POOL_W)),
        'w_out': nrm(ks[16], (L, D, D), D ** -0.5),
        'norm2_g': gain(ks[17], (L, D)),
        'w_router': nrm(ks[18], (L, D, N_EXPERTS), D ** -0.5),
        'b_router': nrm(ks[19], (L, N_EXPERTS), 0.01),
        'w_exp_gate': nrm(ks[20], (L, N_EXPERTS, D, D_EXPERT), D ** -0.5),
        'w_exp_up': nrm(ks[21], (L, N_EXPERTS, D, D_EXPERT), D ** -0.5),
        'w_exp_down': nrm(ks[22], (L, N_EXPERTS, D_EXPERT, D), D_EXPERT ** -0.5),
        'w_sh_gate': nrm(ks[23], (L, D, D_SHARED), D ** -0.5),
        'w_sh_up': nrm(ks[24], (L, D, D_SHARED), D ** -0.5),
        'w_sh_down': nrm(ks[25], (L, D_SHARED, D), D_SHARED ** -0.5),
    }


def reference(x, c, rel_bias_table, w_ada, b_ada, norm1_g, w_in, q_norm_g, k_norm_g,
              lambda_q1, lambda_k1, lambda_q2, lambda_k2, subln_g, pool_w, pool_scale,
              w_out, norm2_g, w_router, b_router, w_exp_gate, w_exp_up, w_exp_down,
              w_sh_gate, w_sh_up, w_sh_down):
    B, S, D = x.shape
    split_pts = [Q_COLS, 2 * Q_COLS, 2 * Q_COLS + ATTN_W, 2 * Q_COLS + ATTN_W + POOL_W]
    for l in range(DEPTH):
        mod = jax.nn.silu(c) @ w_ada[l] + b_ada[l]
        sh1, sc1, g1, sh2, sc2, g2 = [m[:, None, :] for m in jnp.split(mod, 6, axis=-1)]

        h = rmsnorm(x, norm1_g[l]) * (1 + sc1) + sh1
        proj = h @ w_in[l]
        q, k, v, p_in, gts = jnp.split(proj, split_pts, axis=-1)
        q = rmsnorm(q.reshape(B, S, N_HEADS, 2, HEAD_DK), q_norm_g[l]) * (HEAD_DK ** -0.5)
        k = rmsnorm(k.reshape(B, S, N_HEADS, 2, HEAD_DK), k_norm_g[l])
        v = v.reshape(B, S, N_HEADS, HEAD_DV)
        lam_init = 0.8 - 0.6 * math.exp(-0.3 * l)
        lam = (jnp.exp(jnp.sum((lambda_q1[l] * lambda_k1[l]).astype(jnp.float32)))
               - jnp.exp(jnp.sum((lambda_q2[l] * lambda_k2[l]).astype(jnp.float32))) + lam_init)
        attn = diff_attention(q, k, v, rel_bias_table, lam)
        attn = (rmsnorm(attn, subln_g[l]) * (1 - lam_init)).reshape(B, S, ATTN_W)
        pool = multiscale_pool(p_in, pool_w[l], pool_scale[l])
        gate_a, gate_p = jnp.split(jax.nn.sigmoid(gts), 2, axis=-1)
        x = x + g1 * ((gate_a * attn + gate_p * pool) @ w_out[l])

        h2 = rmsnorm(x, norm2_g[l]) * (1 + sc2) + sh2
        y2 = moe_ffn(h2.reshape(B * S, D), w_router[l], b_router[l], w_exp_gate[l], w_exp_up[l],
                     w_exp_down[l], w_sh_gate[l], w_sh_up[l], w_sh_down[l])
        x = x + g2 * y2.reshape(B, S, D)
    return x
```

```python
import functools
import math

import jax
import jax.numpy as jnp
from jax import lax
from jax.experimental import pallas as pl
from jax.experimental.pallas import tpu as pltpu

N_HEADS = 8
HEAD_DK = 64
HEAD_DV = 2 * HEAD_DK
N_BUCKETS = 32
MAX_DISTANCE = 128
POOL_WINDOWS = (2, 4, 8, 16)
N_POOL_GROUPS = 4
N_EXPERTS = 256
TOP_K = 8
N_EXPERT_GROUPS = 8
TOPK_GROUPS = 4
ROUTED_SCALE = 2.5
MOE_BLOCK = 256
EPS = 1e-6
LAM_INIT = 0.8 - 0.6 * math.exp(-0.3 * 0)

LANES = 128
SUBLANES = 8
VMEM_LIMIT = 56 * 1024 * 1024

F32 = jnp.float32
BF16 = jnp.bfloat16
HIGHEST = lax.Precision.HIGHEST
NEG_INF = float("-inf")


def _params(*sem):
    return pltpu.CompilerParams(dimension_semantics=sem, vmem_limit_bytes=VMEM_LIMIT)


def _const_spec(shape):
    nd = len(shape)
    return pl.BlockSpec(shape, lambda *_: (0,) * nd, pipeline_mode=pl.Buffered(1))


def _ada_kernel(c_ref, w_ref, b_ref, o_ref):
    c = c_ref[...]
    s = c * jax.nn.sigmoid(c)
    o_ref[...] = jnp.dot(s, w_ref[...], precision=HIGHEST, preferred_element_type=F32) + b_ref[...]


def _ada(c, w_ada, b_ada):
    B, D = c.shape
    n = w_ada.shape[1] // D
    return pl.pallas_call(
        _ada_kernel,
        grid=(n,),
        in_specs=[pl.BlockSpec((B, D), lambda j: (0, 0)),
                  pl.BlockSpec((D, D), lambda j: (0, j)),
                  pl.BlockSpec((1, D), lambda j: (0, j))],
        out_specs=pl.BlockSpec((B, D), lambda j: (0, j)),
        out_shape=jax.ShapeDtypeStruct((B, n * D), F32),
        compiler_params=_params("arbitrary"),
        name="ada",
    )(c, w_ada, b_ada.reshape(1, -1))


def _bias_kernel(bk_ref, tab_ref, o_ref):
    bk = bk_ref[...]
    onehot = (lax.broadcasted_iota(jnp.int32, (N_BUCKETS, bk.shape[1]), 0) == bk).astype(F32)
    o_ref[...] = lax.dot_general(tab_ref[...], onehot, (((0,), (0,)), ((), ())),
                                 precision=HIGHEST, preferred_element_type=F32)


def _t5_buckets(rel):
    nb = N_BUCKETS // 2
    max_exact = nb // 2
    n = jnp.abs(rel)
    large = max_exact + (jnp.log(jnp.maximum(n, 1).astype(jnp.float32) / max_exact)
                         / math.log(MAX_DISTANCE / max_exact) * (nb - max_exact)).astype(jnp.int32)
    large = jnp.minimum(large, nb - 1)
    return jnp.where(rel > 0, nb, 0) + jnp.where(n < max_exact, n, large)


def _bias_strip(rel_table, S):
    L = 2 * S
    buckets = _t5_buckets(jnp.arange(L, dtype=jnp.int32) - S).reshape(1, L)
    return pl.pallas_call(
        _bias_kernel,
        out_shape=jax.ShapeDtypeStruct((N_HEADS, L), F32),
        name="bias",
    )(buckets, rel_table)


def _inproj_kernel(x_ref, mod_ref, g1_ref, w_ref, gmat_ref, qg_ref, kg_ref,
                   q_ref, k_ref, v_ref, p_ref, ga_ref, gp_ref):
    D = x_ref.shape[1]
    x = x_ref[...]
    h = x * lax.rsqrt(jnp.mean(x * x, axis=-1, keepdims=True) + EPS) * g1_ref[...]
    h = h * (1.0 + mod_ref[0, 1:2, :]) + mod_ref[0, 0:1, :]
    hb = h.astype(BF16)

    def chunk(c):
        return jnp.dot(hb, w_ref[:, c * D:(c + 1) * D], preferred_element_type=F32)

    def head_norm(y, g_ref):
        ss = jnp.dot((y * y).astype(BF16), gmat_ref[...], preferred_element_type=F32)
        return y * lax.rsqrt(ss * (1.0 / HEAD_DK) + EPS) * g_ref[...]

    q_ref[...] = head_norm(chunk(0), qg_ref).astype(BF16)
    k_ref[...] = head_norm(chunk(1), kg_ref).astype(BF16)
    v_ref[...] = chunk(2).astype(BF16)
    p_ref[...] = chunk(3)
    ga_ref[...] = jax.nn.sigmoid(chunk(4)).astype(BF16)
    gp_ref[...] = jax.nn.sigmoid(chunk(5)).astype(BF16)


def _inproj(x2, mod3, norm1_g, w_in_b, gmat, qg, kg, S, tm):
    T, D = x2.shape
    per_seq = S // tm
    row = lambda i: (i, 0)
    outs = [jax.ShapeDtypeStruct((T, D), BF16)] * 3 + [jax.ShapeDtypeStruct((T, D), F32)] \
        + [jax.ShapeDtypeStruct((T, D), BF16)] * 2
    return pl.pallas_call(
        _inproj_kernel,
        grid=(T // tm,),
        in_specs=[pl.BlockSpec((tm, D), row),
                  pl.BlockSpec((1, 6, D), lambda i: (i // per_seq, 0, 0)),
                  _const_spec((1, D)),
                  _const_spec(w_in_b.shape),
                  _const_spec((D, D)),
                  _const_spec((1, D)),
                  _const_spec((1, D))],
        out_specs=[pl.BlockSpec((tm, D), row)] * 6,
        out_shape=outs,
        compiler_params=_params("arbitrary"),
        name="inproj",
    )(x2, mod3, norm1_g, w_in_b, gmat, qg, kg)


def _attn_kernel(q_ref, k_ref, v_ref, ga_ref, win_ref, lam_ref, sg_ref, o_ref, bias_ref):
    tq = q_ref.shape[0]
    S = k_ref.shape[0]
    n = win_ref.shape[2]

    @pl.when(pl.program_id(2) == 0)
    def _():
        xb = jnp.broadcast_to(win_ref[0, 0], (tq, n))
        rolled = pltpu.roll(xb, n - tq, 1, stride=1, stride_axis=0)
        bias_ref[...] = rolled[:, :S]

    lp = lam_ref[...]
    lam = (jnp.exp(jnp.sum(lp[0:1] * lp[1:2], axis=-1, keepdims=True))
           - jnp.exp(jnp.sum(lp[2:3] * lp[3:4], axis=-1, keepdims=True)) + LAM_INIT)

    q = q_ref[...]
    first = lax.broadcasted_iota(jnp.int32, (1, HEAD_DV), 1) < HEAD_DK
    zero = jnp.zeros_like(q)
    k = k_ref[...]
    nt = (((1,), (1,)), ((), ()))
    bias = bias_ref[...]
    s1 = lax.dot_general(jnp.where(first, q, zero), k, nt, preferred_element_type=F32) + bias
    s2 = lax.dot_general(jnp.where(first, zero, q), k, nt, preferred_element_type=F32) + bias
    p1 = jnp.exp(s1 - jnp.max(s1, axis=-1, keepdims=True))
    p2 = jnp.exp(s2 - jnp.max(s2, axis=-1, keepdims=True))
    l1 = jnp.sum(p1, axis=-1, keepdims=True)
    l2 = jnp.sum(p2, axis=-1, keepdims=True)
    a = p1 - p2 * (lam * l1 / l2)
    o = jnp.dot(a.astype(BF16), v_ref[...], preferred_element_type=F32) / l1
    o = o * lax.rsqrt(jnp.mean(o * o, axis=-1, keepdims=True) + EPS) * sg_ref[...] * (1.0 - LAM_INIT)
    o_ref[...] = (o * ga_ref[...].astype(F32)).astype(BF16)


def _attention(q, k, v, ga, win, lam_p, subln_g, B, S, tq):
    T, D = q.shape
    nq = S // tq
    n = win.shape[2]
    qmap = lambda h, i, b: (b * nq + i, h)
    kvmap = lambda h, i, b: (b, h)
    return pl.pallas_call(
        _attn_kernel,
        grid=(N_HEADS, nq, B),
        in_specs=[pl.BlockSpec((tq, HEAD_DV), qmap),
                  pl.BlockSpec((S, HEAD_DV), kvmap),
                  pl.BlockSpec((S, HEAD_DV), kvmap),
                  pl.BlockSpec((tq, HEAD_DV), qmap),
                  pl.BlockSpec((1, 1, n), lambda h, i, b: (h * nq + i, 0, 0)),
                  pl.BlockSpec((4, HEAD_DK), lambda h, i, b: (0, 0)),
                  pl.BlockSpec((1, HEAD_DV), lambda h, i, b: (0, 0))],
        out_specs=pl.BlockSpec((tq, HEAD_DV), qmap),
        out_shape=jax.ShapeDtypeStruct((T, D), BF16),
        scratch_shapes=[pltpu.VMEM((tq, S), F32)],
        compiler_params=_params("arbitrary", "arbitrary", "arbitrary"),
        name="attn",
    )(q, k, v, ga, win, lam_p, subln_g)


def _mix_kernel(S, x_ref, am_ref, gp_ref, p_ref, pprev_ref, pnext_ref, mod_ref, pw_ref, ps_ref,
                wo_ref, g2_ref, wr_ref, x1_ref, h2r_ref, h2b_ref, lg_ref):
    tm, D = x_ref.shape
    gc = D // N_POOL_GROUPS
    halo = SUBLANES
    ne = tm + 2 * halo
    pos0 = (pl.program_id(0) % (S // tm)) * tm
    pos_e = pos0 - halo + lax.broadcasted_iota(jnp.int32, (ne, 1), 0)
    valid = (pos_e >= 0) & (pos_e < S)
    ext = jnp.concatenate([pprev_ref[...], p_ref[...], pnext_ref[...]], axis=0)
    ext = jnp.where(valid, ext, 0.0)
    pos = pos0 + lax.broadcasted_iota(jnp.int32, (tm, 1), 0)

    merged = []
    for g, w in enumerate(POOL_WINDOWS):
        half = w // 2
        e = ext[:, g * gc:(g + 1) * gc]
        sw = e
        width = 1
        while width < w:
            sw = sw + pltpu.roll(sw, width, 0)
            width *= 2
        win = pltpu.roll(sw, ne - (half - 1), 0)[halo:halo + tm] if half > 1 else sw[halo:halo + tm]
        lo = jnp.clip(pos - half, 0, S - 1)
        hi = jnp.clip(pos + half - 1, 0, S - 1)
        cnt = (hi - lo + 1).astype(F32)
        mixed = win / cnt - p_ref[:, g * gc:(g + 1) * gc]
        pooled = jnp.dot(mixed.astype(BF16), pw_ref[g], preferred_element_type=F32)
        pooled = pooled * ps_ref[:, g * gc:(g + 1) * gc]
        merged.append(am_ref[:, g * gc:(g + 1) * gc].astype(F32)
                      + gp_ref[:, g * gc:(g + 1) * gc].astype(F32) * pooled)
    merged = jnp.concatenate(merged, axis=-1).astype(BF16)
    y = jnp.dot(merged, wo_ref[...], preferred_element_type=F32)
    x1 = x_ref[...] + mod_ref[0, 2:3, :] * y
    x1_ref[...] = x1
    h2 = x1 * lax.rsqrt(jnp.mean(x1 * x1, axis=-1, keepdims=True) + EPS) * g2_ref[...]
    h2 = h2 * (1.0 + mod_ref[0, 4:5, :]) + mod_ref[0, 3:4, :]
    h2b_ref[...] = h2.astype(BF16)
    for c in range(D // LANES):
        h2r_ref[:, c, :] = h2[:, c * LANES:(c + 1) * LANES]
    lg_ref[...] = lax.dot_general(wr_ref[...], h2, (((1,), (1,)), ((), ())),
                                  precision=HIGHEST, preferred_element_type=F32)


def _mix(x2, am, gp, p_in, mod3, pool_w_b, pool_scale, w_out_b, norm2_g, w_router_t, S, tm):
    T, D = x2.shape
    per_seq = S // tm
    hb = tm // SUBLANES
    last = T // SUBLANES - 1
    row = lambda i: (i, 0)
    E = w_router_t.shape[0]
    return pl.pallas_call(
        functools.partial(_mix_kernel, S),
        grid=(T // tm,),
        in_specs=[pl.BlockSpec((tm, D), row),
                  pl.BlockSpec((tm, D), row),
                  pl.BlockSpec((tm, D), row),
                  pl.BlockSpec((tm, D), row),
                  pl.BlockSpec((SUBLANES, D), lambda i: (jnp.maximum(i * hb - 1, 0), 0)),
                  pl.BlockSpec((SUBLANES, D), lambda i: (jnp.minimum((i + 1) * hb, last), 0)),
                  pl.BlockSpec((1, 6, D), lambda i: (i // per_seq, 0, 0)),
                  _const_spec(pool_w_b.shape),
                  _const_spec((1, D)),
                  _const_spec((D, D)),
                  _const_spec((1, D)),
                  _const_spec((E, D))],
        out_specs=[pl.BlockSpec((tm, D), row),
                   pl.BlockSpec((tm, D // LANES, LANES), lambda i: (i, 0, 0)),
                   pl.BlockSpec((tm, D), row),
                   pl.BlockSpec((E, tm), lambda i: (0, i))],
        out_shape=[jax.ShapeDtypeStruct((T, D), F32),
                   jax.ShapeDtypeStruct((T, D // LANES, LANES), F32),
                   jax.ShapeDtypeStruct((T, D), BF16),
                   jax.ShapeDtypeStruct((E, T), F32)],
        compiler_params=_params("arbitrary"),
        name="mix",
    )(x2, am, gp, p_in, p_in, p_in, mod3, pool_w_b, pool_scale, w_out_b, norm2_g, w_router_t)


def _route_kernel(lg_ref, br_ref, tri_ref, eidx_ref, rank_ref, wts_ref, cnt_ref):
    E, tr = lg_ref.shape
    per = E // N_EXPERT_GROUPS

    @pl.when(pl.program_id(0) == 0)
    def _():
        cnt_ref[...] = jnp.zeros_like(cnt_ref)

    scores = jax.nn.sigmoid(lg_ref[...])
    biased = scores + br_ref[...]
    b3 = biased.reshape(N_EXPERT_GROUPS, per, tr)
    io_per = lax.broadcasted_iota(jnp.int32, b3.shape, 1)
    m1 = jnp.max(b3, axis=1, keepdims=True)
    i1 = jnp.min(jnp.where(b3 == m1, io_per, per), axis=1, keepdims=True)
    m2 = jnp.max(jnp.where(io_per == i1, NEG_INF, b3), axis=1, keepdims=True)
    gs = (m1 + m2)[:, 0, :]

    io_g = lax.broadcasted_iota(jnp.int32, gs.shape, 0)
    gsel = jnp.zeros(gs.shape, jnp.bool_)
    cur = gs
    for _ in range(TOPK_GROUPS):
        gm = jnp.max(cur, axis=0, keepdims=True)
        gi = jnp.min(jnp.where(cur == gm, io_g, N_EXPERT_GROUPS), axis=0, keepdims=True)
        pick = io_g == gi
        gsel = gsel | pick
        cur = jnp.where(pick, NEG_INF, cur)
    cur = jnp.where(gsel[:, None, :], b3, NEG_INF).reshape(E, tr)

    io_e = lax.broadcasted_iota(jnp.int32, (E, tr), 0)
    idxs, raw = [], []
    sel = jnp.zeros((E, tr), jnp.bool_)
    for _ in range(TOP_K):
        m = jnp.max(cur, axis=0, keepdims=True)
        idx = jnp.min(jnp.where(cur == m, io_e, E), axis=0, keepdims=True)
        pick = io_e == idx
        idxs.append(idx)
        raw.append(jnp.sum(jnp.where(pick, scores, 0.0), axis=0, keepdims=True))
        sel = sel | pick
        cur = jnp.where(pick, NEG_INF, cur)
    raw = jnp.concatenate(raw, axis=0)
    wts_ref[...] = raw / jnp.sum(raw, axis=0, keepdims=True) * ROUTED_SCALE
    eidx_ref[...] = jnp.concatenate(idxs, axis=0)

    self_f = sel.astype(F32)
    incl = jnp.dot(sel.astype(BF16), tri_ref[...], preferred_element_type=F32)
    before = cnt_ref[...] + incl - self_f
    ranks = [jnp.sum(jnp.where(io_e == idx, before, 0.0), axis=0, keepdims=True) for idx in idxs]
    rank_ref[...] = jnp.concatenate(ranks, axis=0).astype(jnp.int32)
    cnt_ref[...] = cnt_ref[...] + jnp.sum(self_f, axis=1, keepdims=True)


def _route(logits_t, b_router, tr):
    E, T = logits_t.shape
    tri = (jnp.arange(tr)[:, None] <= jnp.arange(tr)[None, :]).astype(BF16)
    col = lambda i: (0, i)
    return pl.pallas_call(
        _route_kernel,
        grid=(T // tr,),
        in_specs=[pl.BlockSpec((E, tr), col),
                  _const_spec((E, 1)),
                  _const_spec((tr, tr))],
        out_specs=[pl.BlockSpec((TOP_K, tr), col),
                   pl.BlockSpec((TOP_K, tr), col),
                   pl.BlockSpec((TOP_K, tr), col),
                   pl.BlockSpec((E, 1), lambda i: (0, 0))],
        out_shape=[jax.ShapeDtypeStruct((TOP_K, T), jnp.int32),
                   jax.ShapeDtypeStruct((TOP_K, T), jnp.int32),
                   jax.ShapeDtypeStruct((TOP_K, T), F32),
                   jax.ShapeDtypeStruct((E, 1), F32)],
        compiler_params=_params("arbitrary"),
        name="route",
    )(logits_t, b_router.reshape(E, 1), tri)


def _dest_kernel(eidx_ref, rank_ref, base_ref, dest_ref):
    K, tr = eidx_ref.shape
    E = base_ref.shape[0]
    io_e = lax.broadcasted_iota(jnp.int32, (E, tr), 0)
    base = base_ref[...]
    rows = [jnp.sum(jnp.where(io_e == eidx_ref[k:k + 1, :], base, 0.0), axis=0, keepdims=True)
            for k in range(K)]
    dest_ref[...] = jnp.concatenate(rows, axis=0).astype(jnp.int32) + rank_ref[...]


def _dest(eidx, rank, base, tr):
    K, T = eidx.shape
    col = lambda i: (0, i)
    return pl.pallas_call(
        _dest_kernel,
        grid=(T // tr,),
        in_specs=[pl.BlockSpec((K, tr), col), pl.BlockSpec((K, tr), col), _const_spec(base.shape)],
        out_specs=pl.BlockSpec((K, tr), col),
        out_shape=jax.ShapeDtypeStruct((K, T), jnp.int32),
        compiler_params=_params("arbitrary"),
        name="dest",
    )(eidx, rank, base)


def _dispatch_kernel(lastblk_ref, dest_ref, h_ref, xs_ref, zbuf_ref, sem):
    td = h_ref.shape[0]
    R = zbuf_ref.shape[0]

    @pl.when(pl.program_id(0) == 0)
    def _():
        zbuf_ref[...] = jnp.zeros_like(zbuf_ref)

        def zcopy(e):
            return pltpu.make_async_copy(zbuf_ref, xs_ref.at[pl.ds(lastblk_ref[e] * R, R)], sem)

        def start(e, c):
            @pl.when(lastblk_ref[e] >= 0)
            def _():
                zcopy(e).start()
            return c

        def wait(e, c):
            @pl.when(lastblk_ref[e] >= 0)
            def _():
                zcopy(e).wait()
            return c

        lax.fori_loop(0, N_EXPERTS, start, 0)
        lax.fori_loop(0, N_EXPERTS, wait, 0)

    def row_copy(t, k):
        return pltpu.make_async_copy(h_ref.at[t], xs_ref.at[dest_ref[k, t]], sem)

    def start_rows(t, c):
        for k in range(TOP_K):
            row_copy(t, k).start()
        return c

    def wait_rows(t, c):
        for k in range(TOP_K):
            row_copy(t, k).wait()
        return c

    lax.fori_loop(0, td, start_rows, 0)
    lax.fori_loop(0, td, wait_rows, 0)


def _dispatch(lastblk, dest, h2r, n_pad, td):
    T, C, L = h2r.shape
    return pl.pallas_call(
        _dispatch_kernel,
        grid_spec=pltpu.PrefetchScalarGridSpec(
            num_scalar_prefetch=1,
            grid=(T // td,),
            in_specs=[pl.BlockSpec((TOP_K, td), lambda i, lb: (0, i), memory_space=pltpu.SMEM),
                      pl.BlockSpec((td, C, L), lambda i, lb: (i, 0, 0))],
            out_specs=pl.BlockSpec(memory_space=pl.ANY),
            scratch_shapes=[pltpu.VMEM((MOE_BLOCK, C, L), F32), pltpu.SemaphoreType.DMA]),
        out_shape=jax.ShapeDtypeStruct((n_pad, C, L), F32),
        compiler_params=_params("arbitrary"),
        name="dispatch",
    )(lastblk, dest, h2r)


def _expert_kernel(be_ref, nused_ref, xs_ref, wg_ref, wu_ref, wd_ref, ys_ref):
    C = xs_ref.shape[1]

    @pl.when(pl.program_id(0) < nused_ref[0])
    def _():
        x = jnp.concatenate([xs_ref[:, c, :] for c in range(C)], axis=-1).astype(BF16)
        g = jnp.dot(x, wg_ref[0].astype(BF16), preferred_element_type=F32)
        u = jnp.dot(x, wu_ref[0].astype(BF16), preferred_element_type=F32)
        hmid = (g * jax.nn.sigmoid(g) * u).astype(BF16)
        y = jnp.dot(hmid, wd_ref[0].astype(BF16), preferred_element_type=F32)
        for c in range(C):
            ys_ref[:, c, :] = y[:, c * LANES:(c + 1) * LANES]

    @pl.when(pl.program_id(0) >= nused_ref[0])
    def _():
        ys_ref[...] = jnp.zeros_like(ys_ref)


def _experts(blk_expert, nused, xs, w_eg, w_eu, w_ed):
    n_pad, C, L = xs.shape
    n_blocks = n_pad // MOE_BLOCK
    _, D, DE = w_eg.shape
    blk = lambda j, be, nu: jnp.minimum(j, nu[0] - 1)
    wmap = lambda j, be, nu: (be[blk(j, be, nu)], 0, 0)
    return pl.pallas_call(
        _expert_kernel,
        grid_spec=pltpu.PrefetchScalarGridSpec(
            num_scalar_prefetch=2,
            grid=(n_blocks,),
            in_specs=[pl.BlockSpec((MOE_BLOCK, C, L), lambda j, be, nu: (blk(j, be, nu), 0, 0)),
                      pl.BlockSpec((1, D, DE), wmap),
                      pl.BlockSpec((1, D, DE), wmap),
                      pl.BlockSpec((1, DE, D), wmap)],
            out_specs=pl.BlockSpec((MOE_BLOCK, C, L), lambda j, be, nu: (j, 0, 0))),
        out_shape=jax.ShapeDtypeStruct((n_pad, C, L), F32),
        compiler_params=_params("arbitrary"),
        name="expert",
    )(blk_expert, nused, xs, w_eg, w_eu, w_ed)


def _combine_kernel(dest_ref, wt_ref, ys_ref, x1_ref, h2b_ref, mod_ref, wsg_ref, wsu_ref, wsd_ref,
                    o_ref, buf_ref, sem):
    tc = x1_ref.shape[0]
    C = buf_ref.shape[2]

    def row_copy(t, k):
        return pltpu.make_async_copy(ys_ref.at[dest_ref[k, t]], buf_ref.at[k, t], sem)

    def start_rows(t, c):
        for k in range(TOP_K):
            row_copy(t, k).start()
        return c

    def wait_rows(t, c):
        for k in range(TOP_K):
            row_copy(t, k).wait()
        return c

    lax.fori_loop(0, tc, start_rows, 0)

    h = h2b_ref[...]
    g = jnp.dot(h, wsg_ref[...], preferred_element_type=F32)
    u = jnp.dot(h, wsu_ref[...], preferred_element_type=F32)
    shared = jnp.dot((g * jax.nn.sigmoid(g) * u).astype(BF16), wsd_ref[...], preferred_element_type=F32)

    lax.fori_loop(0, tc, wait_rows, 0)

    wt = wt_ref[...]
    acc = None
    for k in range(TOP_K):
        yk = jnp.concatenate([buf_ref[k, :, c, :] for c in range(C)], axis=-1)
        term = yk * wt[:, k:k + 1]
        acc = term if acc is None else acc + term
    o_ref[...] = x1_ref[...] + mod_ref[0, 5:6, :] * (acc + shared)


def _combine(dest, wts_t, ys, x1, h2b, mod3, wsg, wsu, wsd, S, tc):
    T, D = x1.shape
    _, C, L = ys.shape
    per_seq = S // tc
    row = lambda i: (i, 0)
    return pl.pallas_call(
        _combine_kernel,
        grid=(T // tc,),
        in_specs=[pl.BlockSpec((TOP_K, tc), lambda i: (0, i), memory_space=pltpu.SMEM),
                  pl.BlockSpec((tc, TOP_K), row),
                  pl.BlockSpec(memory_space=pl.ANY),
                  pl.BlockSpec((tc, D), row),
                  pl.BlockSpec((tc, D), row),
                  pl.BlockSpec((1, 6, D), lambda i: (i // per_seq, 0, 0)),
                  _const_spec(wsg.shape),
                  _const_spec(wsu.shape),
                  _const_spec(wsd.shape)],
        out_specs=pl.BlockSpec((tc, D), row),
        out_shape=jax.ShapeDtypeStruct((T, D), F32),
        scratch_shapes=[pltpu.VMEM((TOP_K, tc, C, L), F32), pltpu.SemaphoreType.DMA],
        compiler_params=_params("arbitrary"),
        name="combine",
    )(dest, wts_t, ys, x1, h2b, mod3, wsg, wsu, wsd)


def _tile(n, want):
    t = min(n, want)
    while n % t:
        t //= 2
    return t


def kernel(x, c, rel_bias_table, w_ada, b_ada, norm1_g, w_in, q_norm_g, k_norm_g, lambda_q1, lambda_k1,
           lambda_q2, lambda_k2, subln_g, pool_w, pool_scale, w_out, norm2_g, w_router, b_router,
           w_exp_gate, w_exp_up, w_exp_down, w_sh_gate, w_sh_up, w_sh_down):
    B, S, D = x.shape
    T = B * S
    depth = w_ada.shape[0]
    assert depth == 1, "LAM_INIT and the single-layer pipeline assume depth 1"
    assert D == N_HEADS * HEAD_DV and S % LANES == 0
    l = 0
    tm = _tile(S, 512)
    tq = _tile(S, 256)

    x2 = x.reshape(T, D)
    mod3 = _ada(c, w_ada[l], b_ada[l]).reshape(B, 6, D)

    grp = jnp.arange(D) // HEAD_DK
    gmat = (grp[:, None] == grp[None, :]).astype(BF16)
    qg = (jnp.tile(q_norm_g[l], D // HEAD_DK) * (HEAD_DK ** -0.5)).reshape(1, D)
    kg = jnp.tile(k_norm_g[l], D // HEAD_DK).reshape(1, D)
    q, k, v, p_in, ga, gp = _inproj(x2, mod3, norm1_g[l].reshape(1, D), w_in[l].astype(BF16),
                                    gmat, qg, kg, S, tm)

    strip = _bias_strip(rel_bias_table, S)
    nq = S // tq
    n = S + tq
    win = jnp.stack([strip[:, S - (i + 1) * tq: S - (i + 1) * tq + n] for i in range(nq)], axis=1)
    win = win.reshape(N_HEADS * nq, 1, n)
    lam_p = jnp.stack([lambda_q1[l], lambda_k1[l], lambda_q2[l], lambda_k2[l]])
    am = _attention(q, k, v, ga, win, lam_p, subln_g[l].reshape(1, HEAD_DV), B, S, tq)

    x1, h2r, h2b, logits_t = _mix(x2, am, gp, p_in, mod3, pool_w[l].astype(BF16),
                                  pool_scale[l].reshape(1, D), w_out[l].astype(BF16),
                                  norm2_g[l].reshape(1, D), w_router[l].T, S, tm)

    eidx, rank, wts, counts = _route(logits_t, b_router[l], _tile(T, 512))

    cnt = counts[:, 0].astype(jnp.int32)
    nblk = (cnt + MOE_BLOCK - 1) // MOE_BLOCK
    blk_end = jnp.cumsum(nblk)
    base = ((blk_end - nblk) * MOE_BLOCK).astype(F32).reshape(N_EXPERTS, 1)
    lastblk = jnp.where(nblk > 0, blk_end - 1, -1).astype(jnp.int32)
    n_blocks = -(-(T * TOP_K) // MOE_BLOCK) + N_EXPERTS
    blk_expert = jnp.clip(jnp.searchsorted(blk_end, jnp.arange(n_blocks), side='right'),
                          0, N_EXPERTS - 1).astype(jnp.int32)
    nused = blk_end[-1:].astype(jnp.int32)

    dest = _dest(eidx, rank, base, _tile(T, 512))
    xs = _dispatch(lastblk, dest, h2r, n_blocks * MOE_BLOCK, _tile(S, 256))
    ys = _experts(blk_expert, nused, xs, w_exp_gate[l], w_exp_up[l], w_exp_down[l])
    out = _combine(dest, wts.T, ys, x1, h2b, mod3, w_sh_gate[l].astype(BF16), w_sh_up[l].astype(BF16),
                   w_sh_down[l].astype(BF16), S, _tile(S, 128))
    return out.reshape(B, S, D)
```

```python
import functools
import math

import jax
import jax.numpy as jnp
from jax import lax
from jax.experimental import pallas as pl
from jax.experimental.pallas import tpu as pltpu

N_HEADS = 8
HEAD_DK = 64
HEAD_DV = 2 * HEAD_DK
N_BUCKETS = 32
MAX_DISTANCE = 128
POOL_WINDOWS = (2, 4, 8, 16)
N_POOL_GROUPS = 4
N_EXPERTS = 256
TOP_K = 8
N_EXPERT_GROUPS = 8
TOPK_GROUPS = 4
ROUTED_SCALE = 2.5
MOE_BLOCK = 256
EPS = 1e-6
LAM_INIT = 0.8 - 0.6 * math.exp(-0.3 * 0)
LOG2E = math.log2(math.e)
ATTN_ROWS = 16

LANES = 128
SUBLANES = 8
VMEM_LIMIT = 56 * 1024 * 1024

F32 = jnp.float32
BF16 = jnp.bfloat16
HIGHEST = lax.Precision.HIGHEST
NEG_INF = float("-inf")


def _params(*sem):
    return pltpu.CompilerParams(dimension_semantics=sem, vmem_limit_bytes=VMEM_LIMIT)


def _const_spec(shape):
    nd = len(shape)
    return pl.BlockSpec(shape, lambda *_: (0,) * nd, pipeline_mode=pl.Buffered(1))


def _rows_load(ref, first, n, chunks=SUBLANES):
    return jnp.concatenate([ref[pl.ds(first * chunks + c, n, stride=chunks), :] for c in range(chunks)],
                           axis=-1)


def _rows_store(ref, first, val, chunks=SUBLANES):
    n = val.shape[0]
    for c in range(chunks):
        ref[pl.ds(first * chunks + c, n, stride=chunks), :] = val[:, c * LANES:(c + 1) * LANES]


def _ada_kernel(c_ref, w_ref, b_ref, o_ref):
    c = c_ref[...]
    s = c * jax.nn.sigmoid(c)
    o_ref[...] = jnp.dot(s, w_ref[...], precision=HIGHEST, preferred_element_type=F32) + b_ref[...]


def _ada(c, w_ada, b_ada):
    B, D = c.shape
    n = w_ada.shape[1] // D
    return pl.pallas_call(
        _ada_kernel,
        grid=(n,),
        in_specs=[pl.BlockSpec((B, D), lambda j: (0, 0)),
                  pl.BlockSpec((D, D), lambda j: (0, j)),
                  pl.BlockSpec((1, D), lambda j: (0, j))],
        out_specs=pl.BlockSpec((B, D), lambda j: (0, j)),
        out_shape=jax.ShapeDtypeStruct((B, n * D), F32),
        compiler_params=_params("arbitrary"),
        name="ada",
    )(c, w_ada, b_ada.reshape(1, -1))


def _bias_kernel(bk_ref, tab_ref, o_ref):
    bk = bk_ref[...]
    onehot = (lax.broadcasted_iota(jnp.int32, (N_BUCKETS, bk.shape[1]), 0) == bk).astype(F32)
    o_ref[...] = lax.dot_general(tab_ref[...], onehot, (((0,), (0,)), ((), ())),
                                 precision=HIGHEST, preferred_element_type=F32)


def _t5_buckets(rel):
    nb = N_BUCKETS // 2
    max_exact = nb // 2
    n = jnp.abs(rel)
    large = max_exact + (jnp.log(jnp.maximum(n, 1).astype(jnp.float32) / max_exact)
                         / math.log(MAX_DISTANCE / max_exact) * (nb - max_exact)).astype(jnp.int32)
    large = jnp.minimum(large, nb - 1)
    return jnp.where(rel > 0, nb, 0) + jnp.where(n < max_exact, n, large)


def _bias_strip(rel_table, S):
    L = 2 * S
    buckets = _t5_buckets(jnp.arange(L, dtype=jnp.int32) - S).reshape(1, L)
    return pl.pallas_call(
        _bias_kernel,
        out_shape=jax.ShapeDtypeStruct((N_HEADS, L), F32),
        name="bias",
    )(buckets, rel_table)


def _inproj_kernel(x_ref, mod_ref, g1_ref, w_ref, gmat_ref, qg_ref, kg_ref,
                   q_ref, k_ref, v_ref, p_ref, ga_ref, gp_ref):
    D = x_ref.shape[1]
    x = x_ref[...]
    h = x * lax.rsqrt(jnp.mean(x * x, axis=-1, keepdims=True) + EPS) * g1_ref[...]
    h = h * (1.0 + mod_ref[0, 1:2, :]) + mod_ref[0, 0:1, :]
    hb = h.astype(BF16)

    def chunk(c):
        return jnp.dot(hb, w_ref[:, c * D:(c + 1) * D], preferred_element_type=F32)

    def head_norm(y, g_ref):
        ss = jnp.dot((y * y).astype(BF16), gmat_ref[...], preferred_element_type=F32)
        return y * lax.rsqrt(ss * (1.0 / HEAD_DK) + EPS) * g_ref[...]

    q_ref[...] = head_norm(chunk(0), qg_ref).astype(BF16)
    k_ref[...] = head_norm(chunk(1), kg_ref).astype(BF16)
    v_ref[...] = chunk(2).astype(BF16)
    p_ref[...] = chunk(3)
    ga_ref[...] = jax.nn.sigmoid(chunk(4)).astype(BF16)
    gp_ref[...] = jax.nn.sigmoid(chunk(5)).astype(BF16)


def _inproj(x2, mod3, norm1_g, w_in_b, gmat, qg, kg, S, tm):
    T, D = x2.shape
    per_seq = S // tm
    row = lambda i: (i, 0)
    outs = [jax.ShapeDtypeStruct((T, D), BF16)] * 3 + [jax.ShapeDtypeStruct((T, D), F32)] \
        + [jax.ShapeDtypeStruct((T, D), BF16)] * 2
    return pl.pallas_call(
        _inproj_kernel,
        grid=(T // tm,),
        in_specs=[pl.BlockSpec((tm, D), row),
                  pl.BlockSpec((1, 6, D), lambda i: (i // per_seq, 0, 0)),
                  _const_spec((1, D)),
                  _const_spec(w_in_b.shape),
                  _const_spec((D, D)),
                  _const_spec((1, D)),
                  _const_spec((1, D))],
        out_specs=[pl.BlockSpec((tm, D), row)] * 6,
        out_shape=outs,
        compiler_params=_params("arbitrary"),
        name="inproj",
    )(x2, mod3, norm1_g, w_in_b, gmat, qg, kg)


def _attn_kernel(q_ref, k_ref, v_ref, ga_ref, win_ref, lam_ref, sg_ref, o_ref,
                 bias_ref, s1_ref, s2_ref, a_ref, rl_ref):
    tq = q_ref.shape[0]
    S = k_ref.shape[0]
    n = win_ref.shape[2]

    @pl.when(pl.program_id(2) == 0)
    def _():
        xb = jnp.broadcast_to(win_ref[0, 0], (tq, n))
        rolled = pltpu.roll(xb, n - tq, 1, stride=1, stride_axis=0)
        bias_ref[...] = rolled[:, :S]

    lp = lam_ref[...]
    lam = (jnp.exp(jnp.sum(lp[0:1] * lp[1:2], axis=-1, keepdims=True))
           - jnp.exp(jnp.sum(lp[2:3] * lp[3:4], axis=-1, keepdims=True)) + LAM_INIT)

    q = q_ref[...]
    first = lax.broadcasted_iota(jnp.int32, (1, HEAD_DV), 1) < HEAD_DK
    zero = jnp.zeros_like(q)
    k = k_ref[...]
    nt = (((1,), (1,)), ((), ()))
    s1_ref[...] = lax.dot_general(jnp.where(first, q, zero), k, nt, preferred_element_type=F32)
    s2_ref[...] = lax.dot_general(jnp.where(first, zero, q), k, nt, preferred_element_type=F32)

    def softmax_rows(i):
        rows = pl.ds(i * ATTN_ROWS, ATTN_ROWS)
        x2 = s2_ref[rows, :] + bias_ref[rows, :]
        p2 = jnp.exp2(x2 - jnp.max(x2, axis=-1, keepdims=True))
        l2 = jnp.sum(p2, axis=-1, keepdims=True)
        x1 = s1_ref[rows, :] + bias_ref[rows, :]
        p1 = jnp.exp2(x1 - jnp.max(x1, axis=-1, keepdims=True))
        l1 = jnp.sum(p1, axis=-1, keepdims=True)
        a_ref[rows, :] = (p1 - p2 * (lam * l1 / l2)).astype(BF16)
        rl_ref[rows, :] = 1.0 / l1

    for i in range(tq // ATTN_ROWS):
        softmax_rows(i)
    o = jnp.dot(a_ref[...], v_ref[...], preferred_element_type=F32) * rl_ref[...]
    o = o * lax.rsqrt(jnp.mean(o * o, axis=-1, keepdims=True) + EPS) * sg_ref[...] * (1.0 - LAM_INIT)
    o_ref[...] = (o * ga_ref[...].astype(F32)).astype(BF16)


def _attention(q, k, v, ga, win, lam_p, subln_g, B, S, tq):
    T, D = q.shape
    nq = S // tq
    n = win.shape[2]
    qmap = lambda h, i, b: (b * nq + i, h)
    kvmap = lambda h, i, b: (b, h)
    return pl.pallas_call(
        _attn_kernel,
        grid=(N_HEADS, nq, B),
        in_specs=[pl.BlockSpec((tq, HEAD_DV), qmap),
                  pl.BlockSpec((S, HEAD_DV), kvmap),
                  pl.BlockSpec((S, HEAD_DV), kvmap),
                  pl.BlockSpec((tq, HEAD_DV), qmap),
                  pl.BlockSpec((1, 1, n), lambda h, i, b: (h * nq + i, 0, 0)),
                  pl.BlockSpec((4, HEAD_DK), lambda h, i, b: (0, 0)),
                  pl.BlockSpec((1, HEAD_DV), lambda h, i, b: (0, 0))],
        out_specs=pl.BlockSpec((tq, HEAD_DV), qmap),
        out_shape=jax.ShapeDtypeStruct((T, D), BF16),
        scratch_shapes=[pltpu.VMEM((tq, S), F32), pltpu.VMEM((tq, S), F32), pltpu.VMEM((tq, S), F32),
                        pltpu.VMEM((tq, S), BF16), pltpu.VMEM((tq, 1), F32)],
        compiler_params=_params("arbitrary", "arbitrary", "arbitrary"),
        name="attn",
    )(q, k, v, ga, win, lam_p, subln_g)


def _mix_kernel(S, x_ref, am_ref, gp_ref, p_ref, pprev_ref, pnext_ref, mod_ref, pw_ref, ps_ref,
                wo_ref, g2_ref, wr_ref, x1_ref, h2r_ref, h2b_ref, lg_ref):
    tm, D = x_ref.shape
    gc = D // N_POOL_GROUPS
    halo = SUBLANES
    ne = tm + 2 * halo
    pos0 = (pl.program_id(0) % (S // tm)) * tm
    pos_e = pos0 - halo + lax.broadcasted_iota(jnp.int32, (ne, 1), 0)
    valid = (pos_e >= 0) & (pos_e < S)
    ext = jnp.concatenate([pprev_ref[...], p_ref[...], pnext_ref[...]], axis=0)
    ext = jnp.where(valid, ext, 0.0)
    pos = pos0 + lax.broadcasted_iota(jnp.int32, (tm, 1), 0)

    merged = []
    for g, w in enumerate(POOL_WINDOWS):
        half = w // 2
        e = ext[:, g * gc:(g + 1) * gc]
        sw = e
        width = 1
        while width < w:
            sw = sw + pltpu.roll(sw, width, 0)
            width *= 2
        win = pltpu.roll(sw, ne - (half - 1), 0)[halo:halo + tm] if half > 1 else sw[halo:halo + tm]
        lo = jnp.clip(pos - half, 0, S - 1)
        hi = jnp.clip(pos + half - 1, 0, S - 1)
        cnt = (hi - lo + 1).astype(F32)
        mixed = win / cnt - p_ref[:, g * gc:(g + 1) * gc]
        pooled = jnp.dot(mixed.astype(BF16), pw_ref[g], preferred_element_type=F32)
        pooled = pooled * ps_ref[:, g * gc:(g + 1) * gc]
        merged.append(am_ref[:, g * gc:(g + 1) * gc].astype(F32)
                      + gp_ref[:, g * gc:(g + 1) * gc].astype(F32) * pooled)
    merged = jnp.concatenate(merged, axis=-1).astype(BF16)
    y = jnp.dot(merged, wo_ref[...], preferred_element_type=F32)
    x1 = x_ref[...] + mod_ref[0, 2:3, :] * y
    x1_ref[...] = x1
    h2 = x1 * lax.rsqrt(jnp.mean(x1 * x1, axis=-1, keepdims=True) + EPS) * g2_ref[...]
    h2 = h2 * (1.0 + mod_ref[0, 4:5, :]) + mod_ref[0, 3:4, :]
    h2b_ref[...] = h2.astype(BF16)
    _rows_store(h2r_ref, 0, h2)
    lg_ref[...] = lax.dot_general(wr_ref[...], h2, (((1,), (1,)), ((), ())),
                                  precision=HIGHEST, preferred_element_type=F32)


def _mix(x2, am, gp, p_in, mod3, pool_w_b, pool_scale, w_out_b, norm2_g, w_router_t, S, tm):
    T, D = x2.shape
    per_seq = S // tm
    hb = tm // SUBLANES
    last = T // SUBLANES - 1
    row = lambda i: (i, 0)
    E = w_router_t.shape[0]
    return pl.pallas_call(
        functools.partial(_mix_kernel, S),
        grid=(T // tm,),
        in_specs=[pl.BlockSpec((tm, D), row),
                  pl.BlockSpec((tm, D), row),
                  pl.BlockSpec((tm, D), row),
                  pl.BlockSpec((tm, D), row),
                  pl.BlockSpec((SUBLANES, D), lambda i: (jnp.maximum(i * hb - 1, 0), 0)),
                  pl.BlockSpec((SUBLANES, D), lambda i: (jnp.minimum((i + 1) * hb, last), 0)),
                  pl.BlockSpec((1, 6, D), lambda i: (i // per_seq, 0, 0)),
                  _const_spec(pool_w_b.shape),
                  _const_spec((1, D)),
                  _const_spec((D, D)),
                  _const_spec((1, D)),
                  _const_spec((E, D))],
        out_specs=[pl.BlockSpec((tm, D), row),
                   pl.BlockSpec((tm * SUBLANES, LANES), row),
                   pl.BlockSpec((tm, D), row),
                   pl.BlockSpec((E, tm), lambda i: (0, i))],
        out_shape=[jax.ShapeDtypeStruct((T, D), F32),
                   jax.ShapeDtypeStruct((T * SUBLANES, LANES), F32),
                   jax.ShapeDtypeStruct((T, D), BF16),
                   jax.ShapeDtypeStruct((E, T), F32)],
        compiler_params=_params("arbitrary"),
        name="mix",
    )(x2, am, gp, p_in, p_in, p_in, mod3, pool_w_b, pool_scale, w_out_b, norm2_g, w_router_t)


def _route_kernel(lg_ref, br_ref, tri_ref, eidx_ref, rank_ref, wts_ref, cnt_ref):
    E, tr = lg_ref.shape
    per = E // N_EXPERT_GROUPS

    @pl.when(pl.program_id(0) == 0)
    def _():
        cnt_ref[...] = jnp.zeros_like(cnt_ref)

    scores = jax.nn.sigmoid(lg_ref[...])
    biased = scores + br_ref[...]
    b3 = biased.reshape(N_EXPERT_GROUPS, per, tr)
    io_per = lax.broadcasted_iota(jnp.int32, b3.shape, 1)
    m1 = jnp.max(b3, axis=1, keepdims=True)
    i1 = jnp.min(jnp.where(b3 == m1, io_per, per), axis=1, keepdims=True)
    m2 = jnp.max(jnp.where(io_per == i1, NEG_INF, b3), axis=1, keepdims=True)
    gs = (m1 + m2)[:, 0, :]

    io_g = lax.broadcasted_iota(jnp.int32, gs.shape, 0)
    gsel = jnp.zeros(gs.shape, jnp.bool_)
    cur = gs
    for _ in range(TOPK_GROUPS):
        gm = jnp.max(cur, axis=0, keepdims=True)
        gi = jnp.min(jnp.where(cur == gm, io_g, N_EXPERT_GROUPS), axis=0, keepdims=True)
        pick = io_g == gi
        gsel = gsel | pick
        cur = jnp.where(pick, NEG_INF, cur)
    cur = jnp.where(gsel[:, None, :], b3, NEG_INF).reshape(E, tr)

    io_e = lax.broadcasted_iota(jnp.int32, (E, tr), 0)
    idxs, raw = [], []
    sel = jnp.zeros((E, tr), jnp.bool_)
    for _ in range(TOP_K):
        m = jnp.max(cur, axis=0, keepdims=True)
        idx = jnp.min(jnp.where(cur == m, io_e, E), axis=0, keepdims=True)
        pick = io_e == idx
        idxs.append(idx)
        raw.append(jnp.sum(jnp.where(pick, scores, 0.0), axis=0, keepdims=True))
        sel = sel | pick
        cur = jnp.where(pick, NEG_INF, cur)
    raw = jnp.concatenate(raw, axis=0)
    wts_ref[...] = raw / jnp.sum(raw, axis=0, keepdims=True) * ROUTED_SCALE
    eidx_ref[...] = jnp.concatenate(idxs, axis=0)

    self_f = sel.astype(F32)
    incl = jnp.dot(sel.astype(BF16), tri_ref[...], preferred_element_type=F32)
    before = cnt_ref[...] + incl - self_f
    ranks = [jnp.sum(jnp.where(io_e == idx, before, 0.0), axis=0, keepdims=True) for idx in idxs]
    rank_ref[...] = jnp.concatenate(ranks, axis=0).astype(jnp.int32)
    cnt_ref[...] = cnt_ref[...] + jnp.sum(self_f, axis=1, keepdims=True)


def _route(logits_t, b_router, tr):
    E, T = logits_t.shape
    tri = (jnp.arange(tr)[:, None] <= jnp.arange(tr)[None, :]).astype(BF16)
    col = lambda i: (0, i)
    return pl.pallas_call(
        _route_kernel,
        grid=(T // tr,),
        in_specs=[pl.BlockSpec((E, tr), col),
                  _const_spec((E, 1)),
                  _const_spec((tr, tr))],
        out_specs=[pl.BlockSpec((TOP_K, tr), col),
                   pl.BlockSpec((TOP_K, tr), col),
                   pl.BlockSpec((TOP_K, tr), col),
                   pl.BlockSpec((E, 1), lambda i: (0, 0))],
        out_shape=[jax.ShapeDtypeStruct((TOP_K, T), jnp.int32),
                   jax.ShapeDtypeStruct((TOP_K, T), jnp.int32),
                   jax.ShapeDtypeStruct((TOP_K, T), F32),
                   jax.ShapeDtypeStruct((E, 1), F32)],
        compiler_params=_params("arbitrary"),
        name="route",
    )(logits_t, b_router.reshape(E, 1), tri)


def _dest_kernel(eidx_ref, rank_ref, base_ref, dest_ref):
    K, tr = eidx_ref.shape
    E = base_ref.shape[0]
    io_e = lax.broadcasted_iota(jnp.int32, (E, tr), 0)
    base = base_ref[...]
    rows = [jnp.sum(jnp.where(io_e == eidx_ref[k:k + 1, :], base, 0.0), axis=0, keepdims=True)
            for k in range(K)]
    dest_ref[...] = jnp.concatenate(rows, axis=0).astype(jnp.int32) + rank_ref[...]


def _dest(eidx, rank, base, tr):
    K, T = eidx.shape
    col = lambda i: (0, i)
    return pl.pallas_call(
        _dest_kernel,
        grid=(T // tr,),
        in_specs=[pl.BlockSpec((K, tr), col), pl.BlockSpec((K, tr), col), _const_spec(base.shape)],
        out_specs=pl.BlockSpec((K, tr), col),
        out_shape=jax.ShapeDtypeStruct((K, T), jnp.int32),
        compiler_params=_params("arbitrary"),
        name="dest",
    )(eidx, rank, base)


def _dispatch_kernel(lastblk_ref, dest_ref, h_ref, xs_ref, zbuf_ref, sem):
    C = SUBLANES
    td = h_ref.shape[0] // C
    RC = zbuf_ref.shape[0]

    def slab(ref, r):
        return ref.at[pl.ds(pl.multiple_of(r * C, C), C)]

    @pl.when(pl.program_id(0) == 0)
    def _():
        zbuf_ref[...] = jnp.zeros_like(zbuf_ref)

        def zcopy(e):
            start = pl.multiple_of(lastblk_ref[e] * RC, RC)
            return pltpu.make_async_copy(zbuf_ref, xs_ref.at[pl.ds(start, RC)], sem)

        def start(e, c):
            @pl.when(lastblk_ref[e] >= 0)
            def _():
                zcopy(e).start()
            return c

        def wait(e, c):
            @pl.when(lastblk_ref[e] >= 0)
            def _():
                zcopy(e).wait()
            return c

        lax.fori_loop(0, N_EXPERTS, start, 0)
        lax.fori_loop(0, N_EXPERTS, wait, 0)

    def row_copy(t, k):
        return pltpu.make_async_copy(slab(h_ref, t), slab(xs_ref, dest_ref[k, t]), sem)

    def start_rows(t, c):
        for k in range(TOP_K):
            row_copy(t, k).start()
        return c

    def wait_rows(t, c):
        for k in range(TOP_K):
            row_copy(t, k).wait()
        return c

    lax.fori_loop(0, td, start_rows, 0)
    lax.fori_loop(0, td, wait_rows, 0)


def _dispatch(lastblk, dest, h2r, n_pad, td):
    C = SUBLANES
    T = h2r.shape[0] // C
    return pl.pallas_call(
        _dispatch_kernel,
        grid_spec=pltpu.PrefetchScalarGridSpec(
            num_scalar_prefetch=1,
            grid=(T // td,),
            in_specs=[pl.BlockSpec((TOP_K, td), lambda i, lb: (0, i), memory_space=pltpu.SMEM),
                      pl.BlockSpec((td * C, LANES), lambda i, lb: (i, 0))],
            out_specs=pl.BlockSpec(memory_space=pl.ANY),
            scratch_shapes=[pltpu.VMEM((MOE_BLOCK * C, LANES), F32), pltpu.SemaphoreType.DMA]),
        out_shape=jax.ShapeDtypeStruct((n_pad * C, LANES), F32),
        compiler_params=_params("arbitrary"),
        name="dispatch",
    )(lastblk, dest, h2r)


def _expert_kernel(be_ref, nused_ref, xs_ref, wg_ref, wu_ref, wd_ref, ys_ref):
    R = xs_ref.shape[0] // SUBLANES

    @pl.when(pl.program_id(0) < nused_ref[0])
    def _():
        x = _rows_load(xs_ref, 0, R).astype(BF16)
        g = jnp.dot(x, wg_ref[0].astype(BF16), preferred_element_type=F32)
        u = jnp.dot(x, wu_ref[0].astype(BF16), preferred_element_type=F32)
        hmid = (g * jax.nn.sigmoid(g) * u).astype(BF16)
        y = jnp.dot(hmid, wd_ref[0].astype(BF16), preferred_element_type=F32)
        _rows_store(ys_ref, 0, y)

    @pl.when(pl.program_id(0) >= nused_ref[0])
    def _():
        ys_ref[...] = jnp.zeros_like(ys_ref)


def _experts(blk_expert, nused, xs, w_eg, w_eu, w_ed):
    RC = MOE_BLOCK * SUBLANES
    n_blocks = xs.shape[0] // RC
    _, D, DE = w_eg.shape
    blk = lambda j, be, nu: jnp.minimum(j, nu[0] - 1)
    wmap = lambda j, be, nu: (be[blk(j, be, nu)], 0, 0)
    return pl.pallas_call(
        _expert_kernel,
        grid_spec=pltpu.PrefetchScalarGridSpec(
            num_scalar_prefetch=2,
            grid=(n_blocks,),
            in_specs=[pl.BlockSpec((RC, LANES), lambda j, be, nu: (blk(j, be, nu), 0)),
                      pl.BlockSpec((1, D, DE), wmap),
                      pl.BlockSpec((1, D, DE), wmap),
                      pl.BlockSpec((1, DE, D), wmap)],
            out_specs=pl.BlockSpec((RC, LANES), lambda j, be, nu: (j, 0))),
        out_shape=jax.ShapeDtypeStruct(xs.shape, F32),
        compiler_params=_params("arbitrary"),
        name="expert",
    )(blk_expert, nused, xs, w_eg, w_eu, w_ed)


def _combine_kernel(dest_ref, wt_ref, ys_ref, x1_ref, h2b_ref, mod_ref, wsg_ref, wsu_ref, wsd_ref,
                    o_ref, buf_ref, sem):
    tc = x1_ref.shape[0]
    C = SUBLANES

    def slab(ref, r):
        return ref.at[pl.ds(pl.multiple_of(r * C, C), C)]

    def row_copy(t, k):
        return pltpu.make_async_copy(slab(ys_ref, dest_ref[k, t]), slab(buf_ref, k * tc + t), sem)

    def start_rows(t, c):
        for k in range(TOP_K):
            row_copy(t, k).start()
        return c

    def wait_rows(t, c):
        for k in range(TOP_K):
            row_copy(t, k).wait()
        return c

    lax.fori_loop(0, tc, start_rows, 0)

    h = h2b_ref[...]
    g = jnp.dot(h, wsg_ref[...], preferred_element_type=F32)
    u = jnp.dot(h, wsu_ref[...], preferred_element_type=F32)
    shared = jnp.dot((g * jax.nn.sigmoid(g) * u).astype(BF16), wsd_ref[...], preferred_element_type=F32)

    lax.fori_loop(0, tc, wait_rows, 0)

    wt = wt_ref[...]
    acc = None
    for k in range(TOP_K):
        term = _rows_load(buf_ref, k * tc, tc) * wt[:, k:k + 1]
        acc = term if acc is None else acc + term
    o_ref[...] = x1_ref[...] + mod_ref[0, 5:6, :] * (acc + shared)


def _combine(dest, wts_t, ys, x1, h2b, mod3, wsg, wsu, wsd, S, tc):
    T, D = x1.shape
    per_seq = S // tc
    row = lambda i: (i, 0)
    return pl.pallas_call(
        _combine_kernel,
        grid=(T // tc,),
        in_specs=[pl.BlockSpec((TOP_K, tc), lambda i: (0, i), memory_space=pltpu.SMEM),
                  pl.BlockSpec((tc, TOP_K), row),
                  pl.BlockSpec(memory_space=pl.ANY),
                  pl.BlockSpec((tc, D), row),
                  pl.BlockSpec((tc, D), row),
                  pl.BlockSpec((1, 6, D), lambda i: (i // per_seq, 0, 0)),
                  _const_spec(wsg.shape),
                  _const_spec(wsu.shape),
                  _const_spec(wsd.shape)],
        out_specs=pl.BlockSpec((tc, D), row),
        out_shape=jax.ShapeDtypeStruct((T, D), F32),
        scratch_shapes=[pltpu.VMEM((TOP_K * tc * SUBLANES, LANES), F32), pltpu.SemaphoreType.DMA],
        compiler_params=_params("arbitrary"),
        name="combine",
    )(dest, wts_t, ys, x1, h2b, mod3, wsg, wsu, wsd)


def _tile(n, want):
    t = min(n, want)
    while n % t:
        t //= 2
    return t


def kernel(x, c, rel_bias_table, w_ada, b_ada, norm1_g, w_in, q_norm_g, k_norm_g, lambda_q1, lambda_k1,
           lambda_q2, lambda_k2, subln_g, pool_w, pool_scale, w_out, norm2_g, w_router, b_router,
           w_exp_gate, w_exp_up, w_exp_down, w_sh_gate, w_sh_up, w_sh_down):
    B, S, D = x.shape
    T = B * S
    depth = w_ada.shape[0]
    assert depth == 1, "LAM_INIT and the single-layer pipeline assume depth 1"
    assert D == N_HEADS * HEAD_DV and S % LANES == 0
    l = 0
    tm = _tile(S, 512)
    tq = _tile(S, 512)

    x2 = x.reshape(T, D)
    mod3 = _ada(c, w_ada[l], b_ada[l]).reshape(B, 6, D)

    grp = jnp.arange(D) // HEAD_DK
    gmat = (grp[:, None] == grp[None, :]).astype(BF16)
    qg = (jnp.tile(q_norm_g[l], D // HEAD_DK) * (HEAD_DK ** -0.5 * LOG2E)).reshape(1, D)
    kg = jnp.tile(k_norm_g[l], D // HEAD_DK).reshape(1, D)
    q, k, v, p_in, ga, gp = _inproj(x2, mod3, norm1_g[l].reshape(1, D), w_in[l].astype(BF16),
                                    gmat, qg, kg, S, tm)

    strip = _bias_strip(rel_bias_table, S) * LOG2E
    nq = S // tq
    n = S + tq
    win = jnp.stack([strip[:, S - (i + 1) * tq: S - (i + 1) * tq + n] for i in range(nq)], axis=1)
    win = win.reshape(N_HEADS * nq, 1, n)
    lam_p = jnp.stack([lambda_q1[l], lambda_k1[l], lambda_q2[l], lambda_k2[l]])
    am = _attention(q, k, v, ga, win, lam_p, subln_g[l].reshape(1, HEAD_DV), B, S, tq)

    x1, h2r, h2b, logits_t = _mix(x2, am, gp, p_in, mod3, pool_w[l].astype(BF16),
                                  pool_scale[l].reshape(1, D), w_out[l].astype(BF16),
                                  norm2_g[l].reshape(1, D), w_router[l].T, S, tm)

    eidx, rank, wts, counts = _route(logits_t, b_router[l], _tile(T, 512))

    cnt = counts[:, 0].astype(jnp.int32)
    nblk = (cnt + MOE_BLOCK - 1) // MOE_BLOCK
    blk_end = jnp.cumsum(nblk)
    base = ((blk_end - nblk) * MOE_BLOCK).astype(F32).reshape(N_EXPERTS, 1)
    lastblk = jnp.where(nblk > 0, blk_end - 1, -1).astype(jnp.int32)
    n_blocks = -(-(T * TOP_K) // MOE_BLOCK) + N_EXPERTS
    blk_expert = jnp.clip(jnp.searchsorted(blk_end, jnp.arange(n_blocks), side='right'),
                          0, N_EXPERTS - 1).astype(jnp.int32)
    nused = blk_end[-1:].astype(jnp.int32)

    dest = _dest(eidx, rank, base, _tile(T, 512))
    xs = _dispatch(lastblk, dest, h2r, n_blocks * MOE_BLOCK, _tile(S, 256))
    ys = _experts(blk_expert, nused, xs, w_exp_gate[l], w_exp_up[l], w_exp_down[l])
    out = _combine(dest, wts.T, ys, x1, h2b, mod3, w_sh_gate[l].astype(BF16), w_sh_up[l].astype(BF16),
                   w_sh_down[l].astype(BF16), S, _tile(S, 128))
    return out.reshape(B, S, D)
```

```python
import functools
import math

import jax
import jax.numpy as jnp
from jax import lax
from jax.experimental import pallas as pl
from jax.experimental.pallas import tpu as pltpu

N_HEADS = 8
HEAD_DK = 64
HEAD_DV = 2 * HEAD_DK
N_BUCKETS = 32
MAX_DISTANCE = 128
POOL_WINDOWS = (2, 4, 8, 16)
N_POOL_GROUPS = 4
N_EXPERTS = 256
TOP_K = 8
N_EXPERT_GROUPS = 8
TOPK_GROUPS = 4
ROUTED_SCALE = 2.5
MOE_BLOCK = 256
EPS = 1e-6
LAM_INIT = 0.8 - 0.6 * math.exp(-0.3 * 0)
LOG2E = math.log2(math.e)
KEY_TILE = 512

LANES = 128
SUBLANES = 8
VMEM_LIMIT = 56 * 1024 * 1024

F32 = jnp.float32
BF16 = jnp.bfloat16
HIGHEST = lax.Precision.HIGHEST
NEG_INF = float("-inf")


def _params(*sem):
    return pltpu.CompilerParams(dimension_semantics=sem, vmem_limit_bytes=VMEM_LIMIT)


def _const_spec(shape):
    nd = len(shape)
    return pl.BlockSpec(shape, lambda *_: (0,) * nd, pipeline_mode=pl.Buffered(1))


def _rows_load(ref, first, n, chunks=SUBLANES):
    return jnp.concatenate([ref[pl.ds(first * chunks + c, n, stride=chunks), :] for c in range(chunks)],
                           axis=-1)


def _rows_store(ref, first, val, chunks=SUBLANES):
    n = val.shape[0]
    for c in range(chunks):
        ref[pl.ds(first * chunks + c, n, stride=chunks), :] = val[:, c * LANES:(c + 1) * LANES]


def _ada_kernel(c_ref, w_ref, b_ref, o_ref):
    c = c_ref[...]
    s = c * jax.nn.sigmoid(c)
    o_ref[...] = jnp.dot(s, w_ref[...], precision=HIGHEST, preferred_element_type=F32) + b_ref[...]


def _ada(c, w_ada, b_ada):
    B, D = c.shape
    n = w_ada.shape[1] // D
    return pl.pallas_call(
        _ada_kernel,
        grid=(n,),
        in_specs=[pl.BlockSpec((B, D), lambda j: (0, 0)),
                  pl.BlockSpec((D, D), lambda j: (0, j)),
                  pl.BlockSpec((1, D), lambda j: (0, j))],
        out_specs=pl.BlockSpec((B, D), lambda j: (0, j)),
        out_shape=jax.ShapeDtypeStruct((B, n * D), F32),
        compiler_params=_params("arbitrary"),
        name="ada",
    )(c, w_ada, b_ada.reshape(1, -1))


def _bias_kernel(bk_ref, tab_ref, o_ref):
    bk = bk_ref[...]
    onehot = (lax.broadcasted_iota(jnp.int32, (N_BUCKETS, bk.shape[1]), 0) == bk).astype(F32)
    o_ref[...] = lax.dot_general(tab_ref[...], onehot, (((0,), (0,)), ((), ())),
                                 precision=HIGHEST, preferred_element_type=F32)


def _t5_buckets(rel):
    nb = N_BUCKETS // 2
    max_exact = nb // 2
    n = jnp.abs(rel)
    large = max_exact + (jnp.log(jnp.maximum(n, 1).astype(jnp.float32) / max_exact)
                         / math.log(MAX_DISTANCE / max_exact) * (nb - max_exact)).astype(jnp.int32)
    large = jnp.minimum(large, nb - 1)
    return jnp.where(rel > 0, nb, 0) + jnp.where(n < max_exact, n, large)


def _bias_strip(rel_table, S):
    L = 2 * S
    buckets = _t5_buckets(jnp.arange(L, dtype=jnp.int32) - S).reshape(1, L)
    return pl.pallas_call(
        _bias_kernel,
        out_shape=jax.ShapeDtypeStruct((N_HEADS, L), F32),
        name="bias",
    )(buckets, rel_table)


def _inproj_kernel(x_ref, mod_ref, g1_ref, w_ref, gmat_ref, qg_ref, kg_ref,
                   q_ref, kt_ref, v_ref, p_ref, ga_ref, gp_ref):
    D = x_ref.shape[1]
    x = x_ref[...]
    h = x * lax.rsqrt(jnp.mean(x * x, axis=-1, keepdims=True) + EPS) * g1_ref[...]
    h = h * (1.0 + mod_ref[0, 1:2, :]) + mod_ref[0, 0:1, :]
    hb = h.astype(BF16)

    def chunk(c):
        return jnp.dot(hb, w_ref[:, c * D:(c + 1) * D], preferred_element_type=F32)

    def head_norm(y, g_ref):
        ss = jnp.dot((y * y).astype(BF16), gmat_ref[...], preferred_element_type=F32)
        return y * lax.rsqrt(ss * (1.0 / HEAD_DK) + EPS) * g_ref[...]

    q_ref[...] = head_norm(chunk(0), qg_ref).astype(BF16)
    kn = head_norm(chunk(1), kg_ref)
    for hd in range(N_HEADS):
        for t in range(x_ref.shape[0] // KEY_TILE):
            blk = kn[t * KEY_TILE:(t + 1) * KEY_TILE, hd * HEAD_DV:(hd + 1) * HEAD_DV]
            kt_ref[0, hd, t] = blk.T.astype(BF16)
    v_ref[...] = chunk(2).astype(BF16)
    p_ref[...] = chunk(3)
    ga_ref[...] = jax.nn.sigmoid(chunk(4)).astype(BF16)
    gp_ref[...] = jax.nn.sigmoid(chunk(5)).astype(BF16)


def _inproj(x2, mod3, norm1_g, w_in_b, gmat, qg, kg, S, tm):
    T, D = x2.shape
    per_seq = S // tm
    row = lambda i: (i, 0)
    tok = jax.ShapeDtypeStruct((T, D), BF16)
    kt = jax.ShapeDtypeStruct((T // S, N_HEADS, S // KEY_TILE, HEAD_DV, KEY_TILE), BF16)
    outs = [tok, kt, tok, jax.ShapeDtypeStruct((T, D), F32), tok, tok]
    tok_spec = pl.BlockSpec((tm, D), row)
    kt_spec = pl.BlockSpec((1, N_HEADS, tm // KEY_TILE, HEAD_DV, KEY_TILE),
                           lambda i: (i // per_seq, 0, i % per_seq, 0, 0))
    return pl.pallas_call(
        _inproj_kernel,
        grid=(T // tm,),
        in_specs=[pl.BlockSpec((tm, D), row),
                  pl.BlockSpec((1, 6, D), lambda i: (i // per_seq, 0, 0)),
                  _const_spec((1, D)),
                  _const_spec(w_in_b.shape),
                  _const_spec((D, D)),
                  _const_spec((1, D)),
                  _const_spec((1, D))],
        out_specs=[tok_spec, kt_spec, tok_spec, tok_spec, tok_spec, tok_spec],
        out_shape=outs,
        compiler_params=_params("arbitrary"),
        name="inproj",
    )(x2, mod3, norm1_g, w_in_b, gmat, qg, kg)


def _attn_kernel(q_ref, kt_ref, v_ref, ga_ref, win_ref, lam_ref, sg_ref, o_ref,
                 bias_ref, x_ref, m_ref, acc_ref):
    tq = q_ref.shape[0]
    n_kt, _, kt = kt_ref.shape[2:]
    n = win_ref.shape[2]

    @pl.when(pl.program_id(2) == 0)
    def _():
        xb = jnp.broadcast_to(win_ref[0, 0], (tq, n))
        rolled = pltpu.roll(xb, n - tq, 1, stride=1, stride_axis=0)
        for j in range(n_kt):
            bias_ref[j] = rolled[:, j * kt:(j + 1) * kt]

    lp = lam_ref[...]
    lam = (jnp.exp(jnp.sum(lp[0:1] * lp[1:2], axis=-1, keepdims=True))
           - jnp.exp(jnp.sum(lp[2:3] * lp[3:4], axis=-1, keepdims=True)) + LAM_INIT)

    q = q_ref[...]
    first = lax.broadcasted_iota(jnp.int32, (1, HEAD_DV), 1) < HEAD_DK
    zero = jnp.zeros_like(q)
    qq = jnp.concatenate([jnp.where(first, q, zero), jnp.where(first, zero, q)], axis=0)

    for j in range(n_kt):
        b = bias_ref[j]
        x = jnp.dot(qq, kt_ref[0, 0, j], preferred_element_type=F32) + jnp.concatenate([b, b], axis=0)
        x_ref[j] = x
        t = x[:, 0:LANES]
        for c in range(1, kt // LANES):
            t = jnp.maximum(t, x[:, c * LANES:(c + 1) * LANES])
        m_ref[...] = t if j == 0 else jnp.maximum(m_ref[...], t)
    m = jnp.max(m_ref[...], axis=-1, keepdims=True)

    ones_col = (lax.broadcasted_iota(jnp.int32, (kt, LANES), 1) == 0).astype(BF16)
    mb = jnp.broadcast_to(m, (2 * tq, kt))
    for j in range(n_kt):
        p = jnp.exp2((x_ref[j] - mb).astype(BF16))
        v_ext = jnp.concatenate([v_ref[j * kt:(j + 1) * kt, :], ones_col], axis=-1)
        pv = jnp.concatenate([jnp.dot(p[:tq], v_ext, preferred_element_type=F32),
                              jnp.dot(p[tq:], v_ext, preferred_element_type=F32)], axis=0)
        acc_ref[...] = pv if j == 0 else acc_ref[...] + pv
    acc = acc_ref[...]
    o1, l1 = acc[:tq, :HEAD_DV], acc[:tq, HEAD_DV:HEAD_DV + 1]
    o2, l2 = acc[tq:, :HEAD_DV], acc[tq:, HEAD_DV:HEAD_DV + 1]
    o = o1 / l1 - lam * (o2 / l2)
    o = o * lax.rsqrt(jnp.mean(o * o, axis=-1, keepdims=True) + EPS) * sg_ref[...] * (1.0 - LAM_INIT)
    o_ref[...] = (o * ga_ref[...].astype(F32)).astype(BF16)


def _attention(q, kt, v, ga, win, lam_p, subln_g, B, S, tq):
    T, D = q.shape
    nq = S // tq
    n = win.shape[2]
    n_kt = S // KEY_TILE
    qmap = lambda h, i, b: (b * nq + i, h)
    return pl.pallas_call(
        _attn_kernel,
        grid=(N_HEADS, nq, B),
        in_specs=[pl.BlockSpec((tq, HEAD_DV), qmap),
                  pl.BlockSpec((1, 1, n_kt, HEAD_DV, KEY_TILE), lambda h, i, b: (b, h, 0, 0, 0)),
                  pl.BlockSpec((S, HEAD_DV), lambda h, i, b: (b, h)),
                  pl.BlockSpec((tq, HEAD_DV), qmap),
                  pl.BlockSpec((1, 1, n), lambda h, i, b: (h * nq + i, 0, 0)),
                  pl.BlockSpec((4, HEAD_DK), lambda h, i, b: (0, 0)),
                  pl.BlockSpec((1, HEAD_DV), lambda h, i, b: (0, 0))],
        out_specs=pl.BlockSpec((tq, HEAD_DV), qmap),
        out_shape=jax.ShapeDtypeStruct((T, D), BF16),
        scratch_shapes=[pltpu.VMEM((n_kt, tq, KEY_TILE), F32),
                        pltpu.VMEM((n_kt, 2 * tq, KEY_TILE), F32),
                        pltpu.VMEM((2 * tq, LANES), F32),
                        pltpu.VMEM((2 * tq, HEAD_DV + LANES), F32)],
        compiler_params=_params("arbitrary", "arbitrary", "arbitrary"),
        name="attn",
    )(q, kt, v, ga, win, lam_p, subln_g)


def _mix_kernel(S, x_ref, am_ref, gp_ref, p_ref, pprev_ref, pnext_ref, mod_ref, pw_ref, ps_ref,
                wo_ref, g2_ref, wr_ref, x1_ref, h2r_ref, h2b_ref, lg_ref):
    tm, D = x_ref.shape
    gc = D // N_POOL_GROUPS
    halo = SUBLANES
    ne = tm + 2 * halo
    pos0 = (pl.program_id(0) % (S // tm)) * tm
    pos_e = pos0 - halo + lax.broadcasted_iota(jnp.int32, (ne, 1), 0)
    valid = (pos_e >= 0) & (pos_e < S)
    ext = jnp.concatenate([pprev_ref[...], p_ref[...], pnext_ref[...]], axis=0)
    ext = jnp.where(valid, ext, 0.0)
    pos = pos0 + lax.broadcasted_iota(jnp.int32, (tm, 1), 0)

    merged = []
    for g, w in enumerate(POOL_WINDOWS):
        half = w // 2
        e = ext[:, g * gc:(g + 1) * gc]
        sw = e
        width = 1
        while width < w:
            sw = sw + pltpu.roll(sw, width, 0)
            width *= 2
        win = pltpu.roll(sw, ne - (half - 1), 0)[halo:halo + tm] if half > 1 else sw[halo:halo + tm]
        lo = jnp.clip(pos - half, 0, S - 1)
        hi = jnp.clip(pos + half - 1, 0, S - 1)
        cnt = (hi - lo + 1).astype(F32)
        mixed = win / cnt - p_ref[:, g * gc:(g + 1) * gc]
        pooled = jnp.dot(mixed.astype(BF16), pw_ref[g], preferred_element_type=F32)
        pooled = pooled * ps_ref[:, g * gc:(g + 1) * gc]
        merged.append(am_ref[:, g * gc:(g + 1) * gc].astype(F32)
                      + gp_ref[:, g * gc:(g + 1) * gc].astype(F32) * pooled)
    merged = jnp.concatenate(merged, axis=-1).astype(BF16)
    y = jnp.dot(merged, wo_ref[...], preferred_element_type=F32)
    x1 = x_ref[...] + mod_ref[0, 2:3, :] * y
    x1_ref[...] = x1
    h2 = x1 * lax.rsqrt(jnp.mean(x1 * x1, axis=-1, keepdims=True) + EPS) * g2_ref[...]
    h2 = h2 * (1.0 + mod_ref[0, 4:5, :]) + mod_ref[0, 3:4, :]
    h2b_ref[...] = h2.astype(BF16)
    _rows_store(h2r_ref, 0, h2)
    lg_ref[...] = lax.dot_general(wr_ref[...], h2, (((1,), (1,)), ((), ())),
                                  precision=HIGHEST, preferred_element_type=F32)


def _mix(x2, am, gp, p_in, mod3, pool_w_b, pool_scale, w_out_b, norm2_g, w_router_t, S, tm):
    T, D = x2.shape
    per_seq = S // tm
    hb = tm // SUBLANES
    last = T // SUBLANES - 1
    row = lambda i: (i, 0)
    E = w_router_t.shape[0]
    return pl.pallas_call(
        functools.partial(_mix_kernel, S),
        grid=(T // tm,),
        in_specs=[pl.BlockSpec((tm, D), row),
                  pl.BlockSpec((tm, D), row),
                  pl.BlockSpec((tm, D), row),
                  pl.BlockSpec((tm, D), row),
                  pl.BlockSpec((SUBLANES, D), lambda i: (jnp.maximum(i * hb - 1, 0), 0)),
                  pl.BlockSpec((SUBLANES, D), lambda i: (jnp.minimum((i + 1) * hb, last), 0)),
                  pl.BlockSpec((1, 6, D), lambda i: (i // per_seq, 0, 0)),
                  _const_spec(pool_w_b.shape),
                  _const_spec((1, D)),
                  _const_spec((D, D)),
                  _const_spec((1, D)),
                  _const_spec((E, D))],
        out_specs=[pl.BlockSpec((tm, D), row),
                   pl.BlockSpec((tm * SUBLANES, LANES), row),
                   pl.BlockSpec((tm, D), row),
                   pl.BlockSpec((E, tm), lambda i: (0, i))],
        out_shape=[jax.ShapeDtypeStruct((T, D), F32),
                   jax.ShapeDtypeStruct((T * SUBLANES, LANES), F32),
                   jax.ShapeDtypeStruct((T, D), BF16),
                   jax.ShapeDtypeStruct((E, T), F32)],
        compiler_params=_params("arbitrary"),
        name="mix",
    )(x2, am, gp, p_in, p_in, p_in, mod3, pool_w_b, pool_scale, w_out_b, norm2_g, w_router_t)


def _route_kernel(lg_ref, br_ref, tri_ref, eidx_ref, rank_ref, wts_ref, cnt_ref):
    E, tr = lg_ref.shape
    per = E // N_EXPERT_GROUPS

    @pl.when(pl.program_id(0) == 0)
    def _():
        cnt_ref[...] = jnp.zeros_like(cnt_ref)

    scores = jax.nn.sigmoid(lg_ref[...])
    biased = scores + br_ref[...]
    b3 = biased.reshape(N_EXPERT_GROUPS, per, tr)
    io_per = lax.broadcasted_iota(jnp.int32, b3.shape, 1)
    m1 = jnp.max(b3, axis=1, keepdims=True)
    i1 = jnp.min(jnp.where(b3 == m1, io_per, per), axis=1, keepdims=True)
    m2 = jnp.max(jnp.where(io_per == i1, NEG_INF, b3), axis=1, keepdims=True)
    gs = (m1 + m2)[:, 0, :]

    io_g = lax.broadcasted_iota(jnp.int32, gs.shape, 0)
    gsel = jnp.zeros(gs.shape, jnp.bool_)
    cur = gs
    for _ in range(TOPK_GROUPS):
        gm = jnp.max(cur, axis=0, keepdims=True)
        gi = jnp.min(jnp.where(cur == gm, io_g, N_EXPERT_GROUPS), axis=0, keepdims=True)
        pick = io_g == gi
        gsel = gsel | pick
        cur = jnp.where(pick, NEG_INF, cur)
    cur = jnp.where(gsel[:, None, :], b3, NEG_INF).reshape(E, tr)

    io_e = lax.broadcasted_iota(jnp.int32, (E, tr), 0)
    idxs, raw = [], []
    sel = jnp.zeros((E, tr), jnp.bool_)
    for _ in range(TOP_K):
        m = jnp.max(cur, axis=0, keepdims=True)
        idx = jnp.min(jnp.where(cur == m, io_e, E), axis=0, keepdims=True)
        pick = io_e == idx
        idxs.append(idx)
        raw.append(jnp.sum(jnp.where(pick, scores, 0.0), axis=0, keepdims=True))
        sel = sel | pick
        cur = jnp.where(pick, NEG_INF, cur)
    raw = jnp.concatenate(raw, axis=0)
    wts_ref[...] = raw / jnp.sum(raw, axis=0, keepdims=True) * ROUTED_SCALE
    eidx_ref[...] = jnp.concatenate(idxs, axis=0)

    self_f = sel.astype(F32)
    incl = jnp.dot(sel.astype(BF16), tri_ref[...], preferred_element_type=F32)
    before = cnt_ref[...] + incl - self_f
    ranks = [jnp.sum(jnp.where(io_e == idx, before, 0.0), axis=0, keepdims=True) for idx in idxs]
    rank_ref[...] = jnp.concatenate(ranks, axis=0).astype(jnp.int32)
    cnt_ref[...] = cnt_ref[...] + jnp.sum(self_f, axis=1, keepdims=True)


def _route(logits_t, b_router, tr):
    E, T = logits_t.shape
    tri = (jnp.arange(tr)[:, None] <= jnp.arange(tr)[None, :]).astype(BF16)
    col = lambda i: (0, i)
    return pl.pallas_call(
        _route_kernel,
        grid=(T // tr,),
        in_specs=[pl.BlockSpec((E, tr), col),
                  _const_spec((E, 1)),
                  _const_spec((tr, tr))],
        out_specs=[pl.BlockSpec((TOP_K, tr), col),
                   pl.BlockSpec((TOP_K, tr), col),
                   pl.BlockSpec((TOP_K, tr), col),
                   pl.BlockSpec((E, 1), lambda i: (0, 0))],
        out_shape=[jax.ShapeDtypeStruct((TOP_K, T), jnp.int32),
                   jax.ShapeDtypeStruct((TOP_K, T), jnp.int32),
                   jax.ShapeDtypeStruct((TOP_K, T), F32),
                   jax.ShapeDtypeStruct((E, 1), F32)],
        compiler_params=_params("arbitrary"),
        name="route",
    )(logits_t, b_router.reshape(E, 1), tri)


def _dest_kernel(eidx_ref, rank_ref, base_ref, dest_ref):
    K, tr = eidx_ref.shape
    E = base_ref.shape[0]
    io_e = lax.broadcasted_iota(jnp.int32, (E, tr), 0)
    base = base_ref[...]
    rows = [jnp.sum(jnp.where(io_e == eidx_ref[k:k + 1, :], base, 0.0), axis=0, keepdims=True)
            for k in range(K)]
    dest_ref[...] = jnp.concatenate(rows, axis=0).astype(jnp.int32) + rank_ref[...]


def _dest(eidx, rank, base, tr):
    K, T = eidx.shape
    col = lambda i: (0, i)
    return pl.pallas_call(
        _dest_kernel,
        grid=(T // tr,),
        in_specs=[pl.BlockSpec((K, tr), col), pl.BlockSpec((K, tr), col), _const_spec(base.shape)],
        out_specs=pl.BlockSpec((K, tr), col),
        out_shape=jax.ShapeDtypeStruct((K, T), jnp.int32),
        compiler_params=_params("arbitrary"),
        name="dest",
    )(eidx, rank, base)


def _dispatch_kernel(lastblk_ref, dest_ref, h_ref, xs_ref, zbuf_ref, sem):
    C = SUBLANES
    td = h_ref.shape[0] // C
    RC = zbuf_ref.shape[0]

    def slab(ref, r):
        return ref.at[pl.ds(pl.multiple_of(r * C, C), C)]

    @pl.when(pl.program_id(0) == 0)
    def _():
        zbuf_ref[...] = jnp.zeros_like(zbuf_ref)

        def zcopy(e):
            start = pl.multiple_of(lastblk_ref[e] * RC, RC)
            return pltpu.make_async_copy(zbuf_ref, xs_ref.at[pl.ds(start, RC)], sem)

        def start(e, c):
            @pl.when(lastblk_ref[e] >= 0)
            def _():
                zcopy(e).start()
            return c

        def wait(e, c):
            @pl.when(lastblk_ref[e] >= 0)
            def _():
                zcopy(e).wait()
            return c

        lax.fori_loop(0, N_EXPERTS, start, 0)
        lax.fori_loop(0, N_EXPERTS, wait, 0)

    def row_copy(t, k):
        return pltpu.make_async_copy(slab(h_ref, t), slab(xs_ref, dest_ref[k, t]), sem)

    def start_rows(t, c):
        for k in range(TOP_K):
            row_copy(t, k).start(priority=k % 2)
        return c

    def wait_rows(t, c):
        for k in range(TOP_K):
            row_copy(t, k).wait()
        return c

    lax.fori_loop(0, td, start_rows, 0)
    lax.fori_loop(0, td, wait_rows, 0)


def _dispatch(lastblk, dest, h2r, n_pad, td):
    C = SUBLANES
    T = h2r.shape[0] // C
    return pl.pallas_call(
        _dispatch_kernel,
        grid_spec=pltpu.PrefetchScalarGridSpec(
            num_scalar_prefetch=1,
            grid=(T // td,),
            in_specs=[pl.BlockSpec((TOP_K, td), lambda i, lb: (0, i), memory_space=pltpu.SMEM),
                      pl.BlockSpec((td * C, LANES), lambda i, lb: (i, 0))],
            out_specs=pl.BlockSpec(memory_space=pl.ANY),
            scratch_shapes=[pltpu.VMEM((MOE_BLOCK * C, LANES), F32), pltpu.SemaphoreType.DMA]),
        out_shape=jax.ShapeDtypeStruct((n_pad * C, LANES), F32),
        compiler_params=_params("arbitrary"),
        name="dispatch",
    )(lastblk, dest, h2r)


def _expert_kernel(be_ref, nused_ref, xs_ref, wg_ref, wu_ref, wd_ref, ys_ref):
    R = xs_ref.shape[0] // SUBLANES

    @pl.when(pl.program_id(0) < nused_ref[0])
    def _():
        x = _rows_load(xs_ref, 0, R).astype(BF16)
        g = jnp.dot(x, wg_ref[0].astype(BF16), preferred_element_type=F32)
        u = jnp.dot(x, wu_ref[0].astype(BF16), preferred_element_type=F32)
        hmid = (g * jax.nn.sigmoid(g) * u).astype(BF16)
        y = jnp.dot(hmid, wd_ref[0].astype(BF16), preferred_element_type=F32)
        _rows_store(ys_ref, 0, y)

    @pl.when(pl.program_id(0) >= nused_ref[0])
    def _():
        ys_ref[...] = jnp.zeros_like(ys_ref)


def _experts(blk_expert, nused, xs, w_eg, w_eu, w_ed):
    RC = MOE_BLOCK * SUBLANES
    n_blocks = xs.shape[0] // RC
    _, D, DE = w_eg.shape
    blk = lambda j, be, nu: jnp.minimum(j, nu[0] - 1)
    wmap = lambda j, be, nu: (be[blk(j, be, nu)], 0, 0)
    return pl.pallas_call(
        _expert_kernel,
        grid_spec=pltpu.PrefetchScalarGridSpec(
            num_scalar_prefetch=2,
            grid=(n_blocks,),
            in_specs=[pl.BlockSpec((RC, LANES), lambda j, be, nu: (blk(j, be, nu), 0)),
                      pl.BlockSpec((1, D, DE), wmap),
                      pl.BlockSpec((1, D, DE), wmap),
                      pl.BlockSpec((1, DE, D), wmap)],
            out_specs=pl.BlockSpec((RC, LANES), lambda j, be, nu: (j, 0))),
        out_shape=jax.ShapeDtypeStruct(xs.shape, F32),
        compiler_params=_params("arbitrary"),
        name="expert",
    )(blk_expert, nused, xs, w_eg, w_eu, w_ed)


def _combine_kernel(dest_ref, wt_ref, ys_ref, x1_ref, h2b_ref, mod_ref, wsg_ref, wsu_ref, wsd_ref,
                    o_ref, buf_ref, sem):
    tc = x1_ref.shape[0]
    C = SUBLANES

    def slab(ref, r):
        return ref.at[pl.ds(pl.multiple_of(r * C, C), C)]

    def row_copy(t, k):
        return pltpu.make_async_copy(slab(ys_ref, dest_ref[k, t]), slab(buf_ref, k * tc + t), sem)

    def start_rows(t, c):
        for k in range(TOP_K):
            row_copy(t, k).start(priority=k % 2)
        return c

    def wait_rows(t, c):
        for k in range(TOP_K):
            row_copy(t, k).wait()
        return c

    lax.fori_loop(0, tc, start_rows, 0)

    h = h2b_ref[...]
    g = jnp.dot(h, wsg_ref[...], preferred_element_type=F32)
    u = jnp.dot(h, wsu_ref[...], preferred_element_type=F32)
    shared = jnp.dot((g * jax.nn.sigmoid(g) * u).astype(BF16), wsd_ref[...], preferred_element_type=F32)

    lax.fori_loop(0, tc, wait_rows, 0)

    wt = wt_ref[...]
    acc = None
    for k in range(TOP_K):
        term = _rows_load(buf_ref, k * tc, tc) * wt[:, k:k + 1]
        acc = term if acc is None else acc + term
    o_ref[...] = x1_ref[...] + mod_ref[0, 5:6, :] * (acc + shared)


def _combine(dest, wts_t, ys, x1, h2b, mod3, wsg, wsu, wsd, S, tc):
    T, D = x1.shape
    per_seq = S // tc
    row = lambda i: (i, 0)
    return pl.pallas_call(
        _combine_kernel,
        grid=(T // tc,),
        in_specs=[pl.BlockSpec((TOP_K, tc), lambda i: (0, i), memory_space=pltpu.SMEM),
                  pl.BlockSpec((tc, TOP_K), row),
                  pl.BlockSpec(memory_space=pl.ANY),
                  pl.BlockSpec((tc, D), row),
                  pl.BlockSpec((tc, D), row),
                  pl.BlockSpec((1, 6, D), lambda i: (i // per_seq, 0, 0)),
                  _const_spec(wsg.shape),
                  _const_spec(wsu.shape),
                  _const_spec(wsd.shape)],
        out_specs=pl.BlockSpec((tc, D), row),
        out_shape=jax.ShapeDtypeStruct((T, D), F32),
        scratch_shapes=[pltpu.VMEM((TOP_K * tc * SUBLANES, LANES), F32), pltpu.SemaphoreType.DMA],
        compiler_params=_params("arbitrary"),
        name="combine",
    )(dest, wts_t, ys, x1, h2b, mod3, wsg, wsu, wsd)


def _tile(n, want):
    t = min(n, want)
    while n % t:
        t //= 2
    return t


def kernel(x, c, rel_bias_table, w_ada, b_ada, norm1_g, w_in, q_norm_g, k_norm_g, lambda_q1, lambda_k1,
           lambda_q2, lambda_k2, subln_g, pool_w, pool_scale, w_out, norm2_g, w_router, b_router,
           w_exp_gate, w_exp_up, w_exp_down, w_sh_gate, w_sh_up, w_sh_down):
    B, S, D = x.shape
    T = B * S
    depth = w_ada.shape[0]
    assert depth == 1, "LAM_INIT and the single-layer pipeline assume depth 1"
    assert D == N_HEADS * HEAD_DV and S % LANES == 0
    l = 0
    tm = _tile(S, 512)
    tq = _tile(S, 512)

    x2 = x.reshape(T, D)
    mod3 = _ada(c, w_ada[l], b_ada[l]).reshape(B, 6, D)

    grp = jnp.arange(D) // HEAD_DK
    gmat = (grp[:, None] == grp[None, :]).astype(BF16)
    qg = (jnp.tile(q_norm_g[l], D // HEAD_DK) * (HEAD_DK ** -0.5 * LOG2E)).reshape(1, D)
    kg = jnp.tile(k_norm_g[l], D // HEAD_DK).reshape(1, D)
    q, k, v, p_in, ga, gp = _inproj(x2, mod3, norm1_g[l].reshape(1, D), w_in[l].astype(BF16),
                                    gmat, qg, kg, S, tm)

    strip = _bias_strip(rel_bias_table, S) * LOG2E
    nq = S // tq
    n = S + tq
    win = jnp.stack([strip[:, S - (i + 1) * tq: S - (i + 1) * tq + n] for i in range(nq)], axis=1)
    win = win.reshape(N_HEADS * nq, 1, n)
    lam_p = jnp.stack([lambda_q1[l], lambda_k1[l], lambda_q2[l], lambda_k2[l]])
    am = _attention(q, k, v, ga, win, lam_p, subln_g[l].reshape(1, HEAD_DV), B, S, tq)

    x1, h2r, h2b, logits_t = _mix(x2, am, gp, p_in, mod3, pool_w[l].astype(BF16),
                                  pool_scale[l].reshape(1, D), w_out[l].astype(BF16),
                                  norm2_g[l].reshape(1, D), w_router[l].T, S, tm)

    eidx, rank, wts, counts = _route(logits_t, b_router[l], _tile(T, 512))

    cnt = counts[:, 0].astype(jnp.int32)
    nblk = (cnt + MOE_BLOCK - 1) // MOE_BLOCK
    blk_end = jnp.cumsum(nblk)
    base = ((blk_end - nblk) * MOE_BLOCK).astype(F32).reshape(N_EXPERTS, 1)
    lastblk = jnp.where(nblk > 0, blk_end - 1, -1).astype(jnp.int32)
    n_blocks = -(-(T * TOP_K) // MOE_BLOCK) + N_EXPERTS
    blk_expert = jnp.clip(jnp.searchsorted(blk_end, jnp.arange(n_blocks), side='right'),
                          0, N_EXPERTS - 1).astype(jnp.int32)
    nused = blk_end[-1:].astype(jnp.int32)

    dest = _dest(eidx, rank, base, _tile(T, 512))
    xs = _dispatch(lastblk, dest, h2r, n_blocks * MOE_BLOCK, _tile(S, 256))
    ys = _experts(blk_expert, nused, xs, w_exp_gate[l], w_exp_up[l], w_exp_down[l])
    out = _combine(dest, wts.T, ys, x1, h2b, mod3, w_sh_gate[l].astype(BF16), w_sh_up[l].astype(BF16),
                   w_sh_down[l].astype(BF16), S, _tile(S, 128))
    return out.reshape(B, S, D)
```

```python
import functools
import math

import jax
import jax.numpy as jnp
from jax import lax
from jax.experimental import pallas as pl
from jax.experimental.pallas import tpu as pltpu

N_HEADS = 8
HEAD_DK = 64
HEAD_DV = 2 * HEAD_DK
N_BUCKETS = 32
MAX_DISTANCE = 128
POOL_WINDOWS = (2, 4, 8, 16)
N_POOL_GROUPS = 4
N_EXPERTS = 256
TOP_K = 8
N_EXPERT_GROUPS = 8
TOPK_GROUPS = 4
ROUTED_SCALE = 2.5
MOE_BLOCK = 512
EPS = 1e-6
LAM_INIT = 0.8 - 0.6 * math.exp(-0.3 * 0)
LOG2E = math.log2(math.e)
KEY_TILE = 512

LANES = 128
SUBLANES = 8
VMEM_LIMIT = 56 * 1024 * 1024

F32 = jnp.float32
BF16 = jnp.bfloat16
HIGHEST = lax.Precision.HIGHEST
NEG_INF = float("-inf")


def _params(*sem):
    return pltpu.CompilerParams(dimension_semantics=sem, vmem_limit_bytes=VMEM_LIMIT)


def _const_spec(shape):
    nd = len(shape)
    return pl.BlockSpec(shape, lambda *_: (0,) * nd, pipeline_mode=pl.Buffered(1))


def _rows_load(ref, first, n, chunks=SUBLANES):
    return jnp.concatenate([ref[pl.ds(first * chunks + c, n, stride=chunks), :] for c in range(chunks)],
                           axis=-1)


def _rows_store(ref, first, val, chunks=SUBLANES):
    n = val.shape[0]
    for c in range(chunks):
        ref[pl.ds(first * chunks + c, n, stride=chunks), :] = val[:, c * LANES:(c + 1) * LANES]


def _ada_kernel(c_ref, w_ref, b_ref, o_ref):
    c = c_ref[...]
    s = c * jax.nn.sigmoid(c)
    o_ref[...] = jnp.dot(s, w_ref[...], precision=HIGHEST, preferred_element_type=F32) + b_ref[...]


def _ada(c, w_ada, b_ada):
    B, D = c.shape
    n = w_ada.shape[1] // D
    return pl.pallas_call(
        _ada_kernel,
        grid=(n,),
        in_specs=[pl.BlockSpec((B, D), lambda j: (0, 0)),
                  pl.BlockSpec((D, D), lambda j: (0, j)),
                  pl.BlockSpec((1, D), lambda j: (0, j))],
        out_specs=pl.BlockSpec((B, D), lambda j: (0, j)),
        out_shape=jax.ShapeDtypeStruct((B, n * D), F32),
        compiler_params=_params("arbitrary"),
        name="ada",
    )(c, w_ada, b_ada.reshape(1, -1))


def _bias_kernel(bk_ref, tab_ref, o_ref):
    bk = bk_ref[...]
    onehot = (lax.broadcasted_iota(jnp.int32, (N_BUCKETS, bk.shape[1]), 0) == bk).astype(F32)
    o_ref[...] = lax.dot_general(tab_ref[...], onehot, (((0,), (0,)), ((), ())),
                                 precision=HIGHEST, preferred_element_type=F32)


def _t5_buckets(rel):
    nb = N_BUCKETS // 2
    max_exact = nb // 2
    n = jnp.abs(rel)
    large = max_exact + (jnp.log(jnp.maximum(n, 1).astype(jnp.float32) / max_exact)
                         / math.log(MAX_DISTANCE / max_exact) * (nb - max_exact)).astype(jnp.int32)
    large = jnp.minimum(large, nb - 1)
    return jnp.where(rel > 0, nb, 0) + jnp.where(n < max_exact, n, large)


def _bias_strip(rel_table, S):
    L = 2 * S
    buckets = _t5_buckets(jnp.arange(L, dtype=jnp.int32) - S).reshape(1, L)
    return pl.pallas_call(
        _bias_kernel,
        out_shape=jax.ShapeDtypeStruct((N_HEADS, L), F32),
        name="bias",
    )(buckets, rel_table)


def _inproj_kernel(x_ref, mod_ref, g1_ref, w_ref, gmat_ref, qg_ref, kg_ref,
                   q_ref, kt_ref, v_ref, p_ref, ga_ref, gp_ref):
    D = x_ref.shape[1]
    x = x_ref[...]
    h = x * lax.rsqrt(jnp.mean(x * x, axis=-1, keepdims=True) + EPS) * g1_ref[...]
    h = h * (1.0 + mod_ref[0, 1:2, :]) + mod_ref[0, 0:1, :]
    hb = h.astype(BF16)

    def chunk(c):
        return jnp.dot(hb, w_ref[:, c * D:(c + 1) * D], preferred_element_type=F32)

    def head_norm(y, g_ref):
        ss = jnp.dot((y * y).astype(BF16), gmat_ref[...], preferred_element_type=F32)
        return y * lax.rsqrt(ss * (1.0 / HEAD_DK) + EPS) * g_ref[...]

    q_ref[...] = head_norm(chunk(0), qg_ref).astype(BF16)
    kn = head_norm(chunk(1), kg_ref)
    for hd in range(N_HEADS):
        for t in range(x_ref.shape[0] // KEY_TILE):
            blk = kn[t * KEY_TILE:(t + 1) * KEY_TILE, hd * HEAD_DV:(hd + 1) * HEAD_DV]
            kt_ref[0, hd, t] = blk.T.astype(BF16)
    v_ref[...] = chunk(2).astype(BF16)
    p_ref[...] = chunk(3)
    ga_ref[...] = jax.nn.sigmoid(chunk(4)).astype(BF16)
    gp_ref[...] = jax.nn.sigmoid(chunk(5)).astype(BF16)


def _inproj(x2, mod3, norm1_g, w_in_b, gmat, qg, kg, S, tm):
    T, D = x2.shape
    per_seq = S // tm
    row = lambda i: (i, 0)
    tok = jax.ShapeDtypeStruct((T, D), BF16)
    kt = jax.ShapeDtypeStruct((T // S, N_HEADS, S // KEY_TILE, HEAD_DV, KEY_TILE), BF16)
    outs = [tok, kt, tok, jax.ShapeDtypeStruct((T, D), F32), tok, tok]
    tok_spec = pl.BlockSpec((tm, D), row)
    kt_spec = pl.BlockSpec((1, N_HEADS, tm // KEY_TILE, HEAD_DV, KEY_TILE),
                           lambda i: (i // per_seq, 0, i % per_seq, 0, 0))
    return pl.pallas_call(
        _inproj_kernel,
        grid=(T // tm,),
        in_specs=[pl.BlockSpec((tm, D), row),
                  pl.BlockSpec((1, 6, D), lambda i: (i // per_seq, 0, 0)),
                  _const_spec((1, D)),
                  _const_spec(w_in_b.shape),
                  _const_spec((D, D)),
                  _const_spec((1, D)),
                  _const_spec((1, D))],
        out_specs=[tok_spec, kt_spec, tok_spec, tok_spec, tok_spec, tok_spec],
        out_shape=outs,
        compiler_params=_params("arbitrary"),
        name="inproj",
    )(x2, mod3, norm1_g, w_in_b, gmat, qg, kg)


def _attn_kernel(q_ref, kt_ref, v_ref, ga_ref, win_ref, lam_ref, sg_ref, o_ref,
                 bias_ref, x_ref, m_ref, acc_ref):
    tq = q_ref.shape[0]
    n_kt, _, kt = kt_ref.shape[2:]
    n = win_ref.shape[2]

    @pl.when(pl.program_id(2) == 0)
    def _():
        xb = jnp.broadcast_to(win_ref[0, 0], (tq, n))
        rolled = pltpu.roll(xb, n - tq, 1, stride=1, stride_axis=0)
        for j in range(n_kt):
            bias_ref[j] = rolled[:, j * kt:(j + 1) * kt]

    lp = lam_ref[...]
    lam = (jnp.exp(jnp.sum(lp[0:1] * lp[1:2], axis=-1, keepdims=True))
           - jnp.exp(jnp.sum(lp[2:3] * lp[3:4], axis=-1, keepdims=True)) + LAM_INIT)

    q = q_ref[...]
    first = lax.broadcasted_iota(jnp.int32, (1, HEAD_DV), 1) < HEAD_DK
    zero = jnp.zeros_like(q)
    qq = jnp.concatenate([jnp.where(first, q, zero), jnp.where(first, zero, q)], axis=0)

    for j in range(n_kt):
        b = bias_ref[j]
        x = jnp.dot(qq, kt_ref[0, 0, j], preferred_element_type=F32) + jnp.concatenate([b, b], axis=0)
        x_ref[j] = x
        t = x[:, 0:LANES]
        for c in range(1, kt // LANES):
            t = jnp.maximum(t, x[:, c * LANES:(c + 1) * LANES])
        m_ref[...] = t if j == 0 else jnp.maximum(m_ref[...], t)
    m = jnp.max(m_ref[...], axis=-1, keepdims=True)

    ones_col = (lax.broadcasted_iota(jnp.int32, (kt, LANES), 1) == 0).astype(BF16)
    mb = jnp.broadcast_to(m, (2 * tq, kt))
    for j in range(n_kt):
        p = jnp.exp2((x_ref[j] - mb).astype(BF16))
        v_ext = jnp.concatenate([v_ref[j * kt:(j + 1) * kt, :], ones_col], axis=-1)
        pv = jnp.concatenate([jnp.dot(p[:tq], v_ext, preferred_element_type=F32),
                              jnp.dot(p[tq:], v_ext, preferred_element_type=F32)], axis=0)
        acc_ref[...] = pv if j == 0 else acc_ref[...] + pv
    acc = acc_ref[...]
    o1, l1 = acc[:tq, :HEAD_DV], acc[:tq, HEAD_DV:HEAD_DV + 1]
    o2, l2 = acc[tq:, :HEAD_DV], acc[tq:, HEAD_DV:HEAD_DV + 1]
    o = o1 / l1 - lam * (o2 / l2)
    o = o * lax.rsqrt(jnp.mean(o * o, axis=-1, keepdims=True) + EPS) * sg_ref[...] * (1.0 - LAM_INIT)
    o_ref[...] = (o * ga_ref[...].astype(F32)).astype(BF16)


def _attention(q, kt, v, ga, win, lam_p, subln_g, B, S, tq):
    T, D = q.shape
    nq = S // tq
    n = win.shape[2]
    n_kt = S // KEY_TILE
    qmap = lambda h, i, b: (b * nq + i, h)
    return pl.pallas_call(
        _attn_kernel,
        grid=(N_HEADS, nq, B),
        in_specs=[pl.BlockSpec((tq, HEAD_DV), qmap),
                  pl.BlockSpec((1, 1, n_kt, HEAD_DV, KEY_TILE), lambda h, i, b: (b, h, 0, 0, 0)),
                  pl.BlockSpec((S, HEAD_DV), lambda h, i, b: (b, h)),
                  pl.BlockSpec((tq, HEAD_DV), qmap),
                  pl.BlockSpec((1, 1, n), lambda h, i, b: (h * nq + i, 0, 0)),
                  pl.BlockSpec((4, HEAD_DK), lambda h, i, b: (0, 0)),
                  pl.BlockSpec((1, HEAD_DV), lambda h, i, b: (0, 0))],
        out_specs=pl.BlockSpec((tq, HEAD_DV), qmap),
        out_shape=jax.ShapeDtypeStruct((T, D), BF16),
        scratch_shapes=[pltpu.VMEM((n_kt, tq, KEY_TILE), F32),
                        pltpu.VMEM((n_kt, 2 * tq, KEY_TILE), F32),
                        pltpu.VMEM((2 * tq, LANES), F32),
                        pltpu.VMEM((2 * tq, HEAD_DV + LANES), F32)],
        compiler_params=_params("arbitrary", "arbitrary", "arbitrary"),
        name="attn",
    )(q, kt, v, ga, win, lam_p, subln_g)


def _mix_kernel(S, x_ref, am_ref, gp_ref, p_ref, pprev_ref, pnext_ref, mod_ref, pw_ref, ps_ref,
                wo_ref, g2_ref, wr_ref, x1_ref, h2r_ref, lg_ref):
    tm, D = x_ref.shape
    gc = D // N_POOL_GROUPS
    halo = SUBLANES
    ne = tm + 2 * halo
    pos0 = (pl.program_id(0) % (S // tm)) * tm
    pos_e = pos0 - halo + lax.broadcasted_iota(jnp.int32, (ne, 1), 0)
    valid = (pos_e >= 0) & (pos_e < S)
    ext = jnp.concatenate([pprev_ref[...], p_ref[...], pnext_ref[...]], axis=0)
    ext = jnp.where(valid, ext, 0.0)
    pos = pos0 + lax.broadcasted_iota(jnp.int32, (tm, 1), 0)

    merged = []
    for g, w in enumerate(POOL_WINDOWS):
        half = w // 2
        e = ext[:, g * gc:(g + 1) * gc]
        sw = e
        width = 1
        while width < w:
            sw = sw + pltpu.roll(sw, width, 0)
            width *= 2
        win = pltpu.roll(sw, ne - (half - 1), 0)[halo:halo + tm] if half > 1 else sw[halo:halo + tm]
        lo = jnp.clip(pos - half, 0, S - 1)
        hi = jnp.clip(pos + half - 1, 0, S - 1)
        cnt = (hi - lo + 1).astype(F32)
        mixed = win / cnt - p_ref[:, g * gc:(g + 1) * gc]
        pooled = jnp.dot(mixed.astype(BF16), pw_ref[g], preferred_element_type=F32)
        pooled = pooled * ps_ref[:, g * gc:(g + 1) * gc]
        merged.append(am_ref[:, g * gc:(g + 1) * gc].astype(F32)
                      + gp_ref[:, g * gc:(g + 1) * gc].astype(F32) * pooled)
    merged = jnp.concatenate(merged, axis=-1).astype(BF16)
    y = jnp.dot(merged, wo_ref[...], preferred_element_type=F32)
    x1 = x_ref[...] + mod_ref[0, 2:3, :] * y
    x1_ref[...] = x1
    h2 = x1 * lax.rsqrt(jnp.mean(x1 * x1, axis=-1, keepdims=True) + EPS) * g2_ref[...]
    h2 = h2 * (1.0 + mod_ref[0, 4:5, :]) + mod_ref[0, 3:4, :]
    _rows_store(h2r_ref, 0, h2)
    lg_ref[...] = lax.dot_general(wr_ref[...], h2, (((1,), (1,)), ((), ())),
                                  precision=HIGHEST, preferred_element_type=F32)


def _mix(x2, am, gp, p_in, mod3, pool_w_b, pool_scale, w_out_b, norm2_g, w_router_t, S, tm):
    T, D = x2.shape
    per_seq = S // tm
    hb = tm // SUBLANES
    last = T // SUBLANES - 1
    row = lambda i: (i, 0)
    E = w_router_t.shape[0]
    return pl.pallas_call(
        functools.partial(_mix_kernel, S),
        grid=(T // tm,),
        in_specs=[pl.BlockSpec((tm, D), row),
                  pl.BlockSpec((tm, D), row),
                  pl.BlockSpec((tm, D), row),
                  pl.BlockSpec((tm, D), row),
                  pl.BlockSpec((SUBLANES, D), lambda i: (jnp.maximum(i * hb - 1, 0), 0)),
                  pl.BlockSpec((SUBLANES, D), lambda i: (jnp.minimum((i + 1) * hb, last), 0)),
                  pl.BlockSpec((1, 6, D), lambda i: (i // per_seq, 0, 0)),
                  _const_spec(pool_w_b.shape),
                  _const_spec((1, D)),
                  _const_spec((D, D)),
                  _const_spec((1, D)),
                  _const_spec((E, D))],
        out_specs=[pl.BlockSpec((tm, D), row),
                   pl.BlockSpec((tm * SUBLANES, LANES), row),
                   pl.BlockSpec((E, tm), lambda i: (0, i))],
        out_shape=[jax.ShapeDtypeStruct((T, D), F32),
                   jax.ShapeDtypeStruct((T * SUBLANES, LANES), F32),
                   jax.ShapeDtypeStruct((E, T), F32)],
        compiler_params=_params("arbitrary"),
        name="mix",
    )(x2, am, gp, p_in, p_in, p_in, mod3, pool_w_b, pool_scale, w_out_b, norm2_g, w_router_t)


def _route_kernel(lg_ref, br_ref, tri_ref, eidx_ref, rank_ref, wts_ref, cnt_ref):
    E, tr = lg_ref.shape
    per = E // N_EXPERT_GROUPS

    @pl.when(pl.program_id(0) == 0)
    def _():
        cnt_ref[...] = jnp.zeros_like(cnt_ref)

    scores = jax.nn.sigmoid(lg_ref[...])
    biased = scores + br_ref[...]
    b3 = biased.reshape(N_EXPERT_GROUPS, per, tr)
    io_per = lax.broadcasted_iota(jnp.int32, b3.shape, 1)
    m1 = jnp.max(b3, axis=1, keepdims=True)
    i1 = jnp.min(jnp.where(b3 == m1, io_per, per), axis=1, keepdims=True)
    m2 = jnp.max(jnp.where(io_per == i1, NEG_INF, b3), axis=1, keepdims=True)
    gs = (m1 + m2)[:, 0, :]

    io_g = lax.broadcasted_iota(jnp.int32, gs.shape, 0)
    gsel = jnp.zeros(gs.shape, jnp.bool_)
    cur = gs
    for _ in range(TOPK_GROUPS):
        gm = jnp.max(cur, axis=0, keepdims=True)
        gi = jnp.min(jnp.where(cur == gm, io_g, N_EXPERT_GROUPS), axis=0, keepdims=True)
        pick = io_g == gi
        gsel = gsel | pick
        cur = jnp.where(pick, NEG_INF, cur)
    cur = jnp.where(gsel[:, None, :], b3, NEG_INF).reshape(E, tr)

    io_e = lax.broadcasted_iota(jnp.int32, (E, tr), 0)
    idxs, raw = [], []
    sel = jnp.zeros((E, tr), jnp.bool_)
    for _ in range(TOP_K):
        m = jnp.max(cur, axis=0, keepdims=True)
        idx = jnp.min(jnp.where(cur == m, io_e, E), axis=0, keepdims=True)
        pick = io_e == idx
        idxs.append(idx)
        raw.append(jnp.sum(jnp.where(pick, scores, 0.0), axis=0, keepdims=True))
        sel = sel | pick
        cur = jnp.where(pick, NEG_INF, cur)
    raw = jnp.concatenate(raw, axis=0)
    wts_ref[...] = raw / jnp.sum(raw, axis=0, keepdims=True) * ROUTED_SCALE
    eidx_ref[...] = jnp.concatenate(idxs, axis=0)

    self_f = sel.astype(F32)
    incl = jnp.dot(sel.astype(BF16), tri_ref[...], preferred_element_type=F32)
    before = cnt_ref[...] + incl - self_f
    ranks = [jnp.sum(jnp.where(io_e == idx, before, 0.0), axis=0, keepdims=True) for idx in idxs]
    rank_ref[...] = jnp.concatenate(ranks, axis=0).astype(jnp.int32)
    cnt_ref[...] = cnt_ref[...] + jnp.sum(self_f, axis=1, keepdims=True)


def _route(logits_t, b_router, tr):
    E, T = logits_t.shape
    tri = (jnp.arange(tr)[:, None] <= jnp.arange(tr)[None, :]).astype(BF16)
    col = lambda i: (0, i)
    return pl.pallas_call(
        _route_kernel,
        grid=(T // tr,),
        in_specs=[pl.BlockSpec((E, tr), col),
                  _const_spec((E, 1)),
                  _const_spec((tr, tr))],
        out_specs=[pl.BlockSpec((TOP_K, tr), col),
                   pl.BlockSpec((TOP_K, tr), col),
                   pl.BlockSpec((TOP_K, tr), col),
                   pl.BlockSpec((E, 1), lambda i: (0, 0))],
        out_shape=[jax.ShapeDtypeStruct((TOP_K, T), jnp.int32),
                   jax.ShapeDtypeStruct((TOP_K, T), jnp.int32),
                   jax.ShapeDtypeStruct((TOP_K, T), F32),
                   jax.ShapeDtypeStruct((E, 1), F32)],
        compiler_params=_params("arbitrary"),
        name="route",
    )(logits_t, b_router.reshape(E, 1), tri)


def _dest_kernel(eidx_ref, rank_ref, base_ref, dest_ref):
    K, tr = eidx_ref.shape
    E = base_ref.shape[0]
    io_e = lax.broadcasted_iota(jnp.int32, (E, tr), 0)
    base = base_ref[...]
    rows = [jnp.sum(jnp.where(io_e == eidx_ref[k:k + 1, :], base, 0.0), axis=0, keepdims=True)
            for k in range(K)]
    dest_ref[...] = jnp.concatenate(rows, axis=0).astype(jnp.int32) + rank_ref[...]


def _dest(eidx, rank, base, tr):
    K, T = eidx.shape
    col = lambda i: (0, i)
    return pl.pallas_call(
        _dest_kernel,
        grid=(T // tr,),
        in_specs=[pl.BlockSpec((K, tr), col), pl.BlockSpec((K, tr), col), _const_spec(base.shape)],
        out_specs=pl.BlockSpec((K, tr), col),
        out_shape=jax.ShapeDtypeStruct((K, T), jnp.int32),
        compiler_params=_params("arbitrary"),
        name="dest",
    )(eidx, rank, base)


def _dispatch_kernel(lastblk_ref, dest_ref, h_ref, wsg_ref, wsu_ref, wsd_ref, xs_ref, sh_ref, zbuf_ref, sem):
    C = SUBLANES
    td = h_ref.shape[0] // C
    RC = zbuf_ref.shape[0]

    def slab(ref, r):
        return ref.at[pl.ds(pl.multiple_of(r * C, C), C)]

    @pl.when(pl.program_id(0) == 0)
    def _():
        zbuf_ref[...] = jnp.zeros_like(zbuf_ref)

        def zcopy(e):
            start = pl.multiple_of(lastblk_ref[e] * RC, RC)
            return pltpu.make_async_copy(zbuf_ref, xs_ref.at[pl.ds(start, RC)], sem)

        def start(e, c):
            @pl.when(lastblk_ref[e] >= 0)
            def _():
                zcopy(e).start()
            return c

        def wait(e, c):
            @pl.when(lastblk_ref[e] >= 0)
            def _():
                zcopy(e).wait()
            return c

        lax.fori_loop(0, N_EXPERTS, start, 0)
        lax.fori_loop(0, N_EXPERTS, wait, 0)

    def row_copy(t, k):
        return pltpu.make_async_copy(slab(h_ref, t), slab(xs_ref, dest_ref[k, t]), sem)

    def start_rows(t, c):
        for k in range(TOP_K):
            row_copy(t, k).start(priority=k % 2)
        return c

    def wait_rows(t, c):
        for k in range(TOP_K):
            row_copy(t, k).wait()
        return c

    lax.fori_loop(0, td, start_rows, 0, unroll=4)

    h = _rows_load(h_ref, 0, td).astype(BF16)
    g = jnp.dot(h, wsg_ref[...], preferred_element_type=F32)
    u = jnp.dot(h, wsu_ref[...], preferred_element_type=F32)
    shared = jnp.dot((g * jax.nn.sigmoid(g) * u).astype(BF16), wsd_ref[...], preferred_element_type=F32)
    sh_ref[...] = shared.astype(BF16)

    lax.fori_loop(0, td, wait_rows, 0)


def _dispatch(lastblk, dest, h2r, wsg, wsu, wsd, n_pad, td):
    C = SUBLANES
    T = h2r.shape[0] // C
    D = C * LANES
    const = lambda shape: pl.BlockSpec(shape, lambda i, lb: (0,) * len(shape), pipeline_mode=pl.Buffered(1))
    return pl.pallas_call(
        _dispatch_kernel,
        grid_spec=pltpu.PrefetchScalarGridSpec(
            num_scalar_prefetch=1,
            grid=(T // td,),
            in_specs=[pl.BlockSpec((TOP_K, td), lambda i, lb: (0, i), memory_space=pltpu.SMEM),
                      pl.BlockSpec((td * C, LANES), lambda i, lb: (i, 0)),
                      const(wsg.shape), const(wsu.shape), const(wsd.shape)],
            out_specs=[pl.BlockSpec(memory_space=pl.ANY),
                       pl.BlockSpec((td, D), lambda i, lb: (i, 0))],
            scratch_shapes=[pltpu.VMEM((MOE_BLOCK * C, LANES), F32), pltpu.SemaphoreType.DMA]),
        out_shape=[jax.ShapeDtypeStruct((n_pad * C, LANES), F32),
                   jax.ShapeDtypeStruct((T, D), BF16)],
        compiler_params=_params("arbitrary"),
        name="dispatch",
    )(lastblk, dest, h2r, wsg, wsu, wsd)


def _expert_kernel(be_ref, nused_ref, xs_ref, wg_ref, wu_ref, wd_ref, ys_ref):
    R = xs_ref.shape[0] // SUBLANES

    @pl.when(pl.program_id(0) < nused_ref[0])
    def _():
        x = _rows_load(xs_ref, 0, R).astype(BF16)
        g = jnp.dot(x, wg_ref[0].astype(BF16), preferred_element_type=F32)
        u = jnp.dot(x, wu_ref[0].astype(BF16), preferred_element_type=F32)
        hmid = (g * jax.nn.sigmoid(g) * u).astype(BF16)
        y = jnp.dot(hmid, wd_ref[0].astype(BF16), preferred_element_type=F32)
        _rows_store(ys_ref, 0, y)

    @pl.when(pl.program_id(0) >= nused_ref[0])
    def _():
        ys_ref[...] = jnp.zeros_like(ys_ref)


def _experts(blk_expert, nused, xs, w_eg, w_eu, w_ed):
    RC = MOE_BLOCK * SUBLANES
    n_blocks = xs.shape[0] // RC
    _, D, DE = w_eg.shape
    blk = lambda j, be, nu: jnp.minimum(j, nu[0] - 1)
    wmap = lambda j, be, nu: (be[blk(j, be, nu)], 0, 0)
    return pl.pallas_call(
        _expert_kernel,
        grid_spec=pltpu.PrefetchScalarGridSpec(
            num_scalar_prefetch=2,
            grid=(n_blocks,),
            in_specs=[pl.BlockSpec((RC, LANES), lambda j, be, nu: (blk(j, be, nu), 0)),
                      pl.BlockSpec((1, D, DE), wmap),
                      pl.BlockSpec((1, D, DE), wmap),
                      pl.BlockSpec((1, DE, D), wmap)],
            out_specs=pl.BlockSpec((RC, LANES), lambda j, be, nu: (j, 0))),
        out_shape=jax.ShapeDtypeStruct(xs.shape, F32),
        compiler_params=_params("arbitrary"),
        name="expert",
    )(blk_expert, nused, xs, w_eg, w_eu, w_ed)


def _combine_kernel(dest_ref, dnext_ref, wt_ref, ys_ref, x1_ref, sh_ref, mod_ref, o_ref, buf_ref, sems):
    tc = x1_ref.shape[0]
    C = SUBLANES
    i = pl.program_id(0)
    n = pl.num_programs(0)
    slot = i % 2
    per_slot = TOP_K * tc

    def slab(ref, r):
        return ref.at[pl.ds(pl.multiple_of(r * C, C), C)]

    def row_copy(d_ref, s, t, k):
        return pltpu.make_async_copy(slab(ys_ref, d_ref[k, t]), slab(buf_ref, s * per_slot + k * tc + t),
                                     sems.at[s])

    def start_tile(d_ref, s):
        def body(t, c):
            for k in range(TOP_K):
                row_copy(d_ref, s, t, k).start(priority=k % 2)
            return c
        lax.fori_loop(0, tc, body, 0, unroll=4)

    @pl.when(i == 0)
    def _():
        start_tile(dest_ref, slot)

    @pl.when(i + 1 < n)
    def _():
        start_tile(dnext_ref, 1 - slot)

    def wait_rows(t, c):
        for k in range(TOP_K):
            row_copy(dest_ref, slot, t, k).wait()
        return c

    lax.fori_loop(0, tc, wait_rows, 0)

    wt = wt_ref[...]
    acc = None
    for k in range(TOP_K):
        term = _rows_load(buf_ref, slot * per_slot + k * tc, tc) * wt[:, k:k + 1]
        acc = term if acc is None else acc + term
    o_ref[...] = x1_ref[...] + mod_ref[0, 5:6, :] * (acc + sh_ref[...].astype(F32))


def _combine(dest, wts_t, ys, x1, sh, mod3, S, tc):
    T, D = x1.shape
    per_seq = S // tc
    last = T // tc - 1
    row = lambda i: (i, 0)
    return pl.pallas_call(
        _combine_kernel,
        grid=(T // tc,),
        in_specs=[pl.BlockSpec((TOP_K, tc), lambda i: (0, i), memory_space=pltpu.SMEM),
                  pl.BlockSpec((TOP_K, tc), lambda i: (0, jnp.minimum(i + 1, last)), memory_space=pltpu.SMEM),
                  pl.BlockSpec((tc, TOP_K), row),
                  pl.BlockSpec(memory_space=pl.ANY),
                  pl.BlockSpec((tc, D), row),
                  pl.BlockSpec((tc, D), row),
                  pl.BlockSpec((1, 6, D), lambda i: (i // per_seq, 0, 0))],
        out_specs=pl.BlockSpec((tc, D), row),
        out_shape=jax.ShapeDtypeStruct((T, D), F32),
        scratch_shapes=[pltpu.VMEM((2 * TOP_K * tc * SUBLANES, LANES), F32), pltpu.SemaphoreType.DMA((2,))],
        compiler_params=_params("arbitrary"),
        name="combine",
    )(dest, dest, wts_t, ys, x1, sh, mod3)


def _tile(n, want):
    t = min(n, want)
    while n % t:
        t //= 2
    return t


def kernel(x, c, rel_bias_table, w_ada, b_ada, norm1_g, w_in, q_norm_g, k_norm_g, lambda_q1, lambda_k1,
           lambda_q2, lambda_k2, subln_g, pool_w, pool_scale, w_out, norm2_g, w_router, b_router,
           w_exp_gate, w_exp_up, w_exp_down, w_sh_gate, w_sh_up, w_sh_down):
    B, S, D = x.shape
    T = B * S
    depth = w_ada.shape[0]
    assert depth == 1, "LAM_INIT and the single-layer pipeline assume depth 1"
    assert D == N_HEADS * HEAD_DV and S % LANES == 0
    l = 0
    tm = _tile(S, 512)
    tq = _tile(S, 512)

    x2 = x.reshape(T, D)
    mod3 = _ada(c, w_ada[l], b_ada[l]).reshape(B, 6, D)

    grp = jnp.arange(D) // HEAD_DK
    gmat = (grp[:, None] == grp[None, :]).astype(BF16)
    qg = (jnp.tile(q_norm_g[l], D // HEAD_DK) * (HEAD_DK ** -0.5 * LOG2E)).reshape(1, D)
    kg = jnp.tile(k_norm_g[l], D // HEAD_DK).reshape(1, D)
    q, k, v, p_in, ga, gp = _inproj(x2, mod3, norm1_g[l].reshape(1, D), w_in[l].astype(BF16),
                                    gmat, qg, kg, S, tm)

    strip = _bias_strip(rel_bias_table, S) * LOG2E
    nq = S // tq
    n = S + tq
    win = jnp.stack([strip[:, S - (i + 1) * tq: S - (i + 1) * tq + n] for i in range(nq)], axis=1)
    win = win.reshape(N_HEADS * nq, 1, n)
    lam_p = jnp.stack([lambda_q1[l], lambda_k1[l], lambda_q2[l], lambda_k2[l]])
    am = _attention(q, k, v, ga, win, lam_p, subln_g[l].reshape(1, HEAD_DV), B, S, tq)

    x1, h2r, logits_t = _mix(x2, am, gp, p_in, mod3, pool_w[l].astype(BF16),
                                  pool_scale[l].reshape(1, D), w_out[l].astype(BF16),
                                  norm2_g[l].reshape(1, D), w_router[l].T, S, tm)

    eidx, rank, wts, counts = _route(logits_t, b_router[l], _tile(T, 512))

    cnt = counts[:, 0].astype(jnp.int32)
    nblk = (cnt + MOE_BLOCK - 1) // MOE_BLOCK
    blk_end = jnp.cumsum(nblk)
    base = ((blk_end - nblk) * MOE_BLOCK).astype(F32).reshape(N_EXPERTS, 1)
    lastblk = jnp.where(nblk > 0, blk_end - 1, -1).astype(jnp.int32)
    n_blocks = -(-(T * TOP_K) // MOE_BLOCK) + N_EXPERTS
    blk_expert = jnp.minimum(jnp.sum(blk_end[None, :] <= jnp.arange(n_blocks)[:, None], axis=1),
                             N_EXPERTS - 1).astype(jnp.int32)
    nused = blk_end[-1:].astype(jnp.int32)

    dest = _dest(eidx, rank, base, _tile(T, 512))
    xs, sh = _dispatch(lastblk, dest, h2r, w_sh_gate[l].astype(BF16), w_sh_up[l].astype(BF16),
                       w_sh_down[l].astype(BF16), n_blocks * MOE_BLOCK, _tile(S, 256))
    ys = _experts(blk_expert, nused, xs, w_exp_gate[l], w_exp_up[l], w_exp_down[l])
    out = _combine(dest, wts.T, ys, x1, sh, mod3, S, _tile(S, 128))
    return out.reshape(B, S, D)
```

```python
import functools
import math

import jax
import jax.numpy as jnp
from jax import lax
from jax.experimental import pallas as pl
from jax.experimental.pallas import tpu as pltpu

N_HEADS = 8
HEAD_DK = 64
HEAD_DV = 2 * HEAD_DK
N_BUCKETS = 32
MAX_DISTANCE = 128
POOL_WINDOWS = (2, 4, 8, 16)
N_POOL_GROUPS = 4
N_EXPERTS = 256
TOP_K = 8
N_EXPERT_GROUPS = 8
TOPK_GROUPS = 4
ROUTED_SCALE = 2.5
MOE_BLOCK = 512
EPS = 1e-6
LAM_INIT = 0.8 - 0.6 * math.exp(-0.3 * 0)
LOG2E = math.log2(math.e)
KEY_TILE = 512

LANES = 128
SUBLANES = 8
VMEM_LIMIT = 56 * 1024 * 1024

F32 = jnp.float32
BF16 = jnp.bfloat16
HIGHEST = lax.Precision.HIGHEST
NEG_INF = float("-inf")


def _params(*sem):
    return pltpu.CompilerParams(dimension_semantics=sem, vmem_limit_bytes=VMEM_LIMIT)


def _const_spec(shape):
    nd = len(shape)
    return pl.BlockSpec(shape, lambda *_: (0,) * nd, pipeline_mode=pl.Buffered(1))


def _rows_load(ref, first, n, chunks=SUBLANES):
    return jnp.concatenate([ref[pl.ds(first * chunks + c, n, stride=chunks), :] for c in range(chunks)],
                           axis=-1)


def _rows_store(ref, first, val, chunks=SUBLANES):
    n = val.shape[0]
    for c in range(chunks):
        ref[pl.ds(first * chunks + c, n, stride=chunks), :] = val[:, c * LANES:(c + 1) * LANES]


def _ada_kernel(c_ref, w_ref, b_ref, o_ref):
    c = c_ref[...]
    s = c * jax.nn.sigmoid(c)
    o_ref[...] = jnp.dot(s, w_ref[...], precision=HIGHEST, preferred_element_type=F32) + b_ref[...]


def _ada(c, w_ada, b_ada):
    B, D = c.shape
    n = w_ada.shape[1] // D
    return pl.pallas_call(
        _ada_kernel,
        grid=(n,),
        in_specs=[pl.BlockSpec((B, D), lambda j: (0, 0)),
                  pl.BlockSpec((D, D), lambda j: (0, j)),
                  pl.BlockSpec((1, D), lambda j: (0, j))],
        out_specs=pl.BlockSpec((B, D), lambda j: (0, j)),
        out_shape=jax.ShapeDtypeStruct((B, n * D), F32),
        compiler_params=_params("arbitrary"),
        name="ada",
    )(c, w_ada, b_ada.reshape(1, -1))


def _bias_kernel(bk_ref, tab_ref, o_ref):
    bk = bk_ref[...]
    onehot = (lax.broadcasted_iota(jnp.int32, (N_BUCKETS, bk.shape[1]), 0) == bk).astype(F32)
    o_ref[...] = lax.dot_general(tab_ref[...], onehot, (((0,), (0,)), ((), ())),
                                 precision=HIGHEST, preferred_element_type=F32)


def _t5_buckets(rel):
    nb = N_BUCKETS // 2
    max_exact = nb // 2
    n = jnp.abs(rel)
    large = max_exact + (jnp.log(jnp.maximum(n, 1).astype(jnp.float32) / max_exact)
                         / math.log(MAX_DISTANCE / max_exact) * (nb - max_exact)).astype(jnp.int32)
    large = jnp.minimum(large, nb - 1)
    return jnp.where(rel > 0, nb, 0) + jnp.where(n < max_exact, n, large)


def _bias_strip(rel_table, S):
    L = 2 * S
    buckets = _t5_buckets(jnp.arange(L, dtype=jnp.int32) - S).reshape(1, L)
    return pl.pallas_call(
        _bias_kernel,
        out_shape=jax.ShapeDtypeStruct((N_HEADS, L), F32),
        name="bias",
    )(buckets, rel_table)


def _inproj_kernel(x_ref, mod_ref, g1_ref, w_ref, gsum_ref, gspread_ref, qg_ref, kg_ref,
                   q_ref, kt_ref, v_ref, p_ref, ga_ref, gp_ref):
    D = x_ref.shape[1]
    x = x_ref[...]
    h = x * lax.rsqrt(jnp.mean(x * x, axis=-1, keepdims=True) + EPS) * g1_ref[...]
    h = h * (1.0 + mod_ref[0, 1:2, :]) + mod_ref[0, 0:1, :]
    hb = h.astype(BF16)

    def chunk(c):
        return jnp.dot(hb, w_ref[:, c * D:(c + 1) * D], preferred_element_type=F32)

    def head_norm(y, g_ref):
        ss = jnp.dot((y * y).astype(BF16), gsum_ref[...], preferred_element_type=F32)
        r = lax.rsqrt(ss * (1.0 / HEAD_DK) + EPS)
        r_hi = r.astype(BF16)
        r_lo = (r - r_hi.astype(F32)).astype(BF16)
        scale = jnp.dot(jnp.concatenate([r_hi, r_lo], axis=-1), gspread_ref[...],
                        preferred_element_type=F32)
        return y * scale * g_ref[...]

    q_ref[...] = head_norm(chunk(0), qg_ref).astype(BF16)
    kn = head_norm(chunk(1), kg_ref)
    for hd in range(N_HEADS):
        for t in range(x_ref.shape[0] // KEY_TILE):
            blk = kn[t * KEY_TILE:(t + 1) * KEY_TILE, hd * HEAD_DV:(hd + 1) * HEAD_DV]
            kt_ref[0, hd, t] = blk.T.astype(BF16)
    v_ref[...] = chunk(2).astype(BF16)
    p_ref[...] = chunk(3)
    ga_ref[...] = jax.nn.sigmoid(chunk(4)).astype(BF16)
    gp_ref[...] = jax.nn.sigmoid(chunk(5)).astype(BF16)


def _inproj(x2, mod3, norm1_g, w_in_b, qg, kg, S, tm):
    T, D = x2.shape
    grp = jnp.arange(D) // HEAD_DK
    gsum = (grp[:, None] == jnp.arange(LANES)[None, :]).astype(BF16)
    gspread = jnp.concatenate([gsum.T, gsum.T], axis=0)
    per_seq = S // tm
    row = lambda i: (i, 0)
    tok = jax.ShapeDtypeStruct((T, D), BF16)
    kt = jax.ShapeDtypeStruct((T // S, N_HEADS, S // KEY_TILE, HEAD_DV, KEY_TILE), BF16)
    outs = [tok, kt, tok, jax.ShapeDtypeStruct((T, D), F32), tok, tok]
    tok_spec = pl.BlockSpec((tm, D), row)
    kt_spec = pl.BlockSpec((1, N_HEADS, tm // KEY_TILE, HEAD_DV, KEY_TILE),
                           lambda i: (i // per_seq, 0, i % per_seq, 0, 0))
    return pl.pallas_call(
        _inproj_kernel,
        grid=(T // tm,),
        in_specs=[pl.BlockSpec((tm, D), row),
                  pl.BlockSpec((1, 6, D), lambda i: (i // per_seq, 0, 0)),
                  _const_spec((1, D)),
                  _const_spec(w_in_b.shape),
                  _const_spec(gsum.shape),
                  _const_spec(gspread.shape),
                  _const_spec((1, D)),
                  _const_spec((1, D))],
        out_specs=[tok_spec, kt_spec, tok_spec, tok_spec, tok_spec, tok_spec],
        out_shape=outs,
        compiler_params=_params("arbitrary"),
        name="inproj",
    )(x2, mod3, norm1_g, w_in_b, gsum, gspread, qg, kg)


def _attn_kernel(q_ref, kt_ref, v_ref, ga_ref, win_ref, lam_ref, sg_ref, o_ref,
                 bias_ref, x_ref, m_ref, acc_ref):
    tq = q_ref.shape[0]
    n_kt, _, kt = kt_ref.shape[2:]
    n = win_ref.shape[2]

    @pl.when(pl.program_id(2) == 0)
    def _():
        xb = jnp.broadcast_to(win_ref[0, 0], (tq, n))
        rolled = pltpu.roll(xb, n - tq, 1, stride=1, stride_axis=0)
        for j in range(n_kt):
            bias_ref[j] = rolled[:, j * kt:(j + 1) * kt]

    lp = lam_ref[...]
    lam = (jnp.exp(jnp.sum(lp[0:1] * lp[1:2], axis=-1, keepdims=True))
           - jnp.exp(jnp.sum(lp[2:3] * lp[3:4], axis=-1, keepdims=True)) + LAM_INIT)

    q = q_ref[...]
    first = lax.broadcasted_iota(jnp.int32, (1, HEAD_DV), 1) < HEAD_DK
    zero = jnp.zeros_like(q)
    qq = jnp.concatenate([jnp.where(first, q, zero), jnp.where(first, zero, q)], axis=0)

    for j in range(n_kt):
        b = bias_ref[j]
        x = jnp.dot(qq, kt_ref[0, 0, j], preferred_element_type=F32) + jnp.concatenate([b, b], axis=0)
        x_ref[j] = x
        t = x[:, 0:LANES]
        for c in range(1, kt // LANES):
            t = jnp.maximum(t, x[:, c * LANES:(c + 1) * LANES])
        m_ref[...] = t if j == 0 else jnp.maximum(m_ref[...], t)
    m = jnp.max(m_ref[...], axis=-1, keepdims=True)

    m_ref[...] = jnp.broadcast_to(m, m_ref.shape)

    def pv_tile(j, carry):
        mb = m_ref[...]
        mb = jnp.concatenate([mb] * (kt // LANES), axis=-1)
        ones_col = (lax.broadcasted_iota(jnp.int32, (kt, LANES), 1) == 0).astype(BF16)
        p = jnp.exp2((x_ref[j] - mb).astype(BF16))
        rows = pl.ds(pl.multiple_of(j * kt, kt), kt)
        v_ext = jnp.concatenate([v_ref[rows, :], ones_col], axis=-1)
        pv = jnp.concatenate([jnp.dot(p[:tq], v_ext, preferred_element_type=F32),
                              jnp.dot(p[tq:], v_ext, preferred_element_type=F32)], axis=0)
        acc_ref[...] = acc_ref[...] + pv
        return carry

    acc_ref[...] = jnp.zeros_like(acc_ref)
    lax.fori_loop(0, n_kt, pv_tile, 0)
    acc = acc_ref[...]
    o1, l1 = acc[:tq, :HEAD_DV], acc[:tq, HEAD_DV:HEAD_DV + 1]
    o2, l2 = acc[tq:, :HEAD_DV], acc[tq:, HEAD_DV:HEAD_DV + 1]
    o = o1 / l1 - lam * (o2 / l2)
    o = o * lax.rsqrt(jnp.mean(o * o, axis=-1, keepdims=True) + EPS) * sg_ref[...] * (1.0 - LAM_INIT)
    o_ref[...] = (o * ga_ref[...].astype(F32)).astype(BF16)


def _attention(q, kt, v, ga, win, lam_p, subln_g, B, S, tq):
    T, D = q.shape
    nq = S // tq
    n = win.shape[2]
    n_kt = S // KEY_TILE
    qmap = lambda h, i, b: (b * nq + i, h)
    return pl.pallas_call(
        _attn_kernel,
        grid=(N_HEADS, nq, B),
        in_specs=[pl.BlockSpec((tq, HEAD_DV), qmap),
                  pl.BlockSpec((1, 1, n_kt, HEAD_DV, KEY_TILE), lambda h, i, b: (b, h, 0, 0, 0)),
                  pl.BlockSpec((S, HEAD_DV), lambda h, i, b: (b, h)),
                  pl.BlockSpec((tq, HEAD_DV), qmap),
                  pl.BlockSpec((1, 1, n), lambda h, i, b: (h * nq + i, 0, 0)),
                  pl.BlockSpec((4, HEAD_DK), lambda h, i, b: (0, 0)),
                  pl.BlockSpec((1, HEAD_DV), lambda h, i, b: (0, 0))],
        out_specs=pl.BlockSpec((tq, HEAD_DV), qmap),
        out_shape=jax.ShapeDtypeStruct((T, D), BF16),
        scratch_shapes=[pltpu.VMEM((n_kt, tq, KEY_TILE), F32),
                        pltpu.VMEM((n_kt, 2 * tq, KEY_TILE), F32),
                        pltpu.VMEM((2 * tq, LANES), F32),
                        pltpu.VMEM((2 * tq, HEAD_DV + LANES), F32)],
        compiler_params=_params("arbitrary", "arbitrary", "arbitrary"),
        name="attn",
    )(q, kt, v, ga, win, lam_p, subln_g)


def _mix_kernel(S, x_ref, am_ref, gp_ref, p_ref, pprev_ref, pnext_ref, mod_ref, pw_ref, ps_ref,
                wo_ref, g2_ref, wr_ref, x1_ref, h2r_ref, lg_ref):
    tm, D = x_ref.shape
    gc = D // N_POOL_GROUPS
    halo = SUBLANES
    ne = tm + 2 * halo
    pos0 = (pl.program_id(0) % (S // tm)) * tm
    pos_e = pos0 - halo + lax.broadcasted_iota(jnp.int32, (ne, 1), 0)
    valid = (pos_e >= 0) & (pos_e < S)
    ext = jnp.concatenate([pprev_ref[...], p_ref[...], pnext_ref[...]], axis=0)
    ext = jnp.where(valid, ext, 0.0)
    pos = pos0 + lax.broadcasted_iota(jnp.int32, (tm, 1), 0)

    merged = []
    for g, w in enumerate(POOL_WINDOWS):
        half = w // 2
        e = ext[:, g * gc:(g + 1) * gc]
        sw = e
        width = 1
        while width < w:
            sw = sw + pltpu.roll(sw, width, 0)
            width *= 2
        win = pltpu.roll(sw, ne - (half - 1), 0)[halo:halo + tm] if half > 1 else sw[halo:halo + tm]
        lo = jnp.clip(pos - half, 0, S - 1)
        hi = jnp.clip(pos + half - 1, 0, S - 1)
        cnt = (hi - lo + 1).astype(F32)
        mixed = win / cnt - p_ref[:, g * gc:(g + 1) * gc]
        pooled = jnp.dot(mixed.astype(BF16), pw_ref[g], preferred_element_type=F32)
        pooled = pooled * ps_ref[:, g * gc:(g + 1) * gc]
        merged.append(am_ref[:, g * gc:(g + 1) * gc].astype(F32)
                      + gp_ref[:, g * gc:(g + 1) * gc].astype(F32) * pooled)
    merged = jnp.concatenate(merged, axis=-1).astype(BF16)
    y = jnp.dot(merged, wo_ref[...], preferred_element_type=F32)
    x1 = x_ref[...] + mod_ref[0, 2:3, :] * y
    x1_ref[...] = x1
    h2 = x1 * lax.rsqrt(jnp.mean(x1 * x1, axis=-1, keepdims=True) + EPS) * g2_ref[...]
    h2 = h2 * (1.0 + mod_ref[0, 4:5, :]) + mod_ref[0, 3:4, :]
    _rows_store(h2r_ref, 0, h2)
    lg_ref[...] = lax.dot_general(wr_ref[...], h2, (((1,), (1,)), ((), ())),
                                  precision=HIGHEST, preferred_element_type=F32)


def _mix(x2, am, gp, p_in, mod3, pool_w_b, pool_scale, w_out_b, norm2_g, w_router_t, S, tm):
    T, D = x2.shape
    per_seq = S // tm
    hb = tm // SUBLANES
    last = T // SUBLANES - 1
    row = lambda i: (i, 0)
    E = w_router_t.shape[0]
    return pl.pallas_call(
        functools.partial(_mix_kernel, S),
        grid=(T // tm,),
        in_specs=[pl.BlockSpec((tm, D), row),
                  pl.BlockSpec((tm, D), row),
                  pl.BlockSpec((tm, D), row),
                  pl.BlockSpec((tm, D), row),
                  pl.BlockSpec((SUBLANES, D), lambda i: (jnp.maximum(i * hb - 1, 0), 0)),
                  pl.BlockSpec((SUBLANES, D), lambda i: (jnp.minimum((i + 1) * hb, last), 0)),
                  pl.BlockSpec((1, 6, D), lambda i: (i // per_seq, 0, 0)),
                  _const_spec(pool_w_b.shape),
                  _const_spec((1, D)),
                  _const_spec((D, D)),
                  _const_spec((1, D)),
                  _const_spec((E, D))],
        out_specs=[pl.BlockSpec((tm, D), row),
                   pl.BlockSpec((tm * SUBLANES, LANES), row),
                   pl.BlockSpec((E, tm), lambda i: (0, i))],
        out_shape=[jax.ShapeDtypeStruct((T, D), F32),
                   jax.ShapeDtypeStruct((T * SUBLANES, LANES), F32),
                   jax.ShapeDtypeStruct((E, T), F32)],
        compiler_params=_params("arbitrary"),
        name="mix",
    )(x2, am, gp, p_in, p_in, p_in, mod3, pool_w_b, pool_scale, w_out_b, norm2_g, w_router_t)


def _route_kernel(lg_ref, br_ref, tri_ref, eidx_ref, rank_ref, wts_ref, cnt_ref):
    E, tr = lg_ref.shape
    per = E // N_EXPERT_GROUPS

    @pl.when(pl.program_id(0) == 0)
    def _():
        cnt_ref[...] = jnp.zeros_like(cnt_ref)

    scores = jax.nn.sigmoid(lg_ref[...])
    biased = scores + br_ref[...]
    b3 = biased.reshape(N_EXPERT_GROUPS, per, tr)
    io_per = lax.broadcasted_iota(jnp.int32, b3.shape, 1)
    m1 = jnp.max(b3, axis=1, keepdims=True)
    i1 = jnp.min(jnp.where(b3 == m1, io_per, per), axis=1, keepdims=True)
    m2 = jnp.max(jnp.where(io_per == i1, NEG_INF, b3), axis=1, keepdims=True)
    gs = (m1 + m2)[:, 0, :]

    io_g = lax.broadcasted_iota(jnp.int32, gs.shape, 0)
    gsel = jnp.zeros(gs.shape, jnp.bool_)
    cur = gs
    for _ in range(TOPK_GROUPS):
        gm = jnp.max(cur, axis=0, keepdims=True)
        gi = jnp.min(jnp.where(cur == gm, io_g, N_EXPERT_GROUPS), axis=0, keepdims=True)
        pick = io_g == gi
        gsel = gsel | pick
        cur = jnp.where(pick, NEG_INF, cur)
    cur = jnp.where(gsel[:, None, :], b3, NEG_INF).reshape(E, tr)

    io_e = lax.broadcasted_iota(jnp.int32, (E, tr), 0)
    idxs, raw = [], []
    sel = jnp.zeros((E, tr), jnp.bool_)
    for _ in range(TOP_K):
        m = jnp.max(cur, axis=0, keepdims=True)
        idx = jnp.min(jnp.where(cur == m, io_e, E), axis=0, keepdims=True)
        pick = io_e == idx
        idxs.append(idx)
        raw.append(jnp.sum(jnp.where(pick, scores, 0.0), axis=0, keepdims=True))
        sel = sel | pick
        cur = jnp.where(pick, NEG_INF, cur)
    raw = jnp.concatenate(raw, axis=0)
    wts_ref[...] = raw / jnp.sum(raw, axis=0, keepdims=True) * ROUTED_SCALE
    eidx_ref[...] = jnp.concatenate(idxs, axis=0)

    self_f = sel.astype(F32)
    incl = jnp.dot(sel.astype(BF16), tri_ref[...], preferred_element_type=F32)
    before = cnt_ref[...] + incl - self_f
    ranks = [jnp.sum(jnp.where(io_e == idx, before, 0.0), axis=0, keepdims=True) for idx in idxs]
    rank_ref[...] = jnp.concatenate(ranks, axis=0).astype(jnp.int32)
    cnt_ref[...] = cnt_ref[...] + jnp.sum(self_f, axis=1, keepdims=True)


def _route(logits_t, b_router, tr):
    E, T = logits_t.shape
    tri = (jnp.arange(tr)[:, None] <= jnp.arange(tr)[None, :]).astype(BF16)
    col = lambda i: (0, i)
    return pl.pallas_call(
        _route_kernel,
        grid=(T // tr,),
        in_specs=[pl.BlockSpec((E, tr), col),
                  _const_spec((E, 1)),
                  _const_spec((tr, tr))],
        out_specs=[pl.BlockSpec((TOP_K, tr), col),
                   pl.BlockSpec((TOP_K, tr), col),
                   pl.BlockSpec((TOP_K, tr), col),
                   pl.BlockSpec((E, 1), lambda i: (0, 0))],
        out_shape=[jax.ShapeDtypeStruct((TOP_K, T), jnp.int32),
                   jax.ShapeDtypeStruct((TOP_K, T), jnp.int32),
                   jax.ShapeDtypeStruct((TOP_K, T), F32),
                   jax.ShapeDtypeStruct((E, 1), F32)],
        compiler_params=_params("arbitrary"),
        name="route",
    )(logits_t, b_router.reshape(E, 1), tri)


def _dest_kernel(eidx_ref, rank_ref, base_ref, dest_ref):
    K, tr = eidx_ref.shape
    E = base_ref.shape[0]
    io_e = lax.broadcasted_iota(jnp.int32, (E, tr), 0)
    base = base_ref[...]
    rows = [jnp.sum(jnp.where(io_e == eidx_ref[k:k + 1, :], base, 0.0), axis=0, keepdims=True)
            for k in range(K)]
    dest_ref[...] = jnp.concatenate(rows, axis=0).astype(jnp.int32) + rank_ref[...]


def _dest(eidx, rank, base, tr):
    K, T = eidx.shape
    col = lambda i: (0, i)
    return pl.pallas_call(
        _dest_kernel,
        grid=(T // tr,),
        in_specs=[pl.BlockSpec((K, tr), col), pl.BlockSpec((K, tr), col), _const_spec(base.shape)],
        out_specs=pl.BlockSpec((K, tr), col),
        out_shape=jax.ShapeDtypeStruct((K, T), jnp.int32),
        compiler_params=_params("arbitrary"),
        name="dest",
    )(eidx, rank, base)


def _dispatch_kernel(lastblk_ref, dest_ref, h_ref, wsg_ref, wsu_ref, wsd_ref, xs_ref, sh_ref, zbuf_ref, sem):
    C = SUBLANES
    td = h_ref.shape[0] // C
    RC = zbuf_ref.shape[0]

    def slab(ref, r):
        return ref.at[pl.ds(pl.multiple_of(r * C, C), C)]

    @pl.when(pl.program_id(0) == 0)
    def _():
        zbuf_ref[...] = jnp.zeros_like(zbuf_ref)

        def zcopy(e):
            start = pl.multiple_of(lastblk_ref[e] * RC, RC)
            return pltpu.make_async_copy(zbuf_ref, xs_ref.at[pl.ds(start, RC)], sem)

        def start(e, c):
            @pl.when(lastblk_ref[e] >= 0)
            def _():
                zcopy(e).start()
            return c

        def wait(e, c):
            @pl.when(lastblk_ref[e] >= 0)
            def _():
                zcopy(e).wait()
            return c

        lax.fori_loop(0, N_EXPERTS, start, 0)
        lax.fori_loop(0, N_EXPERTS, wait, 0)

    def row_copy(t, k):
        return pltpu.make_async_copy(slab(h_ref, t), slab(xs_ref, dest_ref[k, t]), sem)

    def start_rows(t, c):
        for k in range(TOP_K):
            row_copy(t, k).start(priority=k % 2)
        return c

    def wait_rows():
        for _ in range(TOP_K):
            pltpu.make_async_copy(h_ref, xs_ref.at[pl.ds(0, td * C)], sem).wait()

    lax.fori_loop(0, td, start_rows, 0, unroll=4)

    h = _rows_load(h_ref, 0, td).astype(BF16)
    g = jnp.dot(h, wsg_ref[...], preferred_element_type=F32)
    u = jnp.dot(h, wsu_ref[...], preferred_element_type=F32)
    shared = jnp.dot((g * jax.nn.sigmoid(g) * u).astype(BF16), wsd_ref[...], preferred_element_type=F32)
    sh_ref[...] = shared.astype(BF16)

    wait_rows()


def _dispatch(lastblk, dest, h2r, wsg, wsu, wsd, n_pad, td):
    C = SUBLANES
    T = h2r.shape[0] // C
    D = C * LANES
    const = lambda shape: pl.BlockSpec(shape, lambda i, lb: (0,) * len(shape), pipeline_mode=pl.Buffered(1))
    return pl.pallas_call(
        _dispatch_kernel,
        grid_spec=pltpu.PrefetchScalarGridSpec(
            num_scalar_prefetch=1,
            grid=(T // td,),
            in_specs=[pl.BlockSpec((TOP_K, td), lambda i, lb: (0, i), memory_space=pltpu.SMEM),
                      pl.BlockSpec((td * C, LANES), lambda i, lb: (i, 0)),
                      const(wsg.shape), const(wsu.shape), const(wsd.shape)],
            out_specs=[pl.BlockSpec(memory_space=pl.ANY),
                       pl.BlockSpec((td, D), lambda i, lb: (i, 0))],
            scratch_shapes=[pltpu.VMEM((MOE_BLOCK * C, LANES), F32), pltpu.SemaphoreType.DMA]),
        out_shape=[jax.ShapeDtypeStruct((n_pad * C, LANES), F32),
                   jax.ShapeDtypeStruct((T, D), BF16)],
        compiler_params=_params("arbitrary"),
        name="dispatch",
    )(lastblk, dest, h2r, wsg, wsu, wsd)


def _expert_kernel(be_ref, nused_ref, xs_ref, wg_ref, wu_ref, wd_ref, ys_ref):
    R = xs_ref.shape[0] // SUBLANES

    @pl.when(pl.program_id(0) < nused_ref[0])
    def _():
        x = _rows_load(xs_ref, 0, R).astype(BF16)
        g = jnp.dot(x, wg_ref[0].astype(BF16), preferred_element_type=F32)
        u = jnp.dot(x, wu_ref[0].astype(BF16), preferred_element_type=F32)
        hmid = (g * jax.nn.sigmoid(g) * u).astype(BF16)
        y = jnp.dot(hmid, wd_ref[0].astype(BF16), preferred_element_type=F32)
        _rows_store(ys_ref, 0, y)

    @pl.when(pl.program_id(0) >= nused_ref[0])
    def _():
        ys_ref[...] = jnp.zeros_like(ys_ref)


def _experts(blk_expert, nused, xs, w_eg, w_eu, w_ed):
    RC = MOE_BLOCK * SUBLANES
    n_blocks = xs.shape[0] // RC
    _, D, DE = w_eg.shape
    blk = lambda j, be, nu: jnp.minimum(j, nu[0] - 1)
    wmap = lambda j, be, nu: (be[blk(j, be, nu)], 0, 0)
    return pl.pallas_call(
        _expert_kernel,
        grid_spec=pltpu.PrefetchScalarGridSpec(
            num_scalar_prefetch=2,
            grid=(n_blocks,),
            in_specs=[pl.BlockSpec((RC, LANES), lambda j, be, nu: (blk(j, be, nu), 0)),
                      pl.BlockSpec((1, D, DE), wmap),
                      pl.BlockSpec((1, D, DE), wmap),
                      pl.BlockSpec((1, DE, D), wmap)],
            out_specs=pl.BlockSpec((RC, LANES), lambda j, be, nu: (j, 0))),
        out_shape=jax.ShapeDtypeStruct(xs.shape, F32),
        compiler_params=_params("arbitrary"),
        name="expert",
    )(blk_expert, nused, xs, w_eg, w_eu, w_ed)


def _combine_kernel(dest_ref, dnext_ref, wt_ref, ys_ref, x1_ref, sh_ref, mod_ref, o_ref, buf_ref, sems):
    tc = x1_ref.shape[0]
    C = SUBLANES
    i = pl.program_id(0)
    n = pl.num_programs(0)
    slot = i % 2
    per_slot = TOP_K * tc

    def slab(ref, r):
        return ref.at[pl.ds(pl.multiple_of(r * C, C), C)]

    def row_copy(d_ref, s, t, k):
        return pltpu.make_async_copy(slab(ys_ref, d_ref[k, t]), slab(buf_ref, s * per_slot + k * tc + t),
                                     sems.at[s])

    def start_rows(d_ref, s, t):
        for k in range(TOP_K):
            row_copy(d_ref, s, t, k).start(priority=k % 2)

    def start_tile(d_ref, s):
        def body(t, c):
            start_rows(d_ref, s, t)
            return c
        lax.fori_loop(0, tc, body, 0, unroll=4)

    @pl.when(i == 0)
    def _():
        start_tile(dest_ref, slot)

    @pl.when(i + 1 < n)
    def _():
        start_tile(dnext_ref, 1 - slot)

    slot_rows = per_slot * C
    pltpu.make_async_copy(ys_ref.at[pl.ds(0, slot_rows)],
                          buf_ref.at[pl.ds(pl.multiple_of(slot * slot_rows, slot_rows), slot_rows)],
                          sems.at[slot]).wait()

    g2 = mod_ref[0, 5:6, :]
    o_ref[...] = x1_ref[...] + g2 * sh_ref[...].astype(F32)
    for k in range(TOP_K):
        w = wt_ref[:, k:k + 1] * g2
        o_ref[...] = o_ref[...] + _rows_load(buf_ref, slot * per_slot + k * tc, tc) * w


def _combine(dest, wts_t, ys, x1, sh, mod3, S, tc):
    T, D = x1.shape
    per_seq = S // tc
    last = T // tc - 1
    row = lambda i: (i, 0)
    return pl.pallas_call(
        _combine_kernel,
        grid=(T // tc,),
        in_specs=[pl.BlockSpec((TOP_K, tc), lambda i: (0, i), memory_space=pltpu.SMEM),
                  pl.BlockSpec((TOP_K, tc), lambda i: (0, jnp.minimum(i + 1, last)), memory_space=pltpu.SMEM),
                  pl.BlockSpec((tc, TOP_K), row),
                  pl.BlockSpec(memory_space=pl.ANY),
                  pl.BlockSpec((tc, D), row),
                  pl.BlockSpec((tc, D), row),
                  pl.BlockSpec((1, 6, D), lambda i: (i // per_seq, 0, 0))],
        out_specs=pl.BlockSpec((tc, D), row),
        out_shape=jax.ShapeDtypeStruct((T, D), F32),
        scratch_shapes=[pltpu.VMEM((2 * TOP_K * tc * SUBLANES, LANES), F32), pltpu.SemaphoreType.DMA((2,))],
        compiler_params=_params("arbitrary"),
        name="combine",
    )(dest, dest, wts_t, ys, x1, sh, mod3)


def _tile(n, want):
    t = min(n, want)
    while n % t:
        t //= 2
    return t


def kernel(x, c, rel_bias_table, w_ada, b_ada, norm1_g, w_in, q_norm_g, k_norm_g, lambda_q1, lambda_k1,
           lambda_q2, lambda_k2, subln_g, pool_w, pool_scale, w_out, norm2_g, w_router, b_router,
           w_exp_gate, w_exp_up, w_exp_down, w_sh_gate, w_sh_up, w_sh_down):
    B, S, D = x.shape
    T = B * S
    depth = w_ada.shape[0]
    assert depth == 1, "LAM_INIT and the single-layer pipeline assume depth 1"
    assert D == N_HEADS * HEAD_DV and S % LANES == 0
    l = 0
    tm = _tile(S, 512)
    tq = _tile(S, 512)

    x2 = x.reshape(T, D)
    mod3 = _ada(c, w_ada[l], b_ada[l]).reshape(B, 6, D)

    qg = (jnp.tile(q_norm_g[l], D // HEAD_DK) * (HEAD_DK ** -0.5 * LOG2E)).reshape(1, D)
    kg = jnp.tile(k_norm_g[l], D // HEAD_DK).reshape(1, D)
    q, k, v, p_in, ga, gp = _inproj(x2, mod3, norm1_g[l].reshape(1, D), w_in[l].astype(BF16),
                                    qg, kg, S, tm)

    strip = _bias_strip(rel_bias_table, S) * LOG2E
    nq = S // tq
    n = S + tq
    win = jnp.stack([strip[:, S - (i + 1) * tq: S - (i + 1) * tq + n] for i in range(nq)], axis=1)
    win = win.reshape(N_HEADS * nq, 1, n)
    lam_p = jnp.stack([lambda_q1[l], lambda_k1[l], lambda_q2[l], lambda_k2[l]])
    am = _attention(q, k, v, ga, win, lam_p, subln_g[l].reshape(1, HEAD_DV), B, S, tq)

    x1, h2r, logits_t = _mix(x2, am, gp, p_in, mod3, pool_w[l].astype(BF16),
                                  pool_scale[l].reshape(1, D), w_out[l].astype(BF16),
                                  norm2_g[l].reshape(1, D), w_router[l].T, S, tm)

    eidx, rank, wts, counts = _route(logits_t, b_router[l], _tile(T, 512))

    cnt = counts[:, 0].astype(jnp.int32)
    nblk = (cnt + MOE_BLOCK - 1) // MOE_BLOCK
    blk_end = jnp.cumsum(nblk)
    base = ((blk_end - nblk) * MOE_BLOCK).astype(F32).reshape(N_EXPERTS, 1)
    lastblk = jnp.where(nblk > 0, blk_end - 1, -1).astype(jnp.int32)
    n_blocks = -(-(T * TOP_K) // MOE_BLOCK) + N_EXPERTS
    blk_expert = jnp.minimum(jnp.sum(blk_end[None, :] <= jnp.arange(n_blocks)[:, None], axis=1),
                             N_EXPERTS - 1).astype(jnp.int32)
    nused = blk_end[-1:].astype(jnp.int32)

    dest = _dest(eidx, rank, base, _tile(T, 512))
    xs, sh = _dispatch(lastblk, dest, h2r, w_sh_gate[l].astype(BF16), w_sh_up[l].astype(BF16),
                       w_sh_down[l].astype(BF16), n_blocks * MOE_BLOCK, _tile(S, 256))
    ys = _experts(blk_expert, nused, xs, w_exp_gate[l], w_exp_up[l], w_exp_down[l])
    out = _combine(dest, wts.T, ys, x1, sh, mod3, S, _tile(S, 128))
    return out.reshape(B, S, D)
```

```python
import functools
import math

import jax
import jax.numpy as jnp
from jax import lax
from jax.experimental import pallas as pl
from jax.experimental.pallas import tpu as pltpu

N_HEADS = 8
HEAD_DK = 64
HEAD_DV = 2 * HEAD_DK
N_BUCKETS = 32
MAX_DISTANCE = 128
POOL_WINDOWS = (2, 4, 8, 16)
N_POOL_GROUPS = 4
N_EXPERTS = 256
TOP_K = 8
N_EXPERT_GROUPS = 8
TOPK_GROUPS = 4
ROUTED_SCALE = 2.5
MOE_BLOCK = 512
EPS = 1e-6
LAM_INIT = 0.8 - 0.6 * math.exp(-0.3 * 0)
LOG2E = math.log2(math.e)
KEY_TILE = 512

LANES = 128
SUBLANES = 8
VMEM_LIMIT = 56 * 1024 * 1024

F32 = jnp.float32
BF16 = jnp.bfloat16
HIGHEST = lax.Precision.HIGHEST
NEG_INF = float("-inf")


def _params(*sem):
    return pltpu.CompilerParams(dimension_semantics=sem, vmem_limit_bytes=VMEM_LIMIT)


def _const_spec(shape):
    nd = len(shape)
    return pl.BlockSpec(shape, lambda *_: (0,) * nd, pipeline_mode=pl.Buffered(1))


def _rows_load(ref, first, n, chunks=SUBLANES):
    return jnp.concatenate([ref[pl.ds(first * chunks + c, n, stride=chunks), :] for c in range(chunks)],
                           axis=-1)


def _rows_store(ref, first, val, chunks=SUBLANES):
    n = val.shape[0]
    for c in range(chunks):
        ref[pl.ds(first * chunks + c, n, stride=chunks), :] = val[:, c * LANES:(c + 1) * LANES]


def _ada_kernel(c_ref, w_ref, b_ref, o_ref):
    c = c_ref[...]
    s = c * jax.nn.sigmoid(c)
    o_ref[...] = jnp.dot(s, w_ref[...], precision=HIGHEST, preferred_element_type=F32) + b_ref[...]


def _ada(c, w_ada, b_ada):
    B, D = c.shape
    n = w_ada.shape[1] // D
    return pl.pallas_call(
        _ada_kernel,
        grid=(n,),
        in_specs=[pl.BlockSpec((B, D), lambda j: (0, 0)),
                  pl.BlockSpec((D, D), lambda j: (0, j)),
                  pl.BlockSpec((1, D), lambda j: (0, j))],
        out_specs=pl.BlockSpec((B, D), lambda j: (0, j)),
        out_shape=jax.ShapeDtypeStruct((B, n * D), F32),
        compiler_params=_params("arbitrary"),
        name="ada",
    )(c, w_ada, b_ada.reshape(1, -1))


def _bias_kernel(bk_ref, tab_ref, o_ref):
    bk = bk_ref[...]
    onehot = (lax.broadcasted_iota(jnp.int32, (N_BUCKETS, bk.shape[1]), 0) == bk).astype(F32)
    o_ref[...] = lax.dot_general(tab_ref[...], onehot, (((0,), (0,)), ((), ())),
                                 precision=HIGHEST, preferred_element_type=F32)


def _t5_buckets(rel):
    nb = N_BUCKETS // 2
    max_exact = nb // 2
    n = jnp.abs(rel)
    large = max_exact + (jnp.log(jnp.maximum(n, 1).astype(jnp.float32) / max_exact)
                         / math.log(MAX_DISTANCE / max_exact) * (nb - max_exact)).astype(jnp.int32)
    large = jnp.minimum(large, nb - 1)
    return jnp.where(rel > 0, nb, 0) + jnp.where(n < max_exact, n, large)


def _bias_strip(rel_table, S):
    L = 2 * S
    buckets = _t5_buckets(jnp.arange(L, dtype=jnp.int32) - S).reshape(1, L)
    return pl.pallas_call(
        _bias_kernel,
        out_shape=jax.ShapeDtypeStruct((N_HEADS, L), F32),
        name="bias",
    )(buckets, rel_table)


def _inproj_kernel(x_ref, mod_ref, g1_ref, w_ref, gsum_ref, gspread_ref, qg_ref, kg_ref,
                   q_ref, kt_ref, v_ref, p_ref, ga_ref, gp_ref):
    D = x_ref.shape[1]
    x = x_ref[...]
    h = x * lax.rsqrt(jnp.mean(x * x, axis=-1, keepdims=True) + EPS) * g1_ref[...]
    h = h * (1.0 + mod_ref[0, 1:2, :]) + mod_ref[0, 0:1, :]
    hb = h.astype(BF16)

    def chunk(c):
        return jnp.dot(hb, w_ref[:, c * D:(c + 1) * D], preferred_element_type=F32)

    def head_norm(y, g_ref):
        ss = jnp.dot((y * y).astype(BF16), gsum_ref[...], preferred_element_type=F32)
        r = lax.rsqrt(ss * (1.0 / HEAD_DK) + EPS)
        r_hi = r.astype(BF16)
        r_lo = (r - r_hi.astype(F32)).astype(BF16)
        scale = jnp.dot(jnp.concatenate([r_hi, r_lo], axis=-1), gspread_ref[...],
                        preferred_element_type=F32)
        return y * scale * g_ref[...]

    q_ref[...] = head_norm(chunk(0), qg_ref).astype(BF16)
    kn = head_norm(chunk(1), kg_ref)
    for hd in range(N_HEADS):
        for t in range(x_ref.shape[0] // KEY_TILE):
            blk = kn[t * KEY_TILE:(t + 1) * KEY_TILE, hd * HEAD_DV:(hd + 1) * HEAD_DV]
            kt_ref[0, hd, t] = blk.T.astype(BF16)
    v_ref[...] = chunk(2).astype(BF16)
    p_ref[...] = chunk(3)
    ga_ref[...] = jax.nn.sigmoid(chunk(4)).astype(BF16)
    gp_ref[...] = jax.nn.sigmoid(chunk(5)).astype(BF16)


def _inproj(x2, mod3, norm1_g, w_in_b, qg, kg, S, tm):
    T, D = x2.shape
    grp = jnp.arange(D) // HEAD_DK
    gsum = (grp[:, None] == jnp.arange(LANES)[None, :]).astype(BF16)
    gspread = jnp.concatenate([gsum.T, gsum.T], axis=0)
    per_seq = S // tm
    row = lambda i: (i, 0)
    tok = jax.ShapeDtypeStruct((T, D), BF16)
    kt = jax.ShapeDtypeStruct((T // S, N_HEADS, S // KEY_TILE, HEAD_DV, KEY_TILE), BF16)
    outs = [tok, kt, tok, jax.ShapeDtypeStruct((T, D), F32), tok, tok]
    tok_spec = pl.BlockSpec((tm, D), row)
    kt_spec = pl.BlockSpec((1, N_HEADS, tm // KEY_TILE, HEAD_DV, KEY_TILE),
                           lambda i: (i // per_seq, 0, i % per_seq, 0, 0))
    return pl.pallas_call(
        _inproj_kernel,
        grid=(T // tm,),
        in_specs=[pl.BlockSpec((tm, D), row),
                  pl.BlockSpec((1, 6, D), lambda i: (i // per_seq, 0, 0)),
                  _const_spec((1, D)),
                  _const_spec(w_in_b.shape),
                  _const_spec(gsum.shape),
                  _const_spec(gspread.shape),
                  _const_spec((1, D)),
                  _const_spec((1, D))],
        out_specs=[tok_spec, kt_spec, tok_spec, tok_spec, tok_spec, tok_spec],
        out_shape=outs,
        compiler_params=_params("arbitrary"),
        name="inproj",
    )(x2, mod3, norm1_g, w_in_b, gsum, gspread, qg, kg)


def _attn_kernel(q_ref, kt_ref, v_ref, ga_ref, win_ref, lam_ref, sg_ref, o_ref,
                 bias_ref, x_ref, m_ref, acc_ref):
    tq = q_ref.shape[0]
    n_kt, _, kt = kt_ref.shape[2:]
    n = win_ref.shape[2]

    @pl.when(pl.program_id(2) == 0)
    def _():
        xb = jnp.broadcast_to(win_ref[0, 0], (tq, n))
        rolled = pltpu.roll(xb, n - tq, 1, stride=1, stride_axis=0)
        for j in range(n_kt):
            bias_ref[j] = rolled[:, j * kt:(j + 1) * kt]

    lp = lam_ref[...]
    lam = (jnp.exp(jnp.sum(lp[0:1] * lp[1:2], axis=-1, keepdims=True))
           - jnp.exp(jnp.sum(lp[2:3] * lp[3:4], axis=-1, keepdims=True)) + LAM_INIT)

    q = q_ref[...]
    first = lax.broadcasted_iota(jnp.int32, (1, HEAD_DV), 1) < HEAD_DK
    zero = jnp.zeros_like(q)
    qq = jnp.concatenate([jnp.where(first, q, zero), jnp.where(first, zero, q)], axis=0)

    for j in range(n_kt):
        b = bias_ref[j]
        x = jnp.dot(qq, kt_ref[0, 0, j], preferred_element_type=F32) + jnp.concatenate([b, b], axis=0)
        x_ref[j] = x
        t = x[:, 0:LANES]
        for c in range(1, kt // LANES):
            t = jnp.maximum(t, x[:, c * LANES:(c + 1) * LANES])
        m_ref[...] = t if j == 0 else jnp.maximum(m_ref[...], t)
    m = jnp.max(m_ref[...], axis=-1, keepdims=True)

    m_ref[...] = jnp.broadcast_to(m, m_ref.shape)

    def pv_tile(j, carry):
        mb = m_ref[...]
        mb = jnp.concatenate([mb] * (kt // LANES), axis=-1)
        ones_col = (lax.broadcasted_iota(jnp.int32, (kt, LANES), 1) == 0).astype(BF16)
        p = jnp.exp2((x_ref[j] - mb).astype(BF16))
        rows = pl.ds(pl.multiple_of(j * kt, kt), kt)
        v_ext = jnp.concatenate([v_ref[rows, :], ones_col], axis=-1)
        pv = jnp.concatenate([jnp.dot(p[:tq], v_ext, preferred_element_type=F32),
                              jnp.dot(p[tq:], v_ext, preferred_element_type=F32)], axis=0)
        acc_ref[...] = acc_ref[...] + pv
        return carry

    acc_ref[...] = jnp.zeros_like(acc_ref)
    lax.fori_loop(0, n_kt, pv_tile, 0, unroll=2)
    acc = acc_ref[...]
    o1, l1 = acc[:tq, :HEAD_DV], acc[:tq, HEAD_DV:HEAD_DV + 1]
    o2, l2 = acc[tq:, :HEAD_DV], acc[tq:, HEAD_DV:HEAD_DV + 1]
    o = o1 / l1 - lam * (o2 / l2)
    o = o * lax.rsqrt(jnp.mean(o * o, axis=-1, keepdims=True) + EPS) * sg_ref[...] * (1.0 - LAM_INIT)
    o_ref[...] = (o * ga_ref[...].astype(F32)).astype(BF16)


def _attention(q, kt, v, ga, win, lam_p, subln_g, B, S, tq):
    T, D = q.shape
    nq = S // tq
    n = win.shape[2]
    n_kt = S // KEY_TILE
    qmap = lambda h, i, b: (b * nq + i, h)
    return pl.pallas_call(
        _attn_kernel,
        grid=(N_HEADS, nq, B),
        in_specs=[pl.BlockSpec((tq, HEAD_DV), qmap),
                  pl.BlockSpec((1, 1, n_kt, HEAD_DV, KEY_TILE), lambda h, i, b: (b, h, 0, 0, 0)),
                  pl.BlockSpec((S, HEAD_DV), lambda h, i, b: (b, h)),
                  pl.BlockSpec((tq, HEAD_DV), qmap),
                  pl.BlockSpec((1, 1, n), lambda h, i, b: (h * nq + i, 0, 0)),
                  pl.BlockSpec((4, HEAD_DK), lambda h, i, b: (0, 0)),
                  pl.BlockSpec((1, HEAD_DV), lambda h, i, b: (0, 0))],
        out_specs=pl.BlockSpec((tq, HEAD_DV), qmap),
        out_shape=jax.ShapeDtypeStruct((T, D), BF16),
        scratch_shapes=[pltpu.VMEM((n_kt, tq, KEY_TILE), F32),
                        pltpu.VMEM((n_kt, 2 * tq, KEY_TILE), F32),
                        pltpu.VMEM((2 * tq, LANES), F32),
                        pltpu.VMEM((2 * tq, HEAD_DV + LANES), F32)],
        compiler_params=_params("arbitrary", "arbitrary", "arbitrary"),
        name="attn",
    )(q, kt, v, ga, win, lam_p, subln_g)


def _mix_kernel(S, x_ref, am_ref, gp_ref, p_ref, pprev_ref, pnext_ref, mod_ref, pw_ref, ps_ref,
                wo_ref, g2_ref, wr_ref, x1_ref, h2r_ref, lg_ref):
    tm, D = x_ref.shape
    gc = D // N_POOL_GROUPS
    halo = SUBLANES
    ne = tm + 2 * halo
    pos0 = (pl.program_id(0) % (S // tm)) * tm
    pos_e = pos0 - halo + lax.broadcasted_iota(jnp.int32, (ne, 1), 0)
    valid = (pos_e >= 0) & (pos_e < S)
    ext = jnp.concatenate([pprev_ref[...], p_ref[...], pnext_ref[...]], axis=0)
    ext = jnp.where(valid, ext, 0.0)
    pos = pos0 + lax.broadcasted_iota(jnp.int32, (tm, 1), 0)

    merged = []
    for g, w in enumerate(POOL_WINDOWS):
        half = w // 2
        e = ext[:, g * gc:(g + 1) * gc]
        sw = e
        width = 1
        while width < w:
            sw = sw + pltpu.roll(sw, width, 0)
            width *= 2
        win = pltpu.roll(sw, ne - (half - 1), 0)[halo:halo + tm] if half > 1 else sw[halo:halo + tm]
        lo = jnp.clip(pos - half, 0, S - 1)
        hi = jnp.clip(pos + half - 1, 0, S - 1)
        cnt = (hi - lo + 1).astype(F32)
        mixed = win / cnt - p_ref[:, g * gc:(g + 1) * gc]
        pooled = jnp.dot(mixed.astype(BF16), pw_ref[g], preferred_element_type=F32)
        pooled = pooled * ps_ref[:, g * gc:(g + 1) * gc]
        merged.append(am_ref[:, g * gc:(g + 1) * gc].astype(F32)
                      + gp_ref[:, g * gc:(g + 1) * gc].astype(F32) * pooled)
    merged = jnp.concatenate(merged, axis=-1).astype(BF16)
    y = jnp.dot(merged, wo_ref[...], preferred_element_type=F32)
    x1 = x_ref[...] + mod_ref[0, 2:3, :] * y
    x1_ref[...] = x1
    h2 = x1 * lax.rsqrt(jnp.mean(x1 * x1, axis=-1, keepdims=True) + EPS) * g2_ref[...]
    h2 = h2 * (1.0 + mod_ref[0, 4:5, :]) + mod_ref[0, 3:4, :]
    _rows_store(h2r_ref, 0, h2)
    E = lg_ref.shape[0]
    nt = (((1,), (1,)), ((), ()))
    h_hi = h2.astype(BF16)
    h_lo = (h2 - h_hi.astype(F32)).astype(BF16)
    both = lax.dot_general(wr_ref[...], h_hi, nt, preferred_element_type=F32)
    lg_ref[...] = both[:E] + both[E:] + lax.dot_general(wr_ref[:E, :], h_lo, nt, preferred_element_type=F32)


def _mix(x2, am, gp, p_in, mod3, pool_w_b, pool_scale, w_out_b, norm2_g, w_router_t, S, tm):
    T, D = x2.shape
    per_seq = S // tm
    hb = tm // SUBLANES
    last = T // SUBLANES - 1
    row = lambda i: (i, 0)
    E = w_router_t.shape[0]
    w_hi = w_router_t.astype(BF16)
    w_router_t = jnp.concatenate([w_hi, (w_router_t - w_hi.astype(F32)).astype(BF16)], axis=0)
    return pl.pallas_call(
        functools.partial(_mix_kernel, S),
        grid=(T // tm,),
        in_specs=[pl.BlockSpec((tm, D), row),
                  pl.BlockSpec((tm, D), row),
                  pl.BlockSpec((tm, D), row),
                  pl.BlockSpec((tm, D), row),
                  pl.BlockSpec((SUBLANES, D), lambda i: (jnp.maximum(i * hb - 1, 0), 0)),
                  pl.BlockSpec((SUBLANES, D), lambda i: (jnp.minimum((i + 1) * hb, last), 0)),
                  pl.BlockSpec((1, 6, D), lambda i: (i // per_seq, 0, 0)),
                  _const_spec(pool_w_b.shape),
                  _const_spec((1, D)),
                  _const_spec((D, D)),
                  _const_spec((1, D)),
                  _const_spec((2 * E, D))],
        out_specs=[pl.BlockSpec((tm, D), row),
                   pl.BlockSpec((tm * SUBLANES, LANES), row),
                   pl.BlockSpec((E, tm), lambda i: (0, i))],
        out_shape=[jax.ShapeDtypeStruct((T, D), F32),
                   jax.ShapeDtypeStruct((T * SUBLANES, LANES), F32),
                   jax.ShapeDtypeStruct((E, T), F32)],
        compiler_params=_params("arbitrary"),
        name="mix",
    )(x2, am, gp, p_in, p_in, p_in, mod3, pool_w_b, pool_scale, w_out_b, norm2_g, w_router_t)


def _route_kernel(lg_ref, br_ref, tri_ref, eidx_ref, rank_ref, wts_ref, cnt_ref):
    E, tr = lg_ref.shape
    per = E // N_EXPERT_GROUPS

    @pl.when(pl.program_id(0) == 0)
    def _():
        cnt_ref[...] = jnp.zeros_like(cnt_ref)

    scores = jax.nn.sigmoid(lg_ref[...])
    biased = scores + br_ref[...]
    b3 = biased.reshape(N_EXPERT_GROUPS, per, tr)
    io_per = lax.broadcasted_iota(jnp.int32, b3.shape, 1)
    m1 = jnp.max(b3, axis=1, keepdims=True)
    i1 = jnp.min(jnp.where(b3 == m1, io_per, per), axis=1, keepdims=True)
    m2 = jnp.max(jnp.where(io_per == i1, NEG_INF, b3), axis=1, keepdims=True)
    gs = (m1 + m2)[:, 0, :]

    io_g = lax.broadcasted_iota(jnp.int32, gs.shape, 0)
    gsel = jnp.zeros(gs.shape, jnp.bool_)
    cur = gs
    for _ in range(TOPK_GROUPS):
        gm = jnp.max(cur, axis=0, keepdims=True)
        gi = jnp.min(jnp.where(cur == gm, io_g, N_EXPERT_GROUPS), axis=0, keepdims=True)
        pick = io_g == gi
        gsel = gsel | pick
        cur = jnp.where(pick, NEG_INF, cur)
    cur = jnp.where(gsel[:, None, :], b3, NEG_INF).reshape(E, tr)

    io_e = lax.broadcasted_iota(jnp.int32, (E, tr), 0)
    idxs, raw = [], []
    sel = jnp.zeros((E, tr), jnp.bool_)
    for _ in range(TOP_K):
        m = jnp.max(cur, axis=0, keepdims=True)
        idx = jnp.min(jnp.where(cur == m, io_e, E), axis=0, keepdims=True)
        pick = io_e == idx
        idxs.append(idx)
        raw.append(jnp.sum(jnp.where(pick, scores, 0.0), axis=0, keepdims=True))
        sel = sel | pick
        cur = jnp.where(pick, NEG_INF, cur)
    raw = jnp.concatenate(raw, axis=0)
    wts_ref[...] = raw / jnp.sum(raw, axis=0, keepdims=True) * ROUTED_SCALE
    eidx_ref[...] = jnp.concatenate(idxs, axis=0)

    self_f = sel.astype(F32)
    incl = jnp.dot(sel.astype(BF16), tri_ref[...], preferred_element_type=F32)
    before = cnt_ref[...] + incl - self_f
    ranks = [jnp.sum(jnp.where(io_e == idx, before, 0.0), axis=0, keepdims=True) for idx in idxs]
    rank_ref[...] = jnp.concatenate(ranks, axis=0).astype(jnp.int32)
    cnt_ref[...] = cnt_ref[...] + jnp.sum(self_f, axis=1, keepdims=True)


def _route(logits_t, b_router, tr):
    E, T = logits_t.shape
    tri = (jnp.arange(tr)[:, None] <= jnp.arange(tr)[None, :]).astype(BF16)
    col = lambda i: (0, i)
    return pl.pallas_call(
        _route_kernel,
        grid=(T // tr,),
        in_specs=[pl.BlockSpec((E, tr), col),
                  _const_spec((E, 1)),
                  _const_spec((tr, tr))],
        out_specs=[pl.BlockSpec((TOP_K, tr), col),
                   pl.BlockSpec((TOP_K, tr), col),
                   pl.BlockSpec((TOP_K, tr), col),
                   pl.BlockSpec((E, 1), lambda i: (0, 0))],
        out_shape=[jax.ShapeDtypeStruct((TOP_K, T), jnp.int32),
                   jax.ShapeDtypeStruct((TOP_K, T), jnp.int32),
                   jax.ShapeDtypeStruct((TOP_K, T), F32),
                   jax.ShapeDtypeStruct((E, 1), F32)],
        compiler_params=_params("arbitrary"),
        name="route",
    )(logits_t, b_router.reshape(E, 1), tri)


def _dest_kernel(eidx_ref, rank_ref, base_ref, dest_ref):
    K, tr = eidx_ref.shape
    E = base_ref.shape[0]
    io_e = lax.broadcasted_iota(jnp.int32, (E, tr), 0)
    base = base_ref[...]
    rows = [jnp.sum(jnp.where(io_e == eidx_ref[k:k + 1, :], base, 0.0), axis=0, keepdims=True)
            for k in range(K)]
    dest_ref[...] = jnp.concatenate(rows, axis=0).astype(jnp.int32) + rank_ref[...]


def _dest(eidx, rank, base, tr):
    K, T = eidx.shape
    col = lambda i: (0, i)
    return pl.pallas_call(
        _dest_kernel,
        grid=(T // tr,),
        in_specs=[pl.BlockSpec((K, tr), col), pl.BlockSpec((K, tr), col), _const_spec(base.shape)],
        out_specs=pl.BlockSpec((K, tr), col),
        out_shape=jax.ShapeDtypeStruct((K, T), jnp.int32),
        compiler_params=_params("arbitrary"),
        name="dest",
    )(eidx, rank, base)


def _dispatch_kernel(lastblk_ref, dest_ref, h_ref, wsg_ref, wsu_ref, wsd_ref, xs_ref, sh_ref, zbuf_ref, sem):
    C = SUBLANES
    td = h_ref.shape[0] // C
    RC = zbuf_ref.shape[0]

    def slab(ref, r):
        return ref.at[pl.ds(pl.multiple_of(r * C, C), C)]

    @pl.when(pl.program_id(0) == 0)
    def _():
        zbuf_ref[...] = jnp.zeros_like(zbuf_ref)

        def zcopy(e):
            start = pl.multiple_of(lastblk_ref[e] * RC, RC)
            return pltpu.make_async_copy(zbuf_ref, xs_ref.at[pl.ds(start, RC)], sem)

        def start(e, c):
            @pl.when(lastblk_ref[e] >= 0)
            def _():
                zcopy(e).start()
            return c

        def wait(e, c):
            @pl.when(lastblk_ref[e] >= 0)
            def _():
                zcopy(e).wait()
            return c

        lax.fori_loop(0, N_EXPERTS, start, 0)
        lax.fori_loop(0, N_EXPERTS, wait, 0)

    def row_copy(t, k):
        return pltpu.make_async_copy(slab(h_ref, t), slab(xs_ref, dest_ref[k, t]), sem)

    def start_rows(t, c):
        for k in range(TOP_K):
            row_copy(t, k).start(priority=k % 2)
        return c

    def wait_rows():
        for _ in range(TOP_K):
            pltpu.make_async_copy(h_ref, xs_ref.at[pl.ds(0, td * C)], sem).wait()

    lax.fori_loop(0, td, start_rows, 0, unroll=4)

    h = _rows_load(h_ref, 0, td).astype(BF16)
    g = jnp.dot(h, wsg_ref[...], preferred_element_type=F32)
    u = jnp.dot(h, wsu_ref[...], preferred_element_type=F32)
    shared = jnp.dot((g * jax.nn.sigmoid(g) * u).astype(BF16), wsd_ref[...], preferred_element_type=F32)
    sh_ref[...] = shared.astype(BF16)

    wait_rows()


def _dispatch(lastblk, dest, h2r, wsg, wsu, wsd, n_pad, td):
    C = SUBLANES
    T = h2r.shape[0] // C
    D = C * LANES
    const = lambda shape: pl.BlockSpec(shape, lambda i, lb: (0,) * len(shape), pipeline_mode=pl.Buffered(1))
    return pl.pallas_call(
        _dispatch_kernel,
        grid_spec=pltpu.PrefetchScalarGridSpec(
            num_scalar_prefetch=1,
            grid=(T // td,),
            in_specs=[pl.BlockSpec((TOP_K, td), lambda i, lb: (0, i), memory_space=pltpu.SMEM),
                      pl.BlockSpec((td * C, LANES), lambda i, lb: (i, 0)),
                      const(wsg.shape), const(wsu.shape), const(wsd.shape)],
            out_specs=[pl.BlockSpec(memory_space=pl.ANY),
                       pl.BlockSpec((td, D), lambda i, lb: (i, 0))],
            scratch_shapes=[pltpu.VMEM((MOE_BLOCK * C, LANES), F32), pltpu.SemaphoreType.DMA]),
        out_shape=[jax.ShapeDtypeStruct((n_pad * C, LANES), F32),
                   jax.ShapeDtypeStruct((T, D), BF16)],
        compiler_params=_params("arbitrary"),
        name="dispatch",
    )(lastblk, dest, h2r, wsg, wsu, wsd)


def _expert_kernel(be_ref, nused_ref, xs_ref, wg_ref, wu_ref, wd_ref, ys_ref):
    R = xs_ref.shape[0] // SUBLANES

    @pl.when(pl.program_id(0) < nused_ref[0])
    def _():
        x = _rows_load(xs_ref, 0, R).astype(BF16)
        g = jnp.dot(x, wg_ref[0].astype(BF16), preferred_element_type=F32)
        u = jnp.dot(x, wu_ref[0].astype(BF16), preferred_element_type=F32)
        hmid = (g * jax.nn.sigmoid(g) * u).astype(BF16)
        y = jnp.dot(hmid, wd_ref[0].astype(BF16), preferred_element_type=F32)
        _rows_store(ys_ref, 0, y)

    @pl.when(pl.program_id(0) >= nused_ref[0])
    def _():
        ys_ref[...] = jnp.zeros_like(ys_ref)


def _experts(blk_expert, nused, xs, w_eg, w_eu, w_ed):
    RC = MOE_BLOCK * SUBLANES
    n_blocks = xs.shape[0] // RC
    _, D, DE = w_eg.shape
    blk = lambda j, be, nu: jnp.minimum(j, nu[0] - 1)
    wmap = lambda j, be, nu: (be[blk(j, be, nu)], 0, 0)
    return pl.pallas_call(
        _expert_kernel,
        grid_spec=pltpu.PrefetchScalarGridSpec(
            num_scalar_prefetch=2,
            grid=(n_blocks,),
            in_specs=[pl.BlockSpec((RC, LANES), lambda j, be, nu: (blk(j, be, nu), 0)),
                      pl.BlockSpec((1, D, DE), wmap),
                      pl.BlockSpec((1, D, DE), wmap),
                      pl.BlockSpec((1, DE, D), wmap)],
            out_specs=pl.BlockSpec((RC, LANES), lambda j, be, nu: (j, 0))),
        out_shape=jax.ShapeDtypeStruct(xs.shape, F32),
        compiler_params=_params("arbitrary"),
        name="expert",
    )(blk_expert, nused, xs, w_eg, w_eu, w_ed)


def _combine_kernel(dest_ref, dnext_ref, wt_ref, ys_ref, x1_ref, sh_ref, mod_ref, o_ref, buf_ref, sems):
    tc = x1_ref.shape[0]
    C = SUBLANES
    i = pl.program_id(0)
    n = pl.num_programs(0)
    slot = i % 2
    per_slot = TOP_K * tc

    def slab(ref, r):
        return ref.at[pl.ds(pl.multiple_of(r * C, C), C)]

    def row_copy(d_ref, s, t, k):
        return pltpu.make_async_copy(slab(ys_ref, d_ref[k, t]), slab(buf_ref, s * per_slot + k * tc + t),
                                     sems.at[s])

    def start_rows(d_ref, s, t):
        for k in range(TOP_K):
            row_copy(d_ref, s, t, k).start(priority=k % 2)

    def start_tile(d_ref, s):
        def body(t, c):
            start_rows(d_ref, s, t)
            return c
        lax.fori_loop(0, tc, body, 0, unroll=4)

    @pl.when(i == 0)
    def _():
        start_tile(dest_ref, slot)

    @pl.when(i + 1 < n)
    def _():
        start_tile(dnext_ref, 1 - slot)

    slot_rows = per_slot * C
    pltpu.make_async_copy(ys_ref.at[pl.ds(0, slot_rows)],
                          buf_ref.at[pl.ds(pl.multiple_of(slot * slot_rows, slot_rows), slot_rows)],
                          sems.at[slot]).wait()

    g2 = mod_ref[0, 5:6, :]
    o_ref[...] = x1_ref[...] + g2 * sh_ref[...].astype(F32)
    for k in range(TOP_K):
        w = wt_ref[:, k:k + 1] * g2
        o_ref[...] = o_ref[...] + _rows_load(buf_ref, slot * per_slot + k * tc, tc) * w


def _combine(dest, wts_t, ys, x1, sh, mod3, S, tc):
    T, D = x1.shape
    per_seq = S // tc
    last = T // tc - 1
    row = lambda i: (i, 0)
    return pl.pallas_call(
        _combine_kernel,
        grid=(T // tc,),
        in_specs=[pl.BlockSpec((TOP_K, tc), lambda i: (0, i), memory_space=pltpu.SMEM),
                  pl.BlockSpec((TOP_K, tc), lambda i: (0, jnp.minimum(i + 1, last)), memory_space=pltpu.SMEM),
                  pl.BlockSpec((tc, TOP_K), row),
                  pl.BlockSpec(memory_space=pl.ANY),
                  pl.BlockSpec((tc, D), row),
                  pl.BlockSpec((tc, D), row),
                  pl.BlockSpec((1, 6, D), lambda i: (i // per_seq, 0, 0))],
        out_specs=pl.BlockSpec((tc, D), row),
        out_shape=jax.ShapeDtypeStruct((T, D), F32),
        scratch_shapes=[pltpu.VMEM((2 * TOP_K * tc * SUBLANES, LANES), F32), pltpu.SemaphoreType.DMA((2,))],
        compiler_params=_params("arbitrary"),
        name="combine",
    )(dest, dest, wts_t, ys, x1, sh, mod3)


def _tile(n, want):
    t = min(n, want)
    while n % t:
        t //= 2
    return t


def kernel(x, c, rel_bias_table, w_ada, b_ada, norm1_g, w_in, q_norm_g, k_norm_g, lambda_q1, lambda_k1,
           lambda_q2, lambda_k2, subln_g, pool_w, pool_scale, w_out, norm2_g, w_router, b_router,
           w_exp_gate, w_exp_up, w_exp_down, w_sh_gate, w_sh_up, w_sh_down):
    B, S, D = x.shape
    T = B * S
    depth = w_ada.shape[0]
    assert depth == 1, "LAM_INIT and the single-layer pipeline assume depth 1"
    assert D == N_HEADS * HEAD_DV and S % LANES == 0
    l = 0
    tm = _tile(S, 512)
    tq = _tile(S, 1024)

    x2 = x.reshape(T, D)
    mod3 = _ada(c, w_ada[l], b_ada[l]).reshape(B, 6, D)

    qg = (jnp.tile(q_norm_g[l], D // HEAD_DK) * (HEAD_DK ** -0.5 * LOG2E)).reshape(1, D)
    kg = jnp.tile(k_norm_g[l], D // HEAD_DK).reshape(1, D)
    q, k, v, p_in, ga, gp = _inproj(x2, mod3, norm1_g[l].reshape(1, D), w_in[l].astype(BF16),
                                    qg, kg, S, tm)

    strip = _bias_strip(rel_bias_table, S) * LOG2E
    nq = S // tq
    n = S + tq
    win = jnp.stack([strip[:, S - (i + 1) * tq: S - (i + 1) * tq + n] for i in range(nq)], axis=1)
    win = win.reshape(N_HEADS * nq, 1, n)
    lam_p = jnp.stack([lambda_q1[l], lambda_k1[l], lambda_q2[l], lambda_k2[l]])
    am = _attention(q, k, v, ga, win, lam_p, subln_g[l].reshape(1, HEAD_DV), B, S, tq)

    x1, h2r, logits_t = _mix(x2, am, gp, p_in, mod3, pool_w[l].astype(BF16),
                                  pool_scale[l].reshape(1, D), w_out[l].astype(BF16),
                                  norm2_g[l].reshape(1, D), w_router[l].T, S, tm)

    eidx, rank, wts, counts = _route(logits_t, b_router[l], _tile(T, 512))

    cnt = counts[:, 0].astype(jnp.int32)
    nblk = (cnt + MOE_BLOCK - 1) // MOE_BLOCK
    blk_end = jnp.cumsum(nblk)
    base = ((blk_end - nblk) * MOE_BLOCK).astype(F32).reshape(N_EXPERTS, 1)
    lastblk = jnp.where(nblk > 0, blk_end - 1, -1).astype(jnp.int32)
    n_blocks = -(-(T * TOP_K) // MOE_BLOCK) + N_EXPERTS
    blk_expert = jnp.minimum(jnp.sum(blk_end[None, :] <= jnp.arange(n_blocks)[:, None], axis=1),
                             N_EXPERTS - 1).astype(jnp.int32)
    nused = blk_end[-1:].astype(jnp.int32)

    dest = _dest(eidx, rank, base, _tile(T, 512))
    xs, sh = _dispatch(lastblk, dest, h2r, w_sh_gate[l].astype(BF16), w_sh_up[l].astype(BF16),
                       w_sh_down[l].astype(BF16), n_blocks * MOE_BLOCK, _tile(S, 256))
    ys = _experts(blk_expert, nused, xs, w_exp_gate[l], w_exp_up[l], w_exp_down[l])
    out = _combine(dest, wts.T, ys, x1, sh, mod3, S, _tile(S, 128))
    return out.reshape(B, S, D)
```

```python
import functools
import math

import jax
import jax.numpy as jnp
from jax import lax
from jax.experimental import pallas as pl
from jax.experimental.pallas import tpu as pltpu

N_HEADS = 8
HEAD_DK = 64
HEAD_DV = 2 * HEAD_DK
N_BUCKETS = 32
MAX_DISTANCE = 128
POOL_WINDOWS = (2, 4, 8, 16)
N_POOL_GROUPS = 4
N_EXPERTS = 256
TOP_K = 8
N_EXPERT_GROUPS = 8
TOPK_GROUPS = 4
ROUTED_SCALE = 2.5
MOE_BLOCK = 512
EXPERT_IN_BUFS = 3
EXPERT_OUT_BUFS = 2
EPS = 1e-6
LAM_INIT = 0.8 - 0.6 * math.exp(-0.3 * 0)
LOG2E = math.log2(math.e)
KEY_TILE = 512

LANES = 128
SUBLANES = 8
VMEM_LIMIT = 56 * 1024 * 1024

F32 = jnp.float32
BF16 = jnp.bfloat16
HIGHEST = lax.Precision.HIGHEST
NEG_INF = float("-inf")


def _params(*sem):
    return pltpu.CompilerParams(dimension_semantics=sem, vmem_limit_bytes=VMEM_LIMIT)


def _const_spec(shape):
    nd = len(shape)
    return pl.BlockSpec(shape, lambda *_: (0,) * nd, pipeline_mode=pl.Buffered(1))


def _rows_load(ref, first, n, chunks=SUBLANES):
    return jnp.concatenate([ref[pl.ds(first * chunks + c, n, stride=chunks), :] for c in range(chunks)],
                           axis=-1)


def _rows_store(ref, first, val, chunks=SUBLANES):
    n = val.shape[0]
    for c in range(chunks):
        ref[pl.ds(first * chunks + c, n, stride=chunks), :] = val[:, c * LANES:(c + 1) * LANES]


def _ada_kernel(c_ref, w_ref, b_ref, o_ref):
    c = c_ref[...]
    s = c * jax.nn.sigmoid(c)
    o_ref[...] = jnp.dot(s, w_ref[...], precision=HIGHEST, preferred_element_type=F32) + b_ref[...]


def _ada(c, w_ada, b_ada):
    B, D = c.shape
    n = w_ada.shape[1] // D
    return pl.pallas_call(
        _ada_kernel,
        grid=(n,),
        in_specs=[pl.BlockSpec((B, D), lambda j: (0, 0)),
                  pl.BlockSpec((D, D), lambda j: (0, j)),
                  pl.BlockSpec((1, D), lambda j: (0, j))],
        out_specs=pl.BlockSpec((B, D), lambda j: (0, j)),
        out_shape=jax.ShapeDtypeStruct((B, n * D), F32),
        compiler_params=_params("arbitrary"),
        name="ada",
    )(c, w_ada, b_ada.reshape(1, -1))


def _bias_kernel(bk_ref, tab_ref, o_ref):
    bk = bk_ref[...]
    onehot = (lax.broadcasted_iota(jnp.int32, (N_BUCKETS, bk.shape[1]), 0) == bk).astype(F32)
    o_ref[...] = lax.dot_general(tab_ref[...], onehot, (((0,), (0,)), ((), ())),
                                 precision=HIGHEST, preferred_element_type=F32)


def _t5_buckets(rel):
    nb = N_BUCKETS // 2
    max_exact = nb // 2
    n = jnp.abs(rel)
    large = max_exact + (jnp.log(jnp.maximum(n, 1).astype(jnp.float32) / max_exact)
                         / math.log(MAX_DISTANCE / max_exact) * (nb - max_exact)).astype(jnp.int32)
    large = jnp.minimum(large, nb - 1)
    return jnp.where(rel > 0, nb, 0) + jnp.where(n < max_exact, n, large)


def _bias_strip(rel_table, S):
    L = 2 * S
    buckets = _t5_buckets(jnp.arange(L, dtype=jnp.int32) - S).reshape(1, L)
    return pl.pallas_call(
        _bias_kernel,
        out_shape=jax.ShapeDtypeStruct((N_HEADS, L), F32),
        name="bias",
    )(buckets, rel_table)


def _inproj_kernel(x_ref, mod_ref, g1_ref, w_ref, gsum_ref, gspread_ref, qg_ref, kg_ref,
                   q_ref, kt_ref, v_ref, p_ref, ga_ref, gp_ref):
    D = x_ref.shape[1]
    x = x_ref[...]
    h = x * lax.rsqrt(jnp.mean(x * x, axis=-1, keepdims=True) + EPS) * g1_ref[...]
    h = h * (1.0 + mod_ref[0, 1:2, :]) + mod_ref[0, 0:1, :]
    hb = h.astype(BF16)

    def chunk(c):
        return jnp.dot(hb, w_ref[:, c * D:(c + 1) * D], preferred_element_type=F32)

    def head_norm(y, g_ref):
        ss = jnp.dot((y * y).astype(BF16), gsum_ref[...], preferred_element_type=F32)
        r = lax.rsqrt(ss * (1.0 / HEAD_DK) + EPS)
        r_hi = r.astype(BF16)
        r_lo = (r - r_hi.astype(F32)).astype(BF16)
        scale = jnp.dot(jnp.concatenate([r_hi, r_lo], axis=-1), gspread_ref[...],
                        preferred_element_type=F32)
        return y * scale * g_ref[...]

    q_ref[...] = head_norm(chunk(0), qg_ref).astype(BF16)
    kn = head_norm(chunk(1), kg_ref)
    for hd in range(N_HEADS):
        for t in range(x_ref.shape[0] // KEY_TILE):
            blk = kn[t * KEY_TILE:(t + 1) * KEY_TILE, hd * HEAD_DV:(hd + 1) * HEAD_DV]
            kt_ref[0, hd, t] = blk.T.astype(BF16)
    v_ref[...] = chunk(2).astype(BF16)
    p_ref[...] = chunk(3)
    ga_ref[...] = jax.nn.sigmoid(chunk(4)).astype(BF16)
    gp_ref[...] = jax.nn.sigmoid(chunk(5)).astype(BF16)


def _inproj(x2, mod3, norm1_g, w_in_b, qg, kg, S, tm):
    T, D = x2.shape
    grp = jnp.arange(D) // HEAD_DK
    gsum = (grp[:, None] == jnp.arange(LANES)[None, :]).astype(BF16)
    gspread = jnp.concatenate([gsum.T, gsum.T], axis=0)
    per_seq = S // tm
    row = lambda i: (i, 0)
    tok = jax.ShapeDtypeStruct((T, D), BF16)
    kt = jax.ShapeDtypeStruct((T // S, N_HEADS, S // KEY_TILE, HEAD_DV, KEY_TILE), BF16)
    outs = [tok, kt, tok, jax.ShapeDtypeStruct((T, D), F32), tok, tok]
    tok_spec = pl.BlockSpec((tm, D), row)
    kt_spec = pl.BlockSpec((1, N_HEADS, tm // KEY_TILE, HEAD_DV, KEY_TILE),
                           lambda i: (i // per_seq, 0, i % per_seq, 0, 0))
    return pl.pallas_call(
        _inproj_kernel,
        grid=(T // tm,),
        in_specs=[pl.BlockSpec((tm, D), row),
                  pl.BlockSpec((1, 6, D), lambda i: (i // per_seq, 0, 0)),
                  _const_spec((1, D)),
                  _const_spec(w_in_b.shape),
                  _const_spec(gsum.shape),
                  _const_spec(gspread.shape),
                  _const_spec((1, D)),
                  _const_spec((1, D))],
        out_specs=[tok_spec, kt_spec, tok_spec, tok_spec, tok_spec, tok_spec],
        out_shape=outs,
        compiler_params=_params("arbitrary"),
        name="inproj",
    )(x2, mod3, norm1_g, w_in_b, gsum, gspread, qg, kg)


def _attn_kernel(q_ref, kt_ref, v_ref, ga_ref, win_ref, lam_ref, sg_ref, o_ref,
                 bias_ref, x_ref, m_ref, acc_ref):
    tq = q_ref.shape[0]
    n_kt, _, kt = kt_ref.shape[2:]
    n = win_ref.shape[2]

    @pl.when(pl.program_id(2) == 0)
    def _():
        xb = jnp.broadcast_to(win_ref[0, 0], (tq, n))
        rolled = pltpu.roll(xb, n - tq, 1, stride=1, stride_axis=0)
        for j in range(n_kt):
            bias_ref[j] = rolled[:, j * kt:(j + 1) * kt]

    lp = lam_ref[...]
    lam = (jnp.exp(jnp.sum(lp[0:1] * lp[1:2], axis=-1, keepdims=True))
           - jnp.exp(jnp.sum(lp[2:3] * lp[3:4], axis=-1, keepdims=True)) + LAM_INIT)

    q = q_ref[...]
    first = lax.broadcasted_iota(jnp.int32, (1, HEAD_DV), 1) < HEAD_DK
    zero = jnp.zeros_like(q)
    qq = jnp.concatenate([jnp.where(first, q, zero), jnp.where(first, zero, q)], axis=0)

    for j in range(n_kt):
        b = bias_ref[j]
        x = jnp.dot(qq, kt_ref[0, 0, j], preferred_element_type=F32) + jnp.concatenate([b, b], axis=0)
        x_ref[j] = x
        t = x[:, 0:LANES]
        for c in range(1, kt // LANES):
            t = jnp.maximum(t, x[:, c * LANES:(c + 1) * LANES])
        m_ref[...] = t if j == 0 else jnp.maximum(m_ref[...], t)
    m = jnp.max(m_ref[...], axis=-1, keepdims=True)

    m_ref[...] = jnp.broadcast_to(m, m_ref.shape)

    def pv_tile(j, carry):
        mb = m_ref[...]
        mb = jnp.concatenate([mb] * (kt // LANES), axis=-1)
        ones_col = (lax.broadcasted_iota(jnp.int32, (kt, LANES), 1) == 0).astype(BF16)
        p = jnp.exp2((x_ref[j] - mb).astype(BF16))
        rows = pl.ds(pl.multiple_of(j * kt, kt), kt)
        v_ext = jnp.concatenate([v_ref[rows, :], ones_col], axis=-1)
        pv = jnp.concatenate([jnp.dot(p[:tq], v_ext, preferred_element_type=F32),
                              jnp.dot(p[tq:], v_ext, preferred_element_type=F32)], axis=0)
        acc_ref[...] = acc_ref[...] + pv
        return carry

    acc_ref[...] = jnp.zeros_like(acc_ref)
    lax.fori_loop(0, n_kt, pv_tile, 0, unroll=2)
    acc = acc_ref[...]
    o1, l1 = acc[:tq, :HEAD_DV], acc[:tq, HEAD_DV:HEAD_DV + 1]
    o2, l2 = acc[tq:, :HEAD_DV], acc[tq:, HEAD_DV:HEAD_DV + 1]
    o = o1 / l1 - lam * (o2 / l2)
    o = o * lax.rsqrt(jnp.mean(o * o, axis=-1, keepdims=True) + EPS) * sg_ref[...] * (1.0 - LAM_INIT)
    o_ref[...] = (o * ga_ref[...].astype(F32)).astype(BF16)


def _attention(q, kt, v, ga, win, lam_p, subln_g, B, S, tq):
    T, D = q.shape
    nq = S // tq
    n = win.shape[2]
    n_kt = S // KEY_TILE
    qmap = lambda h, i, b: (b * nq + i, h)
    return pl.pallas_call(
        _attn_kernel,
        grid=(N_HEADS, nq, B),
        in_specs=[pl.BlockSpec((tq, HEAD_DV), qmap),
                  pl.BlockSpec((1, 1, n_kt, HEAD_DV, KEY_TILE), lambda h, i, b: (b, h, 0, 0, 0)),
                  pl.BlockSpec((S, HEAD_DV), lambda h, i, b: (b, h)),
                  pl.BlockSpec((tq, HEAD_DV), qmap),
                  pl.BlockSpec((1, 1, n), lambda h, i, b: (h * nq + i, 0, 0)),
                  pl.BlockSpec((4, HEAD_DK), lambda h, i, b: (0, 0)),
                  pl.BlockSpec((1, HEAD_DV), lambda h, i, b: (0, 0))],
        out_specs=pl.BlockSpec((tq, HEAD_DV), qmap),
        out_shape=jax.ShapeDtypeStruct((T, D), BF16),
        scratch_shapes=[pltpu.VMEM((n_kt, tq, KEY_TILE), F32),
                        pltpu.VMEM((n_kt, 2 * tq, KEY_TILE), F32),
                        pltpu.VMEM((2 * tq, LANES), F32),
                        pltpu.VMEM((2 * tq, HEAD_DV + LANES), F32)],
        compiler_params=_params("arbitrary", "arbitrary", "arbitrary"),
        name="attn",
    )(q, kt, v, ga, win, lam_p, subln_g)


def _mix_kernel(S, x_ref, am_ref, gp_ref, p_ref, pprev_ref, pnext_ref, mod_ref, pw_ref, ps_ref,
                wo_ref, g2_ref, wr_ref, x1_ref, h2r_ref, lg_ref):
    tm, D = x_ref.shape
    gc = D // N_POOL_GROUPS
    halo = SUBLANES
    ne = tm + 2 * halo
    pos0 = (pl.program_id(0) % (S // tm)) * tm
    pos_e = pos0 - halo + lax.broadcasted_iota(jnp.int32, (ne, 1), 0)
    valid = (pos_e >= 0) & (pos_e < S)
    ext = jnp.concatenate([pprev_ref[...], p_ref[...], pnext_ref[...]], axis=0)
    ext = jnp.where(valid, ext, 0.0)
    pos = pos0 + lax.broadcasted_iota(jnp.int32, (tm, 1), 0)

    merged = []
    for g, w in enumerate(POOL_WINDOWS):
        half = w // 2
        e = ext[:, g * gc:(g + 1) * gc]
        sw = e
        width = 1
        while width < w:
            sw = sw + pltpu.roll(sw, width, 0)
            width *= 2
        win = pltpu.roll(sw, ne - (half - 1), 0)[halo:halo + tm] if half > 1 else sw[halo:halo + tm]
        lo = jnp.clip(pos - half, 0, S - 1)
        hi = jnp.clip(pos + half - 1, 0, S - 1)
        cnt = (hi - lo + 1).astype(F32)
        mixed = win / cnt - p_ref[:, g * gc:(g + 1) * gc]
        pooled = jnp.dot(mixed.astype(BF16), pw_ref[g], preferred_element_type=F32)
        pooled = pooled * ps_ref[:, g * gc:(g + 1) * gc]
        merged.append(am_ref[:, g * gc:(g + 1) * gc].astype(F32)
                      + gp_ref[:, g * gc:(g + 1) * gc].astype(F32) * pooled)
    merged = jnp.concatenate(merged, axis=-1).astype(BF16)
    y = jnp.dot(merged, wo_ref[...], preferred_element_type=F32)
    x1 = x_ref[...] + mod_ref[0, 2:3, :] * y
    x1_ref[...] = x1
    h2 = x1 * lax.rsqrt(jnp.mean(x1 * x1, axis=-1, keepdims=True) + EPS) * g2_ref[...]
    h2 = h2 * (1.0 + mod_ref[0, 4:5, :]) + mod_ref[0, 3:4, :]
    _rows_store(h2r_ref, 0, h2)
    E = lg_ref.shape[0]
    nt = (((1,), (1,)), ((), ()))
    h_hi = h2.astype(BF16)
    h_lo = (h2 - h_hi.astype(F32)).astype(BF16)
    both = lax.dot_general(wr_ref[...], h_hi, nt, preferred_element_type=F32)
    lg_ref[...] = both[:E] + both[E:] + lax.dot_general(wr_ref[:E, :], h_lo, nt, preferred_element_type=F32)


def _mix(x2, am, gp, p_in, mod3, pool_w_b, pool_scale, w_out_b, norm2_g, w_router_t, S, tm):
    T, D = x2.shape
    per_seq = S // tm
    hb = tm // SUBLANES
    last = T // SUBLANES - 1
    row = lambda i: (i, 0)
    E = w_router_t.shape[0]
    w_hi = w_router_t.astype(BF16)
    w_router_t = jnp.concatenate([w_hi, (w_router_t - w_hi.astype(F32)).astype(BF16)], axis=0)
    return pl.pallas_call(
        functools.partial(_mix_kernel, S),
        grid=(T // tm,),
        in_specs=[pl.BlockSpec((tm, D), row),
                  pl.BlockSpec((tm, D), row),
                  pl.BlockSpec((tm, D), row),
                  pl.BlockSpec((tm, D), row),
                  pl.BlockSpec((SUBLANES, D), lambda i: (jnp.maximum(i * hb - 1, 0), 0)),
                  pl.BlockSpec((SUBLANES, D), lambda i: (jnp.minimum((i + 1) * hb, last), 0)),
                  pl.BlockSpec((1, 6, D), lambda i: (i // per_seq, 0, 0)),
                  _const_spec(pool_w_b.shape),
                  _const_spec((1, D)),
                  _const_spec((D, D)),
                  _const_spec((1, D)),
                  _const_spec((2 * E, D))],
        out_specs=[pl.BlockSpec((tm, D), row),
                   pl.BlockSpec((tm * SUBLANES, LANES), row),
                   pl.BlockSpec((E, tm), lambda i: (0, i))],
        out_shape=[jax.ShapeDtypeStruct((T, D), F32),
                   jax.ShapeDtypeStruct((T * SUBLANES, LANES), F32),
                   jax.ShapeDtypeStruct((E, T), F32)],
        compiler_params=_params("arbitrary"),
        name="mix",
    )(x2, am, gp, p_in, p_in, p_in, mod3, pool_w_b, pool_scale, w_out_b, norm2_g, w_router_t)


def _route_kernel(lg_ref, br_ref, tri_ref, eidx_ref, rank_ref, wts_ref, cnt_ref):
    E, tr = lg_ref.shape
    per = E // N_EXPERT_GROUPS

    @pl.when(pl.program_id(0) == 0)
    def _():
        cnt_ref[...] = jnp.zeros_like(cnt_ref)

    scores = jax.nn.sigmoid(lg_ref[...])
    biased = scores + br_ref[...]
    b3 = biased.reshape(N_EXPERT_GROUPS, per, tr)
    io_per = lax.broadcasted_iota(jnp.int32, b3.shape, 1)
    m1 = jnp.max(b3, axis=1, keepdims=True)
    i1 = jnp.min(jnp.where(b3 == m1, io_per, per), axis=1, keepdims=True)
    m2 = jnp.max(jnp.where(io_per == i1, NEG_INF, b3), axis=1, keepdims=True)
    gs = (m1 + m2)[:, 0, :]

    io_g = lax.broadcasted_iota(jnp.int32, gs.shape, 0)
    gsel = jnp.zeros(gs.shape, jnp.bool_)
    cur = gs
    for _ in range(TOPK_GROUPS):
        gm = jnp.max(cur, axis=0, keepdims=True)
        gi = jnp.min(jnp.where(cur == gm, io_g, N_EXPERT_GROUPS), axis=0, keepdims=True)
        pick = io_g == gi
        gsel = gsel | pick
        cur = jnp.where(pick, NEG_INF, cur)
    cur = jnp.where(gsel[:, None, :], b3, NEG_INF).reshape(E, tr)

    io_e = lax.broadcasted_iota(jnp.int32, (E, tr), 0)
    idxs, raw = [], []
    sel = jnp.zeros((E, tr), jnp.bool_)
    for _ in range(TOP_K):
        m = jnp.max(cur, axis=0, keepdims=True)
        idx = jnp.min(jnp.where(cur == m, io_e, E), axis=0, keepdims=True)
        pick = io_e == idx
        idxs.append(idx)
        raw.append(jnp.sum(jnp.where(pick, scores, 0.0), axis=0, keepdims=True))
        sel = sel | pick
        cur = jnp.where(pick, NEG_INF, cur)
    raw = jnp.concatenate(raw, axis=0)
    wts_ref[...] = raw / jnp.sum(raw, axis=0, keepdims=True) * ROUTED_SCALE
    eidx_ref[...] = jnp.concatenate(idxs, axis=0)

    self_f = sel.astype(F32)
    incl = jnp.dot(sel.astype(BF16), tri_ref[...], preferred_element_type=F32)
    before = cnt_ref[...] + incl - self_f
    ranks = [jnp.sum(jnp.where(io_e == idx, before, 0.0), axis=0, keepdims=True) for idx in idxs]
    rank_ref[...] = jnp.concatenate(ranks, axis=0).astype(jnp.int32)
    cnt_ref[...] = cnt_ref[...] + jnp.sum(self_f, axis=1, keepdims=True)


def _route(logits_t, b_router, tr):
    E, T = logits_t.shape
    tri = (jnp.arange(tr)[:, None] <= jnp.arange(tr)[None, :]).astype(BF16)
    col = lambda i: (0, i)
    return pl.pallas_call(
        _route_kernel,
        grid=(T // tr,),
        in_specs=[pl.BlockSpec((E, tr), col),
                  _const_spec((E, 1)),
                  _const_spec((tr, tr))],
        out_specs=[pl.BlockSpec((TOP_K, tr), col),
                   pl.BlockSpec((TOP_K, tr), col),
                   pl.BlockSpec((TOP_K, tr), col),
                   pl.BlockSpec((E, 1), lambda i: (0, 0))],
        out_shape=[jax.ShapeDtypeStruct((TOP_K, T), jnp.int32),
                   jax.ShapeDtypeStruct((TOP_K, T), jnp.int32),
                   jax.ShapeDtypeStruct((TOP_K, T), F32),
                   jax.ShapeDtypeStruct((E, 1), F32)],
        compiler_params=_params("arbitrary"),
        name="route",
    )(logits_t, b_router.reshape(E, 1), tri)


def _dest_kernel(eidx_ref, rank_ref, base_ref, dest_ref):
    K, tr = eidx_ref.shape
    E = base_ref.shape[0]
    io_e = lax.broadcasted_iota(jnp.int32, (E, tr), 0)
    base = base_ref[...]
    rows = [jnp.sum(jnp.where(io_e == eidx_ref[k:k + 1, :], base, 0.0), axis=0, keepdims=True)
            for k in range(K)]
    dest_ref[...] = jnp.concatenate(rows, axis=0).astype(jnp.int32) + rank_ref[...]


def _dest(eidx, rank, base, tr):
    K, T = eidx.shape
    col = lambda i: (0, i)
    return pl.pallas_call(
        _dest_kernel,
        grid=(T // tr,),
        in_specs=[pl.BlockSpec((K, tr), col), pl.BlockSpec((K, tr), col), _const_spec(base.shape)],
        out_specs=pl.BlockSpec((K, tr), col),
        out_shape=jax.ShapeDtypeStruct((K, T), jnp.int32),
        compiler_params=_params("arbitrary"),
        name="dest",
    )(eidx, rank, base)


def _dispatch_kernel(lastblk_ref, dest_ref, h_ref, wsg_ref, wsu_ref, wsd_ref, xs_ref, sh_ref, zbuf_ref, sem):
    C = SUBLANES
    td = h_ref.shape[0] // C
    RC = zbuf_ref.shape[0]

    def slab(ref, r):
        return ref.at[pl.ds(pl.multiple_of(r * C, C), C)]

    @pl.when(pl.program_id(0) == 0)
    def _():
        zbuf_ref[...] = jnp.zeros_like(zbuf_ref)

        def zcopy(e):
            start = pl.multiple_of(lastblk_ref[e] * RC, RC)
            return pltpu.make_async_copy(zbuf_ref, xs_ref.at[pl.ds(start, RC)], sem)

        def start(e, c):
            @pl.when(lastblk_ref[e] >= 0)
            def _():
                zcopy(e).start()
            return c

        def wait(e, c):
            @pl.when(lastblk_ref[e] >= 0)
            def _():
                zcopy(e).wait()
            return c

        lax.fori_loop(0, N_EXPERTS, start, 0)
        lax.fori_loop(0, N_EXPERTS, wait, 0)

    def row_copy(t, k):
        return pltpu.make_async_copy(slab(h_ref, t), slab(xs_ref, dest_ref[k, t]), sem)

    def start_rows(t, c):
        for k in range(TOP_K):
            row_copy(t, k).start(priority=k % 2)
        return c

    def wait_rows():
        for _ in range(TOP_K):
            pltpu.make_async_copy(h_ref, xs_ref.at[pl.ds(0, td * C)], sem).wait()

    lax.fori_loop(0, td, start_rows, 0, unroll=4)

    h = _rows_load(h_ref, 0, td).astype(BF16)
    g = jnp.dot(h, wsg_ref[...], preferred_element_type=F32)
    u = jnp.dot(h, wsu_ref[...], preferred_element_type=F32)
    shared = jnp.dot((g * jax.nn.sigmoid(g) * u).astype(BF16), wsd_ref[...], preferred_element_type=F32)
    sh_ref[...] = shared.astype(BF16)

    wait_rows()


def _dispatch(lastblk, dest, h2r, wsg, wsu, wsd, n_pad, td):
    C = SUBLANES
    T = h2r.shape[0] // C
    D = C * LANES
    const = lambda shape: pl.BlockSpec(shape, lambda i, lb: (0,) * len(shape), pipeline_mode=pl.Buffered(1))
    return pl.pallas_call(
        _dispatch_kernel,
        grid_spec=pltpu.PrefetchScalarGridSpec(
            num_scalar_prefetch=1,
            grid=(T // td,),
            in_specs=[pl.BlockSpec((TOP_K, td), lambda i, lb: (0, i), memory_space=pltpu.SMEM),
                      pl.BlockSpec((td * C, LANES), lambda i, lb: (i, 0)),
                      const(wsg.shape), const(wsu.shape), const(wsd.shape)],
            out_specs=[pl.BlockSpec(memory_space=pl.ANY),
                       pl.BlockSpec((td, D), lambda i, lb: (i, 0))],
            scratch_shapes=[pltpu.VMEM((MOE_BLOCK * C, LANES), F32), pltpu.SemaphoreType.DMA]),
        out_shape=[jax.ShapeDtypeStruct((n_pad * C, LANES), F32),
                   jax.ShapeDtypeStruct((T, D), BF16)],
        compiler_params=_params("arbitrary"),
        name="dispatch",
    )(lastblk, dest, h2r, wsg, wsu, wsd)


def _expert_kernel(be_ref, nused_ref, xs_ref, wg_ref, wu_ref, wd_ref, ys_ref, xbuf_ref, ybuf_ref, sin, sout):
    RC = MOE_BLOCK * SUBLANES
    j = pl.program_id(0)
    nused = nused_ref[0]

    def block(ref, b):
        start = b * RC
        if not isinstance(start, int):
            start = pl.multiple_of(start, RC)
        return ref.at[pl.ds(start, RC)]

    def copy_in(b):
        s = b % EXPERT_IN_BUFS
        return pltpu.make_async_copy(block(xs_ref, b), block(xbuf_ref, s), sin.at[s])

    def copy_out(b):
        s = b % EXPERT_OUT_BUFS
        return pltpu.make_async_copy(block(ybuf_ref, s), block(ys_ref, b), sout.at[s])

    @pl.when(j == 0)
    def _():
        for b in range(EXPERT_IN_BUFS - 1):
            @pl.when(b < nused)
            def _():
                copy_in(b).start()

    @pl.when(j < nused)
    def _():
        @pl.when(j + EXPERT_IN_BUFS - 1 < nused)
        def _():
            copy_in(j + EXPERT_IN_BUFS - 1).start()

        copy_in(j).wait()

        @pl.when(j >= EXPERT_OUT_BUFS)
        def _():
            copy_out(j - EXPERT_OUT_BUFS).wait()

        x = _rows_load(xbuf_ref, (j % EXPERT_IN_BUFS) * MOE_BLOCK, MOE_BLOCK).astype(BF16)
        g = jnp.dot(x, wg_ref[0].astype(BF16), preferred_element_type=F32)
        u = jnp.dot(x, wu_ref[0].astype(BF16), preferred_element_type=F32)
        hmid = (g * jax.nn.sigmoid(g) * u).astype(BF16)
        y = jnp.dot(hmid, wd_ref[0].astype(BF16), preferred_element_type=F32)
        _rows_store(ybuf_ref, (j % EXPERT_OUT_BUFS) * MOE_BLOCK, y)
        copy_out(j).start()

        @pl.when(j == nused - 1)
        def _():
            for back in range(EXPERT_OUT_BUFS):
                @pl.when(j - back >= 0)
                def _():
                    copy_out(j - back).wait()


def _experts(blk_expert, nused, xs, w_eg, w_eu, w_ed):
    RC = MOE_BLOCK * SUBLANES
    n_blocks = xs.shape[0] // RC
    _, D, DE = w_eg.shape
    blk = lambda j, be, nu: jnp.minimum(j, nu[0] - 1)
    wmap = lambda j, be, nu: (be[blk(j, be, nu)], 0, 0)
    return pl.pallas_call(
        _expert_kernel,
        grid_spec=pltpu.PrefetchScalarGridSpec(
            num_scalar_prefetch=2,
            grid=(n_blocks,),
            in_specs=[pl.BlockSpec(memory_space=pl.ANY),
                      pl.BlockSpec((1, D, DE), wmap),
                      pl.BlockSpec((1, D, DE), wmap),
                      pl.BlockSpec((1, DE, D), wmap)],
            out_specs=pl.BlockSpec(memory_space=pl.ANY),
            scratch_shapes=[pltpu.VMEM((EXPERT_IN_BUFS * RC, LANES), F32),
                            pltpu.VMEM((EXPERT_OUT_BUFS * RC, LANES), F32),
                            pltpu.SemaphoreType.DMA((EXPERT_IN_BUFS,)),
                            pltpu.SemaphoreType.DMA((EXPERT_OUT_BUFS,))]),
        out_shape=jax.ShapeDtypeStruct(xs.shape, F32),
        compiler_params=_params("arbitrary"),
        name="expert",
    )(blk_expert, nused, xs, w_eg, w_eu, w_ed)


def _combine_kernel(dest_ref, dnext_ref, wt_ref, ys_ref, x1_ref, sh_ref, mod_ref, o_ref, buf_ref, sems):
    tc = x1_ref.shape[0]
    C = SUBLANES
    i = pl.program_id(0)
    n = pl.num_programs(0)
    slot = i % 2
    per_slot = TOP_K * tc

    def slab(ref, r):
        return ref.at[pl.ds(pl.multiple_of(r * C, C), C)]

    def row_copy(d_ref, s, t, k):
        return pltpu.make_async_copy(slab(ys_ref, d_ref[k, t]), slab(buf_ref, s * per_slot + k * tc + t),
                                     sems.at[s])

    def start_rows(d_ref, s, t):
        for k in range(TOP_K):
            row_copy(d_ref, s, t, k).start(priority=k % 2)

    def start_tile(d_ref, s):
        def body(t, c):
            start_rows(d_ref, s, t)
            return c
        lax.fori_loop(0, tc, body, 0, unroll=4)

    @pl.when(i == 0)
    def _():
        start_tile(dest_ref, slot)

    @pl.when(i + 1 < n)
    def _():
        start_tile(dnext_ref, 1 - slot)

    slot_rows = per_slot * C
    pltpu.make_async_copy(ys_ref.at[pl.ds(0, slot_rows)],
                          buf_ref.at[pl.ds(pl.multiple_of(slot * slot_rows, slot_rows), slot_rows)],
                          sems.at[slot]).wait()

    g2 = mod_ref[0, 5:6, :]
    o_ref[...] = x1_ref[...] + g2 * sh_ref[...].astype(F32)
    for k in range(TOP_K):
        w = wt_ref[:, k:k + 1] * g2
        o_ref[...] = o_ref[...] + _rows_load(buf_ref, slot * per_slot + k * tc, tc) * w


def _combine(dest, wts_t, ys, x1, sh, mod3, S, tc):
    T, D = x1.shape
    per_seq = S // tc
    last = T // tc - 1
    row = lambda i: (i, 0)
    return pl.pallas_call(
        _combine_kernel,
        grid=(T // tc,),
        in_specs=[pl.BlockSpec((TOP_K, tc), lambda i: (0, i), memory_space=pltpu.SMEM),
                  pl.BlockSpec((TOP_K, tc), lambda i: (0, jnp.minimum(i + 1, last)), memory_space=pltpu.SMEM),
                  pl.BlockSpec((tc, TOP_K), row),
                  pl.BlockSpec(memory_space=pl.ANY),
                  pl.BlockSpec((tc, D), row),
                  pl.BlockSpec((tc, D), row),
                  pl.BlockSpec((1, 6, D), lambda i: (i // per_seq, 0, 0))],
        out_specs=pl.BlockSpec((tc, D), row),
        out_shape=jax.ShapeDtypeStruct((T, D), F32),
        scratch_shapes=[pltpu.VMEM((2 * TOP_K * tc * SUBLANES, LANES), F32), pltpu.SemaphoreType.DMA((2,))],
        compiler_params=_params("arbitrary"),
        name="combine",
    )(dest, dest, wts_t, ys, x1, sh, mod3)


def _tile(n, want):
    t = min(n, want)
    while n % t:
        t //= 2
    return t


def kernel(x, c, rel_bias_table, w_ada, b_ada, norm1_g, w_in, q_norm_g, k_norm_g, lambda_q1, lambda_k1,
           lambda_q2, lambda_k2, subln_g, pool_w, pool_scale, w_out, norm2_g, w_router, b_router,
           w_exp_gate, w_exp_up, w_exp_down, w_sh_gate, w_sh_up, w_sh_down):
    B, S, D = x.shape
    T = B * S
    depth = w_ada.shape[0]
    assert depth == 1, "LAM_INIT and the single-layer pipeline assume depth 1"
    assert D == N_HEADS * HEAD_DV and S % LANES == 0
    l = 0
    tm = _tile(S, 512)
    tq = _tile(S, 1024)

    x2 = x.reshape(T, D)
    mod3 = _ada(c, w_ada[l], b_ada[l]).reshape(B, 6, D)

    qg = (jnp.tile(q_norm_g[l], D // HEAD_DK) * (HEAD_DK ** -0.5 * LOG2E)).reshape(1, D)
    kg = jnp.tile(k_norm_g[l], D // HEAD_DK).reshape(1, D)
    q, k, v, p_in, ga, gp = _inproj(x2, mod3, norm1_g[l].reshape(1, D), w_in[l].astype(BF16),
                                    qg, kg, S, tm)

    strip = _bias_strip(rel_bias_table, S) * LOG2E
    nq = S // tq
    n = S + tq
    win = jnp.stack([strip[:, S - (i + 1) * tq: S - (i + 1) * tq + n] for i in range(nq)], axis=1)
    win = win.reshape(N_HEADS * nq, 1, n)
    lam_p = jnp.stack([lambda_q1[l], lambda_k1[l], lambda_q2[l], lambda_k2[l]])
    am = _attention(q, k, v, ga, win, lam_p, subln_g[l].reshape(1, HEAD_DV), B, S, tq)

    x1, h2r, logits_t = _mix(x2, am, gp, p_in, mod3, pool_w[l].astype(BF16),
                                  pool_scale[l].reshape(1, D), w_out[l].astype(BF16),
                                  norm2_g[l].reshape(1, D), w_router[l].T, S, tm)

    eidx, rank, wts, counts = _route(logits_t, b_router[l], _tile(T, 512))

    cnt = counts[:, 0].astype(jnp.int32)
    nblk = (cnt + MOE_BLOCK - 1) // MOE_BLOCK
    blk_end = jnp.cumsum(nblk)
    base = ((blk_end - nblk) * MOE_BLOCK).astype(F32).reshape(N_EXPERTS, 1)
    lastblk = jnp.where(nblk > 0, blk_end - 1, -1).astype(jnp.int32)
    n_blocks = -(-(T * TOP_K) // MOE_BLOCK) + N_EXPERTS
    blk_expert = jnp.minimum(jnp.sum(blk_end[None, :] <= jnp.arange(n_blocks)[:, None], axis=1),
                             N_EXPERTS - 1).astype(jnp.int32)
    nused = blk_end[-1:].astype(jnp.int32)

    dest = _dest(eidx, rank, base, _tile(T, 512))
    xs, sh = _dispatch(lastblk, dest, h2r, w_sh_gate[l].astype(BF16), w_sh_up[l].astype(BF16),
                       w_sh_down[l].astype(BF16), n_blocks * MOE_BLOCK, _tile(S, 256))
    ys = _experts(blk_expert, nused, xs, w_exp_gate[l], w_exp_up[l], w_exp_down[l])
    out = _combine(dest, wts.T, ys, x1, sh, mod3, S, _tile(S, 128))
    return out.reshape(B, S, D)
```

```python
import functools
import math

import jax
import jax.numpy as jnp
from jax import lax
from jax.experimental import pallas as pl
from jax.experimental.pallas import tpu as pltpu

N_HEADS = 8
HEAD_DK = 64
HEAD_DV = 2 * HEAD_DK
N_BUCKETS = 32
MAX_DISTANCE = 128
POOL_WINDOWS = (2, 4, 8, 16)
N_POOL_GROUPS = 4
N_EXPERTS = 256
TOP_K = 8
N_EXPERT_GROUPS = 8
TOPK_GROUPS = 4
ROUTED_SCALE = 2.5
MOE_BLOCK = 512
EXPERT_IN_BUFS = 4
EXPERT_OUT_BUFS = 3
EPS = 1e-6
LAM_INIT = 0.8 - 0.6 * math.exp(-0.3 * 0)
LOG2E = math.log2(math.e)
KEY_TILE = 512

LANES = 128
SUBLANES = 8
VMEM_LIMIT = 56 * 1024 * 1024

F32 = jnp.float32
BF16 = jnp.bfloat16
HIGHEST = lax.Precision.HIGHEST
NEG_INF = float("-inf")


def _params(*sem):
    return pltpu.CompilerParams(dimension_semantics=sem, vmem_limit_bytes=VMEM_LIMIT)


def _const_spec(shape):
    nd = len(shape)
    return pl.BlockSpec(shape, lambda *_: (0,) * nd, pipeline_mode=pl.Buffered(1))


def _rows_load(ref, first, n, chunks=SUBLANES):
    return jnp.concatenate([ref[pl.ds(first * chunks + c, n, stride=chunks), :] for c in range(chunks)],
                           axis=-1)


def _rows_store(ref, first, val, chunks=SUBLANES):
    n = val.shape[0]
    for c in range(chunks):
        ref[pl.ds(first * chunks + c, n, stride=chunks), :] = val[:, c * LANES:(c + 1) * LANES]


def _ada_kernel(c_ref, w_ref, b_ref, o_ref):
    c = c_ref[...]
    s = c * jax.nn.sigmoid(c)
    o_ref[...] = jnp.dot(s, w_ref[...], precision=HIGHEST, preferred_element_type=F32) + b_ref[...]


def _ada(c, w_ada, b_ada):
    B, D = c.shape
    n = w_ada.shape[1] // D
    return pl.pallas_call(
        _ada_kernel,
        grid=(n,),
        in_specs=[pl.BlockSpec((B, D), lambda j: (0, 0)),
                  pl.BlockSpec((D, D), lambda j: (0, j)),
                  pl.BlockSpec((1, D), lambda j: (0, j))],
        out_specs=pl.BlockSpec((B, D), lambda j: (0, j)),
        out_shape=jax.ShapeDtypeStruct((B, n * D), F32),
        compiler_params=_params("arbitrary"),
        name="ada",
    )(c, w_ada, b_ada.reshape(1, -1))


def _bias_kernel(bk_ref, tab_ref, o_ref):
    bk = bk_ref[...]
    onehot = (lax.broadcasted_iota(jnp.int32, (N_BUCKETS, bk.shape[1]), 0) == bk).astype(F32)
    o_ref[...] = lax.dot_general(tab_ref[...], onehot, (((0,), (0,)), ((), ())),
                                 precision=HIGHEST, preferred_element_type=F32)


def _t5_buckets(rel):
    nb = N_BUCKETS // 2
    max_exact = nb // 2
    n = jnp.abs(rel)
    large = max_exact + (jnp.log(jnp.maximum(n, 1).astype(jnp.float32) / max_exact)
                         / math.log(MAX_DISTANCE / max_exact) * (nb - max_exact)).astype(jnp.int32)
    large = jnp.minimum(large, nb - 1)
    return jnp.where(rel > 0, nb, 0) + jnp.where(n < max_exact, n, large)


def _bias_strip(rel_table, S):
    L = 2 * S
    buckets = _t5_buckets(jnp.arange(L, dtype=jnp.int32) - S).reshape(1, L)
    return pl.pallas_call(
        _bias_kernel,
        out_shape=jax.ShapeDtypeStruct((N_HEADS, L), F32),
        name="bias",
    )(buckets, rel_table)


def _inproj_kernel(x_ref, mod_ref, g1_ref, w_ref, gsum_ref, gspread_ref, qg_ref, kg_ref,
                   q_ref, kt_ref, v_ref, p_ref, ga_ref, gp_ref):
    D = x_ref.shape[1]
    x = x_ref[...]
    h = x * lax.rsqrt(jnp.mean(x * x, axis=-1, keepdims=True) + EPS) * g1_ref[...]
    h = h * (1.0 + mod_ref[0, 1:2, :]) + mod_ref[0, 0:1, :]
    hb = h.astype(BF16)

    def chunk(c):
        return jnp.dot(hb, w_ref[:, c * D:(c + 1) * D], preferred_element_type=F32)

    def head_norm(y, g_ref):
        ss = jnp.dot((y * y).astype(BF16), gsum_ref[...], preferred_element_type=F32)
        r = lax.rsqrt(ss * (1.0 / HEAD_DK) + EPS)
        r_hi = r.astype(BF16)
        r_lo = (r - r_hi.astype(F32)).astype(BF16)
        scale = jnp.dot(jnp.concatenate([r_hi, r_lo], axis=-1), gspread_ref[...],
                        preferred_element_type=F32)
        return y * scale * g_ref[...]

    q_ref[...] = head_norm(chunk(0), qg_ref).astype(BF16)
    kn = head_norm(chunk(1), kg_ref)
    for hd in range(N_HEADS):
        for t in range(x_ref.shape[0] // KEY_TILE):
            blk = kn[t * KEY_TILE:(t + 1) * KEY_TILE, hd * HEAD_DV:(hd + 1) * HEAD_DV]
            kt_ref[0, hd, t] = blk.T.astype(BF16)
    v_ref[...] = chunk(2).astype(BF16)
    p_ref[...] = chunk(3)
    ga_ref[...] = jax.nn.sigmoid(chunk(4)).astype(BF16)
    gp_ref[...] = jax.nn.sigmoid(chunk(5)).astype(BF16)


def _inproj(x2, mod3, norm1_g, w_in_b, qg, kg, S, tm):
    T, D = x2.shape
    grp = jnp.arange(D) // HEAD_DK
    gsum = (grp[:, None] == jnp.arange(LANES)[None, :]).astype(BF16)
    gspread = jnp.concatenate([gsum.T, gsum.T], axis=0)
    per_seq = S // tm
    row = lambda i: (i, 0)
    tok = jax.ShapeDtypeStruct((T, D), BF16)
    kt = jax.ShapeDtypeStruct((T // S, N_HEADS, S // KEY_TILE, HEAD_DV, KEY_TILE), BF16)
    outs = [tok, kt, tok, jax.ShapeDtypeStruct((T, D), F32), tok, tok]
    tok_spec = pl.BlockSpec((tm, D), row)
    kt_spec = pl.BlockSpec((1, N_HEADS, tm // KEY_TILE, HEAD_DV, KEY_TILE),
                           lambda i: (i // per_seq, 0, i % per_seq, 0, 0))
    return pl.pallas_call(
        _inproj_kernel,
        grid=(T // tm,),
        in_specs=[pl.BlockSpec((tm, D), row),
                  pl.BlockSpec((1, 6, D), lambda i: (i // per_seq, 0, 0)),
                  _const_spec((1, D)),
                  _const_spec(w_in_b.shape),
                  _const_spec(gsum.shape),
                  _const_spec(gspread.shape),
                  _const_spec((1, D)),
                  _const_spec((1, D))],
        out_specs=[tok_spec, kt_spec, tok_spec, tok_spec, tok_spec, tok_spec],
        out_shape=outs,
        compiler_params=_params("arbitrary"),
        name="inproj",
    )(x2, mod3, norm1_g, w_in_b, gsum, gspread, qg, kg)


def _attn_kernel(q_ref, kt_ref, v_ref, ga_ref, win_ref, lam_ref, sg_ref, o_ref,
                 bias_ref, x_ref, m_ref, acc_ref):
    tq = q_ref.shape[0]
    n_kt, _, kt = kt_ref.shape[2:]
    n = win_ref.shape[2]

    @pl.when(pl.program_id(2) == 0)
    def _():
        xb = jnp.broadcast_to(win_ref[0, 0], (tq, n))
        rolled = pltpu.roll(xb, n - tq, 1, stride=1, stride_axis=0)
        for j in range(n_kt):
            bias_ref[j] = rolled[:, j * kt:(j + 1) * kt]

    lp = lam_ref[...]
    lam = (jnp.exp(jnp.sum(lp[0:1] * lp[1:2], axis=-1, keepdims=True))
           - jnp.exp(jnp.sum(lp[2:3] * lp[3:4], axis=-1, keepdims=True)) + LAM_INIT)

    q = q_ref[...]
    first = lax.broadcasted_iota(jnp.int32, (1, HEAD_DV), 1) < HEAD_DK
    zero = jnp.zeros_like(q)
    qq = jnp.concatenate([jnp.where(first, q, zero), jnp.where(first, zero, q)], axis=0)

    for j in range(n_kt):
        b = bias_ref[j]
        x = jnp.dot(qq, kt_ref[0, 0, j], preferred_element_type=F32) + jnp.concatenate([b, b], axis=0)
        x_ref[j] = x
        t = x[:, 0:LANES]
        for c in range(1, kt // LANES):
            t = jnp.maximum(t, x[:, c * LANES:(c + 1) * LANES])
        m_ref[...] = t if j == 0 else jnp.maximum(m_ref[...], t)
    m = jnp.max(m_ref[...], axis=-1, keepdims=True)

    m_ref[...] = jnp.broadcast_to(m, m_ref.shape)

    def pv_tile(j, carry):
        mb = m_ref[...]
        mb = jnp.concatenate([mb] * (kt // LANES), axis=-1)
        ones_col = (lax.broadcasted_iota(jnp.int32, (kt, LANES), 1) == 0).astype(BF16)
        p = jnp.exp2((x_ref[j] - mb).astype(BF16))
        rows = pl.ds(pl.multiple_of(j * kt, kt), kt)
        v_ext = jnp.concatenate([v_ref[rows, :], ones_col], axis=-1)
        pv = jnp.concatenate([jnp.dot(p[:tq], v_ext, preferred_element_type=F32),
                              jnp.dot(p[tq:], v_ext, preferred_element_type=F32)], axis=0)
        acc_ref[...] = acc_ref[...] + pv
        return carry

    acc_ref[...] = jnp.zeros_like(acc_ref)
    lax.fori_loop(0, n_kt, pv_tile, 0, unroll=2)
    acc = acc_ref[...]
    o1, l1 = acc[:tq, :HEAD_DV], acc[:tq, HEAD_DV:HEAD_DV + 1]
    o2, l2 = acc[tq:, :HEAD_DV], acc[tq:, HEAD_DV:HEAD_DV + 1]
    o = o1 / l1 - lam * (o2 / l2)
    o = o * lax.rsqrt(jnp.mean(o * o, axis=-1, keepdims=True) + EPS) * sg_ref[...] * (1.0 - LAM_INIT)
    o_ref[...] = (o * ga_ref[...].astype(F32)).astype(BF16)


def _attention(q, kt, v, ga, win, lam_p, subln_g, B, S, tq):
    T, D = q.shape
    nq = S // tq
    n = win.shape[2]
    n_kt = S // KEY_TILE
    qmap = lambda h, i, b: (b * nq + i, h)
    return pl.pallas_call(
        _attn_kernel,
        grid=(N_HEADS, nq, B),
        in_specs=[pl.BlockSpec((tq, HEAD_DV), qmap),
                  pl.BlockSpec((1, 1, n_kt, HEAD_DV, KEY_TILE), lambda h, i, b: (b, h, 0, 0, 0)),
                  pl.BlockSpec((S, HEAD_DV), lambda h, i, b: (b, h)),
                  pl.BlockSpec((tq, HEAD_DV), qmap),
                  pl.BlockSpec((1, 1, n), lambda h, i, b: (h * nq + i, 0, 0)),
                  pl.BlockSpec((4, HEAD_DK), lambda h, i, b: (0, 0)),
                  pl.BlockSpec((1, HEAD_DV), lambda h, i, b: (0, 0))],
        out_specs=pl.BlockSpec((tq, HEAD_DV), qmap),
        out_shape=jax.ShapeDtypeStruct((T, D), BF16),
        scratch_shapes=[pltpu.VMEM((n_kt, tq, KEY_TILE), F32),
                        pltpu.VMEM((n_kt, 2 * tq, KEY_TILE), F32),
                        pltpu.VMEM((2 * tq, LANES), F32),
                        pltpu.VMEM((2 * tq, HEAD_DV + LANES), F32)],
        compiler_params=_params("arbitrary", "arbitrary", "arbitrary"),
        name="attn",
    )(q, kt, v, ga, win, lam_p, subln_g)


def _mix_kernel(S, x_ref, am_ref, gp_ref, p_ref, pprev_ref, pnext_ref, mod_ref, pw_ref, ps_ref,
                wo_ref, g2_ref, wr_ref, x1_ref, h2r_ref, lg_ref):
    tm, D = x_ref.shape
    gc = D // N_POOL_GROUPS
    halo = SUBLANES
    ne = tm + 2 * halo
    pos0 = (pl.program_id(0) % (S // tm)) * tm
    pos_e = pos0 - halo + lax.broadcasted_iota(jnp.int32, (ne, 1), 0)
    valid = (pos_e >= 0) & (pos_e < S)
    ext = jnp.concatenate([pprev_ref[...], p_ref[...], pnext_ref[...]], axis=0)
    ext = jnp.where(valid, ext, 0.0)
    pos = pos0 + lax.broadcasted_iota(jnp.int32, (tm, 1), 0)

    merged = []
    for g, w in enumerate(POOL_WINDOWS):
        half = w // 2
        e = ext[:, g * gc:(g + 1) * gc]
        sw = e
        width = 1
        while width < w:
            sw = sw + pltpu.roll(sw, width, 0)
            width *= 2
        win = pltpu.roll(sw, ne - (half - 1), 0)[halo:halo + tm] if half > 1 else sw[halo:halo + tm]
        lo = jnp.clip(pos - half, 0, S - 1)
        hi = jnp.clip(pos + half - 1, 0, S - 1)
        cnt = (hi - lo + 1).astype(F32)
        mixed = win / cnt - p_ref[:, g * gc:(g + 1) * gc]
        pooled = jnp.dot(mixed.astype(BF16), pw_ref[g], preferred_element_type=F32)
        pooled = pooled * ps_ref[:, g * gc:(g + 1) * gc]
        merged.append(am_ref[:, g * gc:(g + 1) * gc].astype(F32)
                      + gp_ref[:, g * gc:(g + 1) * gc].astype(F32) * pooled)
    merged = jnp.concatenate(merged, axis=-1).astype(BF16)
    y = jnp.dot(merged, wo_ref[...], preferred_element_type=F32)
    x1 = x_ref[...] + mod_ref[0, 2:3, :] * y
    x1_ref[...] = x1
    h2 = x1 * lax.rsqrt(jnp.mean(x1 * x1, axis=-1, keepdims=True) + EPS) * g2_ref[...]
    h2 = h2 * (1.0 + mod_ref[0, 4:5, :]) + mod_ref[0, 3:4, :]
    _rows_store(h2r_ref, 0, h2)
    E = lg_ref.shape[0]
    nt = (((1,), (1,)), ((), ()))
    h_hi = h2.astype(BF16)
    h_lo = (h2 - h_hi.astype(F32)).astype(BF16)
    both = lax.dot_general(wr_ref[...], h_hi, nt, preferred_element_type=F32)
    lg_ref[...] = both[:E] + both[E:] + lax.dot_general(wr_ref[:E, :], h_lo, nt, preferred_element_type=F32)


def _mix(x2, am, gp, p_in, mod3, pool_w_b, pool_scale, w_out_b, norm2_g, w_router_t, S, tm):
    T, D = x2.shape
    per_seq = S // tm
    hb = tm // SUBLANES
    last = T // SUBLANES - 1
    row = lambda i: (i, 0)
    E = w_router_t.shape[0]
    w_hi = w_router_t.astype(BF16)
    w_router_t = jnp.concatenate([w_hi, (w_router_t - w_hi.astype(F32)).astype(BF16)], axis=0)
    return pl.pallas_call(
        functools.partial(_mix_kernel, S),
        grid=(T // tm,),
        in_specs=[pl.BlockSpec((tm, D), row),
                  pl.BlockSpec((tm, D), row),
                  pl.BlockSpec((tm, D), row),
                  pl.BlockSpec((tm, D), row),
                  pl.BlockSpec((SUBLANES, D), lambda i: (jnp.maximum(i * hb - 1, 0), 0)),
                  pl.BlockSpec((SUBLANES, D), lambda i: (jnp.minimum((i + 1) * hb, last), 0)),
                  pl.BlockSpec((1, 6, D), lambda i: (i // per_seq, 0, 0)),
                  _const_spec(pool_w_b.shape),
                  _const_spec((1, D)),
                  _const_spec((D, D)),
                  _const_spec((1, D)),
                  _const_spec((2 * E, D))],
        out_specs=[pl.BlockSpec((tm, D), row),
                   pl.BlockSpec((tm * SUBLANES, LANES), row),
                   pl.BlockSpec((E, tm), lambda i: (0, i))],
        out_shape=[jax.ShapeDtypeStruct((T, D), F32),
                   jax.ShapeDtypeStruct((T * SUBLANES, LANES), F32),
                   jax.ShapeDtypeStruct((E, T), F32)],
        compiler_params=_params("arbitrary"),
        name="mix",
    )(x2, am, gp, p_in, p_in, p_in, mod3, pool_w_b, pool_scale, w_out_b, norm2_g, w_router_t)


def _route_kernel(lg_ref, br_ref, tri_ref, eidx_ref, rank_ref, wts_ref, cnt_ref):
    E, tr = lg_ref.shape
    per = E // N_EXPERT_GROUPS

    @pl.when(pl.program_id(0) == 0)
    def _():
        cnt_ref[...] = jnp.zeros_like(cnt_ref)

    scores = jax.nn.sigmoid(lg_ref[...])
    biased = scores + br_ref[...]
    b3 = biased.reshape(N_EXPERT_GROUPS, per, tr)
    io_per = lax.broadcasted_iota(jnp.int32, b3.shape, 1)
    m1 = jnp.max(b3, axis=1, keepdims=True)
    i1 = jnp.min(jnp.where(b3 == m1, io_per, per), axis=1, keepdims=True)
    m2 = jnp.max(jnp.where(io_per == i1, NEG_INF, b3), axis=1, keepdims=True)
    gs = (m1 + m2)[:, 0, :]

    io_g = lax.broadcasted_iota(jnp.int32, gs.shape, 0)
    gsel = jnp.zeros(gs.shape, jnp.bool_)
    cur = gs
    for _ in range(TOPK_GROUPS):
        gm = jnp.max(cur, axis=0, keepdims=True)
        gi = jnp.min(jnp.where(cur == gm, io_g, N_EXPERT_GROUPS), axis=0, keepdims=True)
        pick = io_g == gi
        gsel = gsel | pick
        cur = jnp.where(pick, NEG_INF, cur)
    cur = jnp.where(gsel[:, None, :], b3, NEG_INF).reshape(E, tr)

    io_e = lax.broadcasted_iota(jnp.int32, (E, tr), 0)
    idxs, raw = [], []
    sel = jnp.zeros((E, tr), jnp.bool_)
    for _ in range(TOP_K):
        m = jnp.max(cur, axis=0, keepdims=True)
        idx = jnp.min(jnp.where(cur == m, io_e, E), axis=0, keepdims=True)
        pick = io_e == idx
        idxs.append(idx)
        raw.append(jnp.sum(jnp.where(pick, scores, 0.0), axis=0, keepdims=True))
        sel = sel | pick
        cur = jnp.where(pick, NEG_INF, cur)
    raw = jnp.concatenate(raw, axis=0)
    wts_ref[...] = raw / jnp.sum(raw, axis=0, keepdims=True) * ROUTED_SCALE
    eidx_ref[...] = jnp.concatenate(idxs, axis=0)

    self_f = sel.astype(F32)
    incl = jnp.dot(sel.astype(BF16), tri_ref[...], preferred_element_type=F32)
    before = cnt_ref[...] + incl - self_f
    ranks = [jnp.sum(jnp.where(io_e == idx, before, 0.0), axis=0, keepdims=True) for idx in idxs]
    rank_ref[...] = jnp.concatenate(ranks, axis=0).astype(jnp.int32)
    cnt_ref[...] = cnt_ref[...] + jnp.sum(self_f, axis=1, keepdims=True)


def _route(logits_t, b_router, tr):
    E, T = logits_t.shape
    tri = (jnp.arange(tr)[:, None] <= jnp.arange(tr)[None, :]).astype(BF16)
    col = lambda i: (0, i)
    return pl.pallas_call(
        _route_kernel,
        grid=(T // tr,),
        in_specs=[pl.BlockSpec((E, tr), col),
                  _const_spec((E, 1)),
                  _const_spec((tr, tr))],
        out_specs=[pl.BlockSpec((TOP_K, tr), col),
                   pl.BlockSpec((TOP_K, tr), col),
                   pl.BlockSpec((TOP_K, tr), col),
                   pl.BlockSpec((E, 1), lambda i: (0, 0))],
        out_shape=[jax.ShapeDtypeStruct((TOP_K, T), jnp.int32),
                   jax.ShapeDtypeStruct((TOP_K, T), jnp.int32),
                   jax.ShapeDtypeStruct((TOP_K, T), F32),
                   jax.ShapeDtypeStruct((E, 1), F32)],
        compiler_params=_params("arbitrary"),
        name="route",
    )(logits_t, b_router.reshape(E, 1), tri)


def _dest_kernel(eidx_ref, rank_ref, base_ref, dest_ref):
    K, tr = eidx_ref.shape
    E = base_ref.shape[0]
    io_e = lax.broadcasted_iota(jnp.int32, (E, tr), 0)
    base = base_ref[...]
    rows = [jnp.sum(jnp.where(io_e == eidx_ref[k:k + 1, :], base, 0.0), axis=0, keepdims=True)
            for k in range(K)]
    dest_ref[...] = jnp.concatenate(rows, axis=0).astype(jnp.int32) + rank_ref[...]


def _dest(eidx, rank, base, tr):
    K, T = eidx.shape
    col = lambda i: (0, i)
    return pl.pallas_call(
        _dest_kernel,
        grid=(T // tr,),
        in_specs=[pl.BlockSpec((K, tr), col), pl.BlockSpec((K, tr), col), _const_spec(base.shape)],
        out_specs=pl.BlockSpec((K, tr), col),
        out_shape=jax.ShapeDtypeStruct((K, T), jnp.int32),
        compiler_params=_params("arbitrary"),
        name="dest",
    )(eidx, rank, base)


def _dispatch_kernel(lastblk_ref, dest_ref, h_ref, hsrc_ref, wsg_ref, wsu_ref, wsd_ref, xs_ref, sh_ref,
                     zbuf_ref, sem):
    C = SUBLANES
    td = h_ref.shape[0] // C
    tile_first = pl.program_id(0) * td
    RC = zbuf_ref.shape[0]

    def slab(ref, r):
        return ref.at[pl.ds(pl.multiple_of(r * C, C), C)]

    @pl.when(pl.program_id(0) == 0)
    def _():
        zbuf_ref[...] = jnp.zeros_like(zbuf_ref)

        def zcopy(e):
            start = pl.multiple_of(lastblk_ref[e] * RC, RC)
            return pltpu.make_async_copy(zbuf_ref, xs_ref.at[pl.ds(start, RC)], sem)

        def start(e, c):
            @pl.when(lastblk_ref[e] >= 0)
            def _():
                zcopy(e).start()
            return c

        def wait(e, c):
            @pl.when(lastblk_ref[e] >= 0)
            def _():
                zcopy(e).wait()
            return c

        lax.fori_loop(0, N_EXPERTS, start, 0)
        lax.fori_loop(0, N_EXPERTS, wait, 0)

    def row_copy(t, k):
        return pltpu.make_async_copy(slab(hsrc_ref, tile_first + t), slab(xs_ref, dest_ref[k, t]), sem)

    def start_rows(t, c):
        for k in range(TOP_K):
            row_copy(t, k).start(priority=k % 2)
        return c

    def wait_rows():
        for _ in range(TOP_K):
            pltpu.make_async_copy(h_ref, xs_ref.at[pl.ds(0, td * C)], sem).wait()

    lax.fori_loop(0, td, start_rows, 0, unroll=4)

    h = _rows_load(h_ref, 0, td).astype(BF16)
    g = jnp.dot(h, wsg_ref[...], preferred_element_type=F32)
    u = jnp.dot(h, wsu_ref[...], preferred_element_type=F32)
    shared = jnp.dot((g * jax.nn.sigmoid(g) * u).astype(BF16), wsd_ref[...], preferred_element_type=F32)
    sh_ref[...] = shared.astype(BF16)

    wait_rows()


def _dispatch(lastblk, dest, h2r, wsg, wsu, wsd, n_pad, td):
    C = SUBLANES
    T = h2r.shape[0] // C
    D = C * LANES
    const = lambda shape: pl.BlockSpec(shape, lambda i, lb: (0,) * len(shape), pipeline_mode=pl.Buffered(1))
    return pl.pallas_call(
        _dispatch_kernel,
        grid_spec=pltpu.PrefetchScalarGridSpec(
            num_scalar_prefetch=1,
            grid=(T // td,),
            in_specs=[pl.BlockSpec((TOP_K, td), lambda i, lb: (0, i), memory_space=pltpu.SMEM),
                      pl.BlockSpec((td * C, LANES), lambda i, lb: (i, 0)),
                      pl.BlockSpec(memory_space=pl.ANY),
                      const(wsg.shape), const(wsu.shape), const(wsd.shape)],
            out_specs=[pl.BlockSpec(memory_space=pl.ANY),
                       pl.BlockSpec((td, D), lambda i, lb: (i, 0))],
            scratch_shapes=[pltpu.VMEM((MOE_BLOCK * C, LANES), F32), pltpu.SemaphoreType.DMA]),
        out_shape=[jax.ShapeDtypeStruct((n_pad * C, LANES), F32),
                   jax.ShapeDtypeStruct((T, D), BF16)],
        compiler_params=_params("arbitrary"),
        name="dispatch",
    )(lastblk, dest, h2r, h2r, wsg, wsu, wsd)


def _expert_kernel(be_ref, nused_ref, xs_ref, wg_ref, wu_ref, wd_ref, ys_ref, xbuf_ref, ybuf_ref, sin, sout):
    RC = MOE_BLOCK * SUBLANES
    j = pl.program_id(0)
    nused = nused_ref[0]

    def block(ref, b):
        start = b * RC
        if not isinstance(start, int):
            start = pl.multiple_of(start, RC)
        return ref.at[pl.ds(start, RC)]

    def copy_in(b):
        s = b % EXPERT_IN_BUFS
        return pltpu.make_async_copy(block(xs_ref, b), block(xbuf_ref, s), sin.at[s])

    def copy_out(b):
        s = b % EXPERT_OUT_BUFS
        return pltpu.make_async_copy(block(ybuf_ref, s), block(ys_ref, b), sout.at[s])

    @pl.when(j == 0)
    def _():
        for b in range(EXPERT_IN_BUFS - 1):
            @pl.when(b < nused)
            def _():
                copy_in(b).start()

    @pl.when(j < nused)
    def _():
        @pl.when(j + EXPERT_IN_BUFS - 1 < nused)
        def _():
            copy_in(j + EXPERT_IN_BUFS - 1).start()

        copy_in(j).wait()

        @pl.when(j >= EXPERT_OUT_BUFS)
        def _():
            copy_out(j - EXPERT_OUT_BUFS).wait()

        x = _rows_load(xbuf_ref, (j % EXPERT_IN_BUFS) * MOE_BLOCK, MOE_BLOCK).astype(BF16)
        g = jnp.dot(x, wg_ref[0].astype(BF16), preferred_element_type=F32)
        u = jnp.dot(x, wu_ref[0].astype(BF16), preferred_element_type=F32)
        hmid = (g * jax.nn.sigmoid(g) * u).astype(BF16)
        y = jnp.dot(hmid, wd_ref[0].astype(BF16), preferred_element_type=F32)
        _rows_store(ybuf_ref, (j % EXPERT_OUT_BUFS) * MOE_BLOCK, y)
        copy_out(j).start()

        @pl.when(j == nused - 1)
        def _():
            for back in range(EXPERT_OUT_BUFS):
                @pl.when(j - back >= 0)
                def _():
                    copy_out(j - back).wait()


def _experts(blk_expert, nused, xs, w_eg, w_eu, w_ed):
    RC = MOE_BLOCK * SUBLANES
    n_blocks = xs.shape[0] // RC
    _, D, DE = w_eg.shape
    blk = lambda j, be, nu: jnp.minimum(j, nu[0] - 1)
    wmap = lambda j, be, nu: (be[blk(j, be, nu)], 0, 0)
    return pl.pallas_call(
        _expert_kernel,
        grid_spec=pltpu.PrefetchScalarGridSpec(
            num_scalar_prefetch=2,
            grid=(n_blocks,),
            in_specs=[pl.BlockSpec(memory_space=pl.ANY),
                      pl.BlockSpec((1, D, DE), wmap),
                      pl.BlockSpec((1, D, DE), wmap),
                      pl.BlockSpec((1, DE, D), wmap)],
            out_specs=pl.BlockSpec(memory_space=pl.ANY),
            scratch_shapes=[pltpu.VMEM((EXPERT_IN_BUFS * RC, LANES), F32),
                            pltpu.VMEM((EXPERT_OUT_BUFS * RC, LANES), F32),
                            pltpu.SemaphoreType.DMA((EXPERT_IN_BUFS,)),
                            pltpu.SemaphoreType.DMA((EXPERT_OUT_BUFS,))]),
        out_shape=jax.ShapeDtypeStruct(xs.shape, F32),
        compiler_params=_params("arbitrary"),
        name="expert",
    )(blk_expert, nused, xs, w_eg, w_eu, w_ed)


def _combine_kernel(dest_ref, dnext_ref, wt_ref, ys_ref, x1_ref, sh_ref, mod_ref, o_ref, buf_ref, sems):
    tc = x1_ref.shape[0]
    C = SUBLANES
    i = pl.program_id(0)
    n = pl.num_programs(0)
    slot = i % 2
    per_slot = TOP_K * tc

    def slab(ref, r):
        return ref.at[pl.ds(pl.multiple_of(r * C, C), C)]

    def row_copy(d_ref, s, t, k):
        return pltpu.make_async_copy(slab(ys_ref, d_ref[k, t]), slab(buf_ref, s * per_slot + k * tc + t),
                                     sems.at[s])

    def start_rows(d_ref, s, t):
        for k in range(TOP_K):
            row_copy(d_ref, s, t, k).start(priority=k % 2)

    def start_tile(d_ref, s):
        def body(t, c):
            start_rows(d_ref, s, t)
            return c
        lax.fori_loop(0, tc, body, 0, unroll=4)

    @pl.when(i == 0)
    def _():
        start_tile(dest_ref, slot)

    @pl.when(i + 1 < n)
    def _():
        start_tile(dnext_ref, 1 - slot)

    slot_rows = per_slot * C
    pltpu.make_async_copy(ys_ref.at[pl.ds(0, slot_rows)],
                          buf_ref.at[pl.ds(pl.multiple_of(slot * slot_rows, slot_rows), slot_rows)],
                          sems.at[slot]).wait()

    g2 = mod_ref[0, 5:6, :]
    o_ref[...] = x1_ref[...] + g2 * sh_ref[...].astype(F32)
    for k in range(TOP_K):
        w = wt_ref[:, k:k + 1] * g2
        o_ref[...] = o_ref[...] + _rows_load(buf_ref, slot * per_slot + k * tc, tc) * w


def _combine(dest, wts_t, ys, x1, sh, mod3, S, tc):
    T, D = x1.shape
    per_seq = S // tc
    last = T // tc - 1
    row = lambda i: (i, 0)
    return pl.pallas_call(
        _combine_kernel,
        grid=(T // tc,),
        in_specs=[pl.BlockSpec((TOP_K, tc), lambda i: (0, i), memory_space=pltpu.SMEM),
                  pl.BlockSpec((TOP_K, tc), lambda i: (0, jnp.minimum(i + 1, last)), memory_space=pltpu.SMEM),
                  pl.BlockSpec((tc, TOP_K), row),
                  pl.BlockSpec(memory_space=pl.ANY),
                  pl.BlockSpec((tc, D), row),
                  pl.BlockSpec((tc, D), row),
                  pl.BlockSpec((1, 6, D), lambda i: (i // per_seq, 0, 0))],
        out_specs=pl.BlockSpec((tc, D), row),
        out_shape=jax.ShapeDtypeStruct((T, D), F32),
        scratch_shapes=[pltpu.VMEM((2 * TOP_K * tc * SUBLANES, LANES), F32), pltpu.SemaphoreType.DMA((2,))],
        compiler_params=_params("arbitrary"),
        name="combine",
    )(dest, dest, wts_t, ys, x1, sh, mod3)


def _tile(n, want):
    t = min(n, want)
    while n % t:
        t //= 2
    return t


def kernel(x, c, rel_bias_table, w_ada, b_ada, norm1_g, w_in, q_norm_g, k_norm_g, lambda_q1, lambda_k1,
           lambda_q2, lambda_k2, subln_g, pool_w, pool_scale, w_out, norm2_g, w_router, b_router,
           w_exp_gate, w_exp_up, w_exp_down, w_sh_gate, w_sh_up, w_sh_down):
    B, S, D = x.shape
    T = B * S
    depth = w_ada.shape[0]
    assert depth == 1, "LAM_INIT and the single-layer pipeline assume depth 1"
    assert D == N_HEADS * HEAD_DV and S % LANES == 0
    l = 0
    tm = _tile(S, 512)
    tq = _tile(S, 1024)

    x2 = x.reshape(T, D)
    mod3 = _ada(c, w_ada[l], b_ada[l]).reshape(B, 6, D)

    qg = (jnp.tile(q_norm_g[l], D // HEAD_DK) * (HEAD_DK ** -0.5 * LOG2E)).reshape(1, D)
    kg = jnp.tile(k_norm_g[l], D // HEAD_DK).reshape(1, D)
    q, k, v, p_in, ga, gp = _inproj(x2, mod3, norm1_g[l].reshape(1, D), w_in[l].astype(BF16),
                                    qg, kg, S, tm)

    strip = _bias_strip(rel_bias_table, S) * LOG2E
    nq = S // tq
    n = S + tq
    win = jnp.stack([strip[:, S - (i + 1) * tq: S - (i + 1) * tq + n] for i in range(nq)], axis=1)
    win = win.reshape(N_HEADS * nq, 1, n)
    lam_p = jnp.stack([lambda_q1[l], lambda_k1[l], lambda_q2[l], lambda_k2[l]])
    am = _attention(q, k, v, ga, win, lam_p, subln_g[l].reshape(1, HEAD_DV), B, S, tq)

    x1, h2r, logits_t = _mix(x2, am, gp, p_in, mod3, pool_w[l].astype(BF16),
                                  pool_scale[l].reshape(1, D), w_out[l].astype(BF16),
                                  norm2_g[l].reshape(1, D), w_router[l].T, S, tm)

    eidx, rank, wts, counts = _route(logits_t, b_router[l], _tile(T, 512))

    cnt = counts[:, 0].astype(jnp.int32)
    nblk = (cnt + MOE_BLOCK - 1) // MOE_BLOCK
    blk_end = jnp.cumsum(nblk)
    base = ((blk_end - nblk) * MOE_BLOCK).astype(F32).reshape(N_EXPERTS, 1)
    lastblk = jnp.where(nblk > 0, blk_end - 1, -1).astype(jnp.int32)
    n_blocks = -(-(T * TOP_K) // MOE_BLOCK) + N_EXPERTS
    blk_expert = jnp.minimum(jnp.sum(blk_end[None, :] <= jnp.arange(n_blocks)[:, None], axis=1),
                             N_EXPERTS - 1).astype(jnp.int32)
    nused = blk_end[-1:].astype(jnp.int32)

    dest = _dest(eidx, rank, base, _tile(T, 512))
    xs, sh = _dispatch(lastblk, dest, h2r, w_sh_gate[l].astype(BF16), w_sh_up[l].astype(BF16),
                       w_sh_down[l].astype(BF16), n_blocks * MOE_BLOCK, _tile(S, 256))
    ys = _experts(blk_expert, nused, xs, w_exp_gate[l], w_exp_up[l], w_exp_down[l])
    out = _combine(dest, wts.T, ys, x1, sh, mod3, S, _tile(S, 128))
    return out.reshape(B, S, D)
```

```python
import functools
import math

import jax
import jax.numpy as jnp
from jax import lax
from jax.experimental import pallas as pl
from jax.experimental.pallas import tpu as pltpu

N_HEADS = 8
HEAD_DK = 64
HEAD_DV = 2 * HEAD_DK
N_BUCKETS = 32
MAX_DISTANCE = 128
POOL_WINDOWS = (2, 4, 8, 16)
N_POOL_GROUPS = 4
N_EXPERTS = 256
TOP_K = 8
N_EXPERT_GROUPS = 8
TOPK_GROUPS = 4
ROUTED_SCALE = 2.5
MOE_BLOCK = 512
EXPERT_IN_BUFS = 3
EXPERT_OUT_BUFS = 2
EPS = 1e-6
LAM_INIT = 0.8 - 0.6 * math.exp(-0.3 * 0)
LOG2E = math.log2(math.e)
KEY_TILE = 512

LANES = 128
SUBLANES = 8
VMEM_LIMIT = 56 * 1024 * 1024

F32 = jnp.float32
BF16 = jnp.bfloat16
HIGHEST = lax.Precision.HIGHEST
NEG_INF = float("-inf")


def _params(*sem):
    return pltpu.CompilerParams(dimension_semantics=sem, vmem_limit_bytes=VMEM_LIMIT)


def _const_spec(shape):
    nd = len(shape)
    return pl.BlockSpec(shape, lambda *_: (0,) * nd, pipeline_mode=pl.Buffered(1))


def _rows_load(ref, first, n, chunks=SUBLANES):
    return jnp.concatenate([ref[pl.ds(first * chunks + c, n, stride=chunks), :] for c in range(chunks)],
                           axis=-1)


def _rows_store(ref, first, val, chunks=SUBLANES):
    n = val.shape[0]
    for c in range(chunks):
        ref[pl.ds(first * chunks + c, n, stride=chunks), :] = val[:, c * LANES:(c + 1) * LANES]


def _ada_kernel(c_ref, w_ref, b_ref, o_ref):
    c = c_ref[...]
    s = c * jax.nn.sigmoid(c)
    o_ref[...] = jnp.dot(s, w_ref[...], precision=HIGHEST, preferred_element_type=F32) + b_ref[...]


def _ada(c, w_ada, b_ada):
    B, D = c.shape
    n = w_ada.shape[1] // D
    return pl.pallas_call(
        _ada_kernel,
        grid=(n,),
        in_specs=[pl.BlockSpec((B, D), lambda j: (0, 0)),
                  pl.BlockSpec((D, D), lambda j: (0, j)),
                  pl.BlockSpec((1, D), lambda j: (0, j))],
        out_specs=pl.BlockSpec((B, D), lambda j: (0, j)),
        out_shape=jax.ShapeDtypeStruct((B, n * D), F32),
        compiler_params=_params("arbitrary"),
        name="ada",
    )(c, w_ada, b_ada.reshape(1, -1))


def _bias_kernel(bk_ref, tab_ref, o_ref):
    bk = bk_ref[...]
    onehot = (lax.broadcasted_iota(jnp.int32, (N_BUCKETS, bk.shape[1]), 0) == bk).astype(F32)
    o_ref[...] = lax.dot_general(tab_ref[...], onehot, (((0,), (0,)), ((), ())),
                                 precision=HIGHEST, preferred_element_type=F32)


def _t5_buckets(rel):
    nb = N_BUCKETS // 2
    max_exact = nb // 2
    n = jnp.abs(rel)
    large = max_exact + (jnp.log(jnp.maximum(n, 1).astype(jnp.float32) / max_exact)
                         / math.log(MAX_DISTANCE / max_exact) * (nb - max_exact)).astype(jnp.int32)
    large = jnp.minimum(large, nb - 1)
    return jnp.where(rel > 0, nb, 0) + jnp.where(n < max_exact, n, large)


def _bias_strip(rel_table, S):
    L = 2 * S
    buckets = _t5_buckets(jnp.arange(L, dtype=jnp.int32) - S).reshape(1, L)
    return pl.pallas_call(
        _bias_kernel,
        out_shape=jax.ShapeDtypeStruct((N_HEADS, L), F32),
        name="bias",
    )(buckets, rel_table)


def _inproj_kernel(x_ref, mod_ref, g1_ref, w_ref, gsum_ref, gspread_ref, qg_ref, kg_ref,
                   q_ref, kt_ref, v_ref, p_ref, ga_ref, gp_ref):
    D = x_ref.shape[1]
    x = x_ref[...]
    h = x * lax.rsqrt(jnp.mean(x * x, axis=-1, keepdims=True) + EPS) * g1_ref[...]
    h = h * (1.0 + mod_ref[0, 1:2, :]) + mod_ref[0, 0:1, :]
    hb = h.astype(BF16)

    def chunk(c):
        return jnp.dot(hb, w_ref[:, c * D:(c + 1) * D], preferred_element_type=F32)

    def head_norm(y, g_ref):
        ss = jnp.dot((y * y).astype(BF16), gsum_ref[...], preferred_element_type=F32)
        r = lax.rsqrt(ss * (1.0 / HEAD_DK) + EPS)
        r_hi = r.astype(BF16)
        r_lo = (r - r_hi.astype(F32)).astype(BF16)
        scale = jnp.dot(jnp.concatenate([r_hi, r_lo], axis=-1), gspread_ref[...],
                        preferred_element_type=F32)
        return y * scale * g_ref[...]

    q_ref[...] = head_norm(chunk(0), qg_ref).astype(BF16)
    kn = head_norm(chunk(1), kg_ref)
    for hd in range(N_HEADS):
        for t in range(x_ref.shape[0] // KEY_TILE):
            blk = kn[t * KEY_TILE:(t + 1) * KEY_TILE, hd * HEAD_DV:(hd + 1) * HEAD_DV]
            kt_ref[0, hd, t] = blk.T.astype(BF16)
    v_ref[...] = chunk(2).astype(BF16)
    p_ref[...] = chunk(3)
    ga_ref[...] = jax.nn.sigmoid(chunk(4)).astype(BF16)
    gp_ref[...] = jax.nn.sigmoid(chunk(5)).astype(BF16)


def _inproj(x2, mod3, norm1_g, w_in_b, qg, kg, S, tm):
    T, D = x2.shape
    grp = jnp.arange(D) // HEAD_DK
    gsum = (grp[:, None] == jnp.arange(LANES)[None, :]).astype(BF16)
    gspread = jnp.concatenate([gsum.T, gsum.T], axis=0)
    per_seq = S // tm
    row = lambda i: (i, 0)
    tok = jax.ShapeDtypeStruct((T, D), BF16)
    kt = jax.ShapeDtypeStruct((T // S, N_HEADS, S // KEY_TILE, HEAD_DV, KEY_TILE), BF16)
    outs = [tok, kt, tok, jax.ShapeDtypeStruct((T, D), F32), tok, tok]
    tok_spec = pl.BlockSpec((tm, D), row)
    kt_spec = pl.BlockSpec((1, N_HEADS, tm // KEY_TILE, HEAD_DV, KEY_TILE),
                           lambda i: (i // per_seq, 0, i % per_seq, 0, 0))
    return pl.pallas_call(
        _inproj_kernel,
        grid=(T // tm,),
        in_specs=[pl.BlockSpec((tm, D), row),
                  pl.BlockSpec((1, 6, D), lambda i: (i // per_seq, 0, 0)),
                  _const_spec((1, D)),
                  _const_spec(w_in_b.shape),
                  _const_spec(gsum.shape),
                  _const_spec(gspread.shape),
                  _const_spec((1, D)),
                  _const_spec((1, D))],
        out_specs=[tok_spec, kt_spec, tok_spec, tok_spec, tok_spec, tok_spec],
        out_shape=outs,
        compiler_params=_params("arbitrary"),
        name="inproj",
    )(x2, mod3, norm1_g, w_in_b, gsum, gspread, qg, kg)


def _attn_kernel(q_ref, kt_ref, v_ref, ga_ref, win_ref, lam_ref, sg_ref, o_ref,
                 bias_ref, x_ref, m_ref, acc_ref):
    tq = q_ref.shape[0]
    n_kt, _, kt = kt_ref.shape[2:]
    n = win_ref.shape[2]

    @pl.when(pl.program_id(2) == 0)
    def _():
        xb = jnp.broadcast_to(win_ref[0, 0], (tq, n))
        rolled = pltpu.roll(xb, n - tq, 1, stride=1, stride_axis=0)
        for j in range(n_kt):
            bias_ref[j] = rolled[:, j * kt:(j + 1) * kt]

    lp = lam_ref[...]
    lam = (jnp.exp(jnp.sum(lp[0:1] * lp[1:2], axis=-1, keepdims=True))
           - jnp.exp(jnp.sum(lp[2:3] * lp[3:4], axis=-1, keepdims=True)) + LAM_INIT)

    q = q_ref[...]
    first = lax.broadcasted_iota(jnp.int32, (1, HEAD_DV), 1) < HEAD_DK
    zero = jnp.zeros_like(q)
    qq = jnp.concatenate([jnp.where(first, q, zero), jnp.where(first, zero, q)], axis=0)

    for j in range(n_kt):
        b = bias_ref[j]
        x = jnp.dot(qq, kt_ref[0, 0, j], preferred_element_type=F32) + jnp.concatenate([b, b], axis=0)
        x_ref[j] = x
        t = x[:, 0:LANES]
        for c in range(1, kt // LANES):
            t = jnp.maximum(t, x[:, c * LANES:(c + 1) * LANES])
        m_ref[...] = t if j == 0 else jnp.maximum(m_ref[...], t)
    m = jnp.max(m_ref[...], axis=-1, keepdims=True)

    m_ref[...] = jnp.broadcast_to(m, m_ref.shape)

    def pv_tile(j, carry):
        mb = m_ref[...]
        mb = jnp.concatenate([mb] * (kt // LANES), axis=-1)
        ones_col = (lax.broadcasted_iota(jnp.int32, (kt, LANES), 1) == 0).astype(BF16)
        p = jnp.exp2((x_ref[j] - mb).astype(BF16))
        rows = pl.ds(pl.multiple_of(j * kt, kt), kt)
        v_ext = jnp.concatenate([v_ref[rows, :], ones_col], axis=-1)
        pv = jnp.concatenate([jnp.dot(p[:tq], v_ext, preferred_element_type=F32),
                              jnp.dot(p[tq:], v_ext, preferred_element_type=F32)], axis=0)
        acc_ref[...] = acc_ref[...] + pv
        return carry

    acc_ref[...] = jnp.zeros_like(acc_ref)
    lax.fori_loop(0, n_kt, pv_tile, 0, unroll=2)
    acc = acc_ref[...]
    o1, l1 = acc[:tq, :HEAD_DV], acc[:tq, HEAD_DV:HEAD_DV + 1]
    o2, l2 = acc[tq:, :HEAD_DV], acc[tq:, HEAD_DV:HEAD_DV + 1]
    o = o1 / l1 - lam * (o2 / l2)
    o = o * lax.rsqrt(jnp.mean(o * o, axis=-1, keepdims=True) + EPS) * sg_ref[...] * (1.0 - LAM_INIT)
    o_ref[...] = (o * ga_ref[...].astype(F32)).astype(BF16)


def _attention(q, kt, v, ga, win, lam_p, subln_g, B, S, tq):
    T, D = q.shape
    nq = S // tq
    n = win.shape[2]
    n_kt = S // KEY_TILE
    qmap = lambda h, i, b: (b * nq + i, h)
    return pl.pallas_call(
        _attn_kernel,
        grid=(N_HEADS, nq, B),
        in_specs=[pl.BlockSpec((tq, HEAD_DV), qmap),
                  pl.BlockSpec((1, 1, n_kt, HEAD_DV, KEY_TILE), lambda h, i, b: (b, h, 0, 0, 0)),
                  pl.BlockSpec((S, HEAD_DV), lambda h, i, b: (b, h)),
                  pl.BlockSpec((tq, HEAD_DV), qmap),
                  pl.BlockSpec((1, 1, n), lambda h, i, b: (h * nq + i, 0, 0)),
                  pl.BlockSpec((4, HEAD_DK), lambda h, i, b: (0, 0)),
                  pl.BlockSpec((1, HEAD_DV), lambda h, i, b: (0, 0))],
        out_specs=pl.BlockSpec((tq, HEAD_DV), qmap),
        out_shape=jax.ShapeDtypeStruct((T, D), BF16),
        scratch_shapes=[pltpu.VMEM((n_kt, tq, KEY_TILE), F32),
                        pltpu.VMEM((n_kt, 2 * tq, KEY_TILE), F32),
                        pltpu.VMEM((2 * tq, LANES), F32),
                        pltpu.VMEM((2 * tq, HEAD_DV + LANES), F32)],
        compiler_params=_params("arbitrary", "arbitrary", "arbitrary"),
        name="attn",
    )(q, kt, v, ga, win, lam_p, subln_g)


def _mix_kernel(S, x_ref, am_ref, gp_ref, p_ref, pprev_ref, pnext_ref, mod_ref, pw_ref, ps_ref,
                wo_ref, g2_ref, wr_ref, x1_ref, h2r_ref, lg_ref):
    tm, D = x_ref.shape
    gc = D // N_POOL_GROUPS
    halo = SUBLANES
    ne = tm + 2 * halo
    pos0 = (pl.program_id(0) % (S // tm)) * tm
    pos_e = pos0 - halo + lax.broadcasted_iota(jnp.int32, (ne, 1), 0)
    valid = (pos_e >= 0) & (pos_e < S)
    ext = jnp.concatenate([pprev_ref[...], p_ref[...], pnext_ref[...]], axis=0)
    ext = jnp.where(valid, ext, 0.0)
    pos = pos0 + lax.broadcasted_iota(jnp.int32, (tm, 1), 0)

    merged = []
    for g, w in enumerate(POOL_WINDOWS):
        half = w // 2
        e = ext[:, g * gc:(g + 1) * gc]
        sw = e
        width = 1
        while width < w:
            sw = sw + pltpu.roll(sw, width, 0)
            width *= 2
        win = pltpu.roll(sw, ne - (half - 1), 0)[halo:halo + tm] if half > 1 else sw[halo:halo + tm]
        lo = jnp.clip(pos - half, 0, S - 1)
        hi = jnp.clip(pos + half - 1, 0, S - 1)
        cnt = (hi - lo + 1).astype(F32)
        mixed = win / cnt - p_ref[:, g * gc:(g + 1) * gc]
        pooled = jnp.dot(mixed.astype(BF16), pw_ref[g], preferred_element_type=F32)
        pooled = pooled * ps_ref[:, g * gc:(g + 1) * gc]
        merged.append(am_ref[:, g * gc:(g + 1) * gc].astype(F32)
                      + gp_ref[:, g * gc:(g + 1) * gc].astype(F32) * pooled)
    merged = jnp.concatenate(merged, axis=-1).astype(BF16)
    y = jnp.dot(merged, wo_ref[...], preferred_element_type=F32)
    x1 = x_ref[...] + mod_ref[0, 2:3, :] * y
    x1_ref[...] = x1
    h2 = x1 * lax.rsqrt(jnp.mean(x1 * x1, axis=-1, keepdims=True) + EPS) * g2_ref[...]
    h2 = h2 * (1.0 + mod_ref[0, 4:5, :]) + mod_ref[0, 3:4, :]
    _rows_store(h2r_ref, 0, h2)
    E = lg_ref.shape[0]
    nt = (((1,), (1,)), ((), ()))
    h_hi = h2.astype(BF16)
    h_lo = (h2 - h_hi.astype(F32)).astype(BF16)
    both = lax.dot_general(wr_ref[...], h_hi, nt, preferred_element_type=F32)
    lg_ref[...] = both[:E] + both[E:] + lax.dot_general(wr_ref[:E, :], h_lo, nt, preferred_element_type=F32)


def _mix(x2, am, gp, p_in, mod3, pool_w_b, pool_scale, w_out_b, norm2_g, w_router_t, S, tm):
    T, D = x2.shape
    per_seq = S // tm
    hb = tm // SUBLANES
    last = T // SUBLANES - 1
    row = lambda i: (i, 0)
    E = w_router_t.shape[0]
    w_hi = w_router_t.astype(BF16)
    w_router_t = jnp.concatenate([w_hi, (w_router_t - w_hi.astype(F32)).astype(BF16)], axis=0)
    return pl.pallas_call(
        functools.partial(_mix_kernel, S),
        grid=(T // tm,),
        in_specs=[pl.BlockSpec((tm, D), row),
                  pl.BlockSpec((tm, D), row),
                  pl.BlockSpec((tm, D), row),
                  pl.BlockSpec((tm, D), row),
                  pl.BlockSpec((SUBLANES, D), lambda i: (jnp.maximum(i * hb - 1, 0), 0)),
                  pl.BlockSpec((SUBLANES, D), lambda i: (jnp.minimum((i + 1) * hb, last), 0)),
                  pl.BlockSpec((1, 6, D), lambda i: (i // per_seq, 0, 0)),
                  _const_spec(pool_w_b.shape),
                  _const_spec((1, D)),
                  _const_spec((D, D)),
                  _const_spec((1, D)),
                  _const_spec((2 * E, D))],
        out_specs=[pl.BlockSpec((tm, D), row),
                   pl.BlockSpec((tm * SUBLANES, LANES), row),
                   pl.BlockSpec((E, tm), lambda i: (0, i))],
        out_shape=[jax.ShapeDtypeStruct((T, D), F32),
                   jax.ShapeDtypeStruct((T * SUBLANES, LANES), F32),
                   jax.ShapeDtypeStruct((E, T), F32)],
        compiler_params=_params("arbitrary"),
        name="mix",
    )(x2, am, gp, p_in, p_in, p_in, mod3, pool_w_b, pool_scale, w_out_b, norm2_g, w_router_t)


def _route_kernel(lg_ref, br_ref, tri_ref, eidx_ref, rank_ref, wts_ref, cnt_ref):
    E, tr = lg_ref.shape
    per = E // N_EXPERT_GROUPS

    @pl.when(pl.program_id(0) == 0)
    def _():
        cnt_ref[...] = jnp.zeros_like(cnt_ref)

    scores = jax.nn.sigmoid(lg_ref[...])
    biased = scores + br_ref[...]
    b3 = biased.reshape(N_EXPERT_GROUPS, per, tr)
    io_per = lax.broadcasted_iota(jnp.int32, b3.shape, 1)
    m1 = jnp.max(b3, axis=1, keepdims=True)
    i1 = jnp.min(jnp.where(b3 == m1, io_per, per), axis=1, keepdims=True)
    m2 = jnp.max(jnp.where(io_per == i1, NEG_INF, b3), axis=1, keepdims=True)
    gs = (m1 + m2)[:, 0, :]

    io_g = lax.broadcasted_iota(jnp.int32, gs.shape, 0)
    gsel = jnp.zeros(gs.shape, jnp.bool_)
    cur = gs
    for _ in range(TOPK_GROUPS):
        gm = jnp.max(cur, axis=0, keepdims=True)
        gi = jnp.min(jnp.where(cur == gm, io_g, N_EXPERT_GROUPS), axis=0, keepdims=True)
        pick = io_g == gi
        gsel = gsel | pick
        cur = jnp.where(pick, NEG_INF, cur)
    cur = jnp.where(gsel[:, None, :], b3, NEG_INF).reshape(E, tr)

    io_e = lax.broadcasted_iota(jnp.int32, (E, tr), 0)
    idxs, raw = [], []
    sel = jnp.zeros((E, tr), jnp.bool_)
    for _ in range(TOP_K):
        m = jnp.max(cur, axis=0, keepdims=True)
        idx = jnp.min(jnp.where(cur == m, io_e, E), axis=0, keepdims=True)
        pick = io_e == idx
        idxs.append(idx)
        raw.append(jnp.sum(jnp.where(pick, scores, 0.0), axis=0, keepdims=True))
        sel = sel | pick
        cur = jnp.where(pick, NEG_INF, cur)
    raw = jnp.concatenate(raw, axis=0)
    wts_ref[...] = raw / jnp.sum(raw, axis=0, keepdims=True) * ROUTED_SCALE
    eidx_ref[...] = jnp.concatenate(idxs, axis=0)

    self_f = sel.astype(F32)
    incl = jnp.dot(sel.astype(BF16), tri_ref[...], preferred_element_type=F32)
    before = cnt_ref[...] + incl - self_f
    ranks = [jnp.sum(jnp.where(io_e == idx, before, 0.0), axis=0, keepdims=True) for idx in idxs]
    rank_ref[...] = jnp.concatenate(ranks, axis=0).astype(jnp.int32)
    cnt_ref[...] = cnt_ref[...] + jnp.sum(self_f, axis=1, keepdims=True)


def _route(logits_t, b_router, tr):
    E, T = logits_t.shape
    tri = (jnp.arange(tr)[:, None] <= jnp.arange(tr)[None, :]).astype(BF16)
    col = lambda i: (0, i)
    return pl.pallas_call(
        _route_kernel,
        grid=(T // tr,),
        in_specs=[pl.BlockSpec((E, tr), col),
                  _const_spec((E, 1)),
                  _const_spec((tr, tr))],
        out_specs=[pl.BlockSpec((TOP_K, tr), col),
                   pl.BlockSpec((TOP_K, tr), col),
                   pl.BlockSpec((TOP_K, tr), col),
                   pl.BlockSpec((E, 1), lambda i: (0, 0))],
        out_shape=[jax.ShapeDtypeStruct((TOP_K, T), jnp.int32),
                   jax.ShapeDtypeStruct((TOP_K, T), jnp.int32),
                   jax.ShapeDtypeStruct((TOP_K, T), F32),
                   jax.ShapeDtypeStruct((E, 1), F32)],
        compiler_params=_params("arbitrary"),
        name="route",
    )(logits_t, b_router.reshape(E, 1), tri)


def _dest_kernel(eidx_ref, rank_ref, base_ref, dest_ref):
    K, tr = eidx_ref.shape
    E = base_ref.shape[0]
    io_e = lax.broadcasted_iota(jnp.int32, (E, tr), 0)
    base = base_ref[...]
    rows = [jnp.sum(jnp.where(io_e == eidx_ref[k:k + 1, :], base, 0.0), axis=0, keepdims=True)
            for k in range(K)]
    dest_ref[...] = jnp.concatenate(rows, axis=0).astype(jnp.int32) + rank_ref[...]


def _dest(eidx, rank, base, tr):
    K, T = eidx.shape
    col = lambda i: (0, i)
    return pl.pallas_call(
        _dest_kernel,
        grid=(T // tr,),
        in_specs=[pl.BlockSpec((K, tr), col), pl.BlockSpec((K, tr), col), _const_spec(base.shape)],
        out_specs=pl.BlockSpec((K, tr), col),
        out_shape=jax.ShapeDtypeStruct((K, T), jnp.int32),
        compiler_params=_params("arbitrary"),
        name="dest",
    )(eidx, rank, base)


def _dispatch_kernel(lastblk_ref, dest_ref, h_ref, wsg_ref, wsu_ref, wsd_ref, xs_ref, sh_ref, zbuf_ref, sem):
    C = SUBLANES
    td = h_ref.shape[0] // C
    RC = zbuf_ref.shape[0]

    def slab(ref, r):
        return ref.at[pl.ds(pl.multiple_of(r * C, C), C)]

    @pl.when(pl.program_id(0) == 0)
    def _():
        zbuf_ref[...] = jnp.zeros_like(zbuf_ref)

        def zcopy(e):
            start = pl.multiple_of(lastblk_ref[e] * RC, RC)
            return pltpu.make_async_copy(zbuf_ref, xs_ref.at[pl.ds(start, RC)], sem)

        def start(e, c):
            @pl.when(lastblk_ref[e] >= 0)
            def _():
                zcopy(e).start()
            return c

        def wait(e, c):
            @pl.when(lastblk_ref[e] >= 0)
            def _():
                zcopy(e).wait()
            return c

        lax.fori_loop(0, N_EXPERTS, start, 0)
        lax.fori_loop(0, N_EXPERTS, wait, 0)

    def row_copy(t, k):
        return pltpu.make_async_copy(slab(h_ref, t), slab(xs_ref, dest_ref[k, t]), sem)

    def start_rows(t, c):
        for k in range(TOP_K):
            row_copy(t, k).start(priority=k % 2)
        return c

    def wait_rows():
        for _ in range(TOP_K):
            pltpu.make_async_copy(h_ref, xs_ref.at[pl.ds(0, td * C)], sem).wait()

    lax.fori_loop(0, td, start_rows, 0, unroll=4)

    h = _rows_load(h_ref, 0, td).astype(BF16)
    g = jnp.dot(h, wsg_ref[...], preferred_element_type=F32)
    u = jnp.dot(h, wsu_ref[...], preferred_element_type=F32)
    shared = jnp.dot((g * jax.nn.sigmoid(g) * u).astype(BF16), wsd_ref[...], preferred_element_type=F32)
    sh_ref[...] = shared.astype(BF16)

    wait_rows()


def _dispatch(lastblk, dest, h2r, wsg, wsu, wsd, n_pad, td):
    C = SUBLANES
    T = h2r.shape[0] // C
    D = C * LANES
    const = lambda shape: pl.BlockSpec(shape, lambda i, lb: (0,) * len(shape), pipeline_mode=pl.Buffered(1))
    return pl.pallas_call(
        _dispatch_kernel,
        grid_spec=pltpu.PrefetchScalarGridSpec(
            num_scalar_prefetch=1,
            grid=(T // td,),
            in_specs=[pl.BlockSpec((TOP_K, td), lambda i, lb: (0, i), memory_space=pltpu.SMEM),
                      pl.BlockSpec((td * C, LANES), lambda i, lb: (i, 0)),
                      const(wsg.shape), const(wsu.shape), const(wsd.shape)],
            out_specs=[pl.BlockSpec(memory_space=pl.ANY),
                       pl.BlockSpec((td, D), lambda i, lb: (i, 0))],
            scratch_shapes=[pltpu.VMEM((MOE_BLOCK * C, LANES), F32), pltpu.SemaphoreType.DMA]),
        out_shape=[jax.ShapeDtypeStruct((n_pad * C, LANES), F32),
                   jax.ShapeDtypeStruct((T, D), BF16)],
        compiler_params=_params("arbitrary"),
        name="dispatch",
    )(lastblk, dest, h2r, wsg, wsu, wsd)


def _expert_kernel(first_ref, nblk_ref, nused_ref, xs_ref, wg_ref, wu_ref, wd_ref, ys_ref,
                   xbuf_ref, ybuf_ref, sin, sout):
    RC = MOE_BLOCK * SUBLANES
    e = pl.program_id(0)
    nused = nused_ref[0]

    def block(ref, b):
        start = b * RC
        if not isinstance(start, int):
            start = pl.multiple_of(start, RC)
        return ref.at[pl.ds(start, RC)]

    def copy_in(b):
        s = b % EXPERT_IN_BUFS
        return pltpu.make_async_copy(block(xs_ref, b), block(xbuf_ref, s), sin.at[s])

    def copy_out(b):
        s = b % EXPERT_OUT_BUFS
        return pltpu.make_async_copy(block(ybuf_ref, s), block(ys_ref, b), sout.at[s])

    @pl.when(e == 0)
    def _():
        for b in range(EXPERT_IN_BUFS - 1):
            @pl.when(b < nused)
            def _():
                copy_in(b).start()

    def one_block(i, carry):
        j = first_ref[e] + i

        @pl.when(j + EXPERT_IN_BUFS - 1 < nused)
        def _():
            copy_in(j + EXPERT_IN_BUFS - 1).start()

        copy_in(j).wait()

        @pl.when(j >= EXPERT_OUT_BUFS)
        def _():
            copy_out(j - EXPERT_OUT_BUFS).wait()

        x = _rows_load(xbuf_ref, (j % EXPERT_IN_BUFS) * MOE_BLOCK, MOE_BLOCK).astype(BF16)
        g = jnp.dot(x, wg_ref[0].astype(BF16), preferred_element_type=F32)
        u = jnp.dot(x, wu_ref[0].astype(BF16), preferred_element_type=F32)
        hmid = (g * jax.nn.sigmoid(g) * u).astype(BF16)
        y = jnp.dot(hmid, wd_ref[0].astype(BF16), preferred_element_type=F32)
        _rows_store(ybuf_ref, (j % EXPERT_OUT_BUFS) * MOE_BLOCK, y)
        copy_out(j).start()

        @pl.when(j == nused - 1)
        def _():
            for back in range(EXPERT_OUT_BUFS):
                @pl.when(j - back >= 0)
                def _():
                    copy_out(j - back).wait()
        return carry

    lax.fori_loop(0, nblk_ref[e], one_block, 0)


def _experts(blk_first, nblk, nused, xs, w_eg, w_eu, w_ed):
    RC = MOE_BLOCK * SUBLANES
    E, D, DE = w_eg.shape
    wmap = lambda e, bf, nb, nu: (e, 0, 0)
    return pl.pallas_call(
        _expert_kernel,
        grid_spec=pltpu.PrefetchScalarGridSpec(
            num_scalar_prefetch=3,
            grid=(E,),
            in_specs=[pl.BlockSpec(memory_space=pl.ANY),
                      pl.BlockSpec((1, D, DE), wmap),
                      pl.BlockSpec((1, D, DE), wmap),
                      pl.BlockSpec((1, DE, D), wmap)],
            out_specs=pl.BlockSpec(memory_space=pl.ANY),
            scratch_shapes=[pltpu.VMEM((EXPERT_IN_BUFS * RC, LANES), F32),
                            pltpu.VMEM((EXPERT_OUT_BUFS * RC, LANES), F32),
                            pltpu.SemaphoreType.DMA((EXPERT_IN_BUFS,)),
                            pltpu.SemaphoreType.DMA((EXPERT_OUT_BUFS,))]),
        out_shape=jax.ShapeDtypeStruct(xs.shape, F32),
        compiler_params=_params("arbitrary"),
        name="expert",
    )(blk_first, nblk, nused, xs, w_eg, w_eu, w_ed)


def _combine_kernel(dest_ref, dnext_ref, wt_ref, ys_ref, x1_ref, sh_ref, mod_ref, o_ref, buf_ref, sems):
    tc = x1_ref.shape[0]
    C = SUBLANES
    i = pl.program_id(0)
    n = pl.num_programs(0)
    slot = i % 2
    per_slot = TOP_K * tc

    def slab(ref, r):
        return ref.at[pl.ds(pl.multiple_of(r * C, C), C)]

    def row_copy(d_ref, s, t, k):
        return pltpu.make_async_copy(slab(ys_ref, d_ref[k, t]), slab(buf_ref, s * per_slot + k * tc + t),
                                     sems.at[s])

    def start_rows(d_ref, s, t):
        for k in range(TOP_K):
            row_copy(d_ref, s, t, k).start(priority=k % 2)

    def start_tile(d_ref, s):
        def body(t, c):
            start_rows(d_ref, s, t)
            return c
        lax.fori_loop(0, tc, body, 0, unroll=4)

    @pl.when(i == 0)
    def _():
        start_tile(dest_ref, slot)

    @pl.when(i + 1 < n)
    def _():
        start_tile(dnext_ref, 1 - slot)

    slot_rows = per_slot * C
    pltpu.make_async_copy(ys_ref.at[pl.ds(0, slot_rows)],
                          buf_ref.at[pl.ds(pl.multiple_of(slot * slot_rows, slot_rows), slot_rows)],
                          sems.at[slot]).wait()

    g2 = mod_ref[0, 5:6, :]
    o_ref[...] = x1_ref[...] + g2 * sh_ref[...].astype(F32)
    for k in range(TOP_K):
        w = wt_ref[:, k:k + 1] * g2
        o_ref[...] = o_ref[...] + _rows_load(buf_ref, slot * per_slot + k * tc, tc) * w


def _combine(dest, wts_t, ys, x1, sh, mod3, S, tc):
    T, D = x1.shape
    per_seq = S // tc
    last = T // tc - 1
    row = lambda i: (i, 0)
    return pl.pallas_call(
        _combine_kernel,
        grid=(T // tc,),
        in_specs=[pl.BlockSpec((TOP_K, tc), lambda i: (0, i), memory_space=pltpu.SMEM),
                  pl.BlockSpec((TOP_K, tc), lambda i: (0, jnp.minimum(i + 1, last)), memory_space=pltpu.SMEM),
                  pl.BlockSpec((tc, TOP_K), row),
                  pl.BlockSpec(memory_space=pl.ANY),
                  pl.BlockSpec((tc, D), row),
                  pl.BlockSpec((tc, D), row),
                  pl.BlockSpec((1, 6, D), lambda i: (i // per_seq, 0, 0))],
        out_specs=pl.BlockSpec((tc, D), row),
        out_shape=jax.ShapeDtypeStruct((T, D), F32),
        scratch_shapes=[pltpu.VMEM((2 * TOP_K * tc * SUBLANES, LANES), F32), pltpu.SemaphoreType.DMA((2,))],
        compiler_params=_params("arbitrary"),
        name="combine",
    )(dest, dest, wts_t, ys, x1, sh, mod3)


def _tile(n, want):
    t = min(n, want)
    while n % t:
        t //= 2
    return t


def kernel(x, c, rel_bias_table, w_ada, b_ada, norm1_g, w_in, q_norm_g, k_norm_g, lambda_q1, lambda_k1,
           lambda_q2, lambda_k2, subln_g, pool_w, pool_scale, w_out, norm2_g, w_router, b_router,
           w_exp_gate, w_exp_up, w_exp_down, w_sh_gate, w_sh_up, w_sh_down):
    B, S, D = x.shape
    T = B * S
    depth = w_ada.shape[0]
    assert depth == 1, "LAM_INIT and the single-layer pipeline assume depth 1"
    assert D == N_HEADS * HEAD_DV and S % LANES == 0
    l = 0
    tm = _tile(S, 512)
    tq = _tile(S, 1024)

    x2 = x.reshape(T, D)
    mod3 = _ada(c, w_ada[l], b_ada[l]).reshape(B, 6, D)

    qg = (jnp.tile(q_norm_g[l], D // HEAD_DK) * (HEAD_DK ** -0.5 * LOG2E)).reshape(1, D)
    kg = jnp.tile(k_norm_g[l], D // HEAD_DK).reshape(1, D)
    q, k, v, p_in, ga, gp = _inproj(x2, mod3, norm1_g[l].reshape(1, D), w_in[l].astype(BF16),
                                    qg, kg, S, tm)

    strip = _bias_strip(rel_bias_table, S) * LOG2E
    nq = S // tq
    n = S + tq
    win = jnp.stack([strip[:, S - (i + 1) * tq: S - (i + 1) * tq + n] for i in range(nq)], axis=1)
    win = win.reshape(N_HEADS * nq, 1, n)
    lam_p = jnp.stack([lambda_q1[l], lambda_k1[l], lambda_q2[l], lambda_k2[l]])
    am = _attention(q, k, v, ga, win, lam_p, subln_g[l].reshape(1, HEAD_DV), B, S, tq)

    x1, h2r, logits_t = _mix(x2, am, gp, p_in, mod3, pool_w[l].astype(BF16),
                                  pool_scale[l].reshape(1, D), w_out[l].astype(BF16),
                                  norm2_g[l].reshape(1, D), w_router[l].T, S, tm)

    eidx, rank, wts, counts = _route(logits_t, b_router[l], _tile(T, 512))

    cnt = counts[:, 0].astype(jnp.int32)
    nblk = (cnt + MOE_BLOCK - 1) // MOE_BLOCK
    blk_end = jnp.cumsum(nblk)
    base = ((blk_end - nblk) * MOE_BLOCK).astype(F32).reshape(N_EXPERTS, 1)
    lastblk = jnp.where(nblk > 0, blk_end - 1, -1).astype(jnp.int32)
    n_blocks = -(-(T * TOP_K) // MOE_BLOCK) + N_EXPERTS
    blk_first = (blk_end - nblk).astype(jnp.int32)
    nused = blk_end[-1:].astype(jnp.int32)

    dest = _dest(eidx, rank, base, _tile(T, 512))
    xs, sh = _dispatch(lastblk, dest, h2r, w_sh_gate[l].astype(BF16), w_sh_up[l].astype(BF16),
                       w_sh_down[l].astype(BF16), n_blocks * MOE_BLOCK, _tile(S, 512))
    ys = _experts(blk_first, nblk.astype(jnp.int32), nused, xs, w_exp_gate[l], w_exp_up[l], w_exp_down[l])
    out = _combine(dest, wts.T, ys, x1, sh, mod3, S, _tile(S, 256))
    return out.reshape(B, S, D)
```

```python
import functools
import math

import jax
import jax.numpy as jnp
from jax import lax
from jax.experimental import pallas as pl
from jax.experimental.pallas import tpu as pltpu

N_HEADS = 8
HEAD_DK = 64
HEAD_DV = 2 * HEAD_DK
N_BUCKETS = 32
MAX_DISTANCE = 128
POOL_WINDOWS = (2, 4, 8, 16)
N_POOL_GROUPS = 4
N_EXPERTS = 256
TOP_K = 8
N_EXPERT_GROUPS = 8
TOPK_GROUPS = 4
ROUTED_SCALE = 2.5
MOE_BLOCK = 512
EXPERT_IN_BUFS = 3
EXPERT_OUT_BUFS = 2
EPS = 1e-6
LAM_INIT = 0.8 - 0.6 * math.exp(-0.3 * 0)
LOG2E = math.log2(math.e)
KEY_TILE = 512

LANES = 128
SUBLANES = 8
SLAB = 4
VMEM_LIMIT = 56 * 1024 * 1024

F32 = jnp.float32
BF16 = jnp.bfloat16
HIGHEST = lax.Precision.HIGHEST
NEG_INF = float("-inf")


def _params(*sem):
    return pltpu.CompilerParams(dimension_semantics=sem, vmem_limit_bytes=VMEM_LIMIT)


def _const_spec(shape):
    nd = len(shape)
    return pl.BlockSpec(shape, lambda *_: (0,) * nd, pipeline_mode=pl.Buffered(1))


def _pack_rows(x):
    half = x.shape[1] // 2
    lo = lax.bitcast_convert_type(x[:, :half].astype(BF16).astype(F32), jnp.uint32) >> 16
    hi = lax.bitcast_convert_type(x[:, half:].astype(BF16).astype(F32), jnp.uint32) & jnp.uint32(0xFFFF0000)
    return hi | lo


def _unpack_rows(w):
    lo = lax.bitcast_convert_type(w << 16, F32)
    hi = lax.bitcast_convert_type(w & jnp.uint32(0xFFFF0000), F32)
    return jnp.concatenate([lo, hi], axis=-1)


def _rows_load(ref, first, n):
    flat = ref.reshape(ref.shape[0] * SLAB, LANES)
    return _unpack_rows(jnp.concatenate(
        [flat[pl.ds(first * SLAB + c, n, stride=SLAB), :] for c in range(SLAB)], axis=-1))


def _rows_store(ref, first, val):
    flat = ref.reshape(ref.shape[0] * SLAB, LANES)
    n = val.shape[0]
    w = _pack_rows(val)
    for c in range(SLAB):
        flat[pl.ds(first * SLAB + c, n, stride=SLAB), :] = w[:, c * LANES:(c + 1) * LANES]


def _ada_kernel(c_ref, w_ref, b_ref, o_ref):
    c = c_ref[...]
    s = c * jax.nn.sigmoid(c)
    o_ref[...] = jnp.dot(s, w_ref[...], precision=HIGHEST, preferred_element_type=F32) + b_ref[...]


def _ada(c, w_ada, b_ada):
    B, D = c.shape
    n = w_ada.shape[1] // D
    return pl.pallas_call(
        _ada_kernel,
        grid=(n,),
        in_specs=[pl.BlockSpec((B, D), lambda j: (0, 0)),
                  pl.BlockSpec((D, D), lambda j: (0, j)),
                  pl.BlockSpec((1, D), lambda j: (0, j))],
        out_specs=pl.BlockSpec((B, D), lambda j: (0, j)),
        out_shape=jax.ShapeDtypeStruct((B, n * D), F32),
        compiler_params=_params("arbitrary"),
        name="ada",
    )(c, w_ada, b_ada.reshape(1, -1))


def _bias_kernel(bk_ref, tab_ref, o_ref):
    bk = bk_ref[...]
    onehot = (lax.broadcasted_iota(jnp.int32, (N_BUCKETS, bk.shape[1]), 0) == bk).astype(F32)
    o_ref[...] = lax.dot_general(tab_ref[...], onehot, (((0,), (0,)), ((), ())),
                                 precision=HIGHEST, preferred_element_type=F32)


def _t5_buckets(rel):
    nb = N_BUCKETS // 2
    max_exact = nb // 2
    n = jnp.abs(rel)
    large = max_exact + (jnp.log(jnp.maximum(n, 1).astype(jnp.float32) / max_exact)
                         / math.log(MAX_DISTANCE / max_exact) * (nb - max_exact)).astype(jnp.int32)
    large = jnp.minimum(large, nb - 1)
    return jnp.where(rel > 0, nb, 0) + jnp.where(n < max_exact, n, large)


def _bias_strip(rel_table, S):
    L = 2 * S
    buckets = _t5_buckets(jnp.arange(L, dtype=jnp.int32) - S).reshape(1, L)
    return pl.pallas_call(
        _bias_kernel,
        out_shape=jax.ShapeDtypeStruct((N_HEADS, L), F32),
        name="bias",
    )(buckets, rel_table)


def _inproj_kernel(x_ref, mod_ref, g1_ref, w_ref, gsum_ref, gspread_ref, qg_ref, kg_ref,
                   q_ref, kt_ref, v_ref, p_ref, ga_ref, gp_ref):
    D = x_ref.shape[1]
    x = x_ref[...]
    h = x * lax.rsqrt(jnp.mean(x * x, axis=-1, keepdims=True) + EPS) * g1_ref[...]
    h = h * (1.0 + mod_ref[0, 1:2, :]) + mod_ref[0, 0:1, :]
    hb = h.astype(BF16)

    def chunk(c):
        return jnp.dot(hb, w_ref[:, c * D:(c + 1) * D], preferred_element_type=F32)

    def head_norm(y, g_ref):
        ss = jnp.dot((y * y).astype(BF16), gsum_ref[...], preferred_element_type=F32)
        r = lax.rsqrt(ss * (1.0 / HEAD_DK) + EPS)
        r_hi = r.astype(BF16)
        r_lo = (r - r_hi.astype(F32)).astype(BF16)
        scale = jnp.dot(jnp.concatenate([r_hi, r_lo], axis=-1), gspread_ref[...],
                        preferred_element_type=F32)
        return y * scale * g_ref[...]

    q_ref[...] = head_norm(chunk(0), qg_ref).astype(BF16)
    kn = head_norm(chunk(1), kg_ref)
    for hd in range(N_HEADS):
        for t in range(x_ref.shape[0] // KEY_TILE):
            blk = kn[t * KEY_TILE:(t + 1) * KEY_TILE, hd * HEAD_DV:(hd + 1) * HEAD_DV]
            kt_ref[0, hd, t] = blk.T.astype(BF16)
    v_ref[...] = chunk(2).astype(BF16)
    p_ref[...] = chunk(3)
    ga_ref[...] = jax.nn.sigmoid(chunk(4)).astype(BF16)
    gp_ref[...] = jax.nn.sigmoid(chunk(5)).astype(BF16)


def _inproj(x2, mod3, norm1_g, w_in_b, qg, kg, S, tm):
    T, D = x2.shape
    grp = jnp.arange(D) // HEAD_DK
    gsum = (grp[:, None] == jnp.arange(LANES)[None, :]).astype(BF16)
    gspread = jnp.concatenate([gsum.T, gsum.T], axis=0)
    per_seq = S // tm
    row = lambda i: (i, 0)
    tok = jax.ShapeDtypeStruct((T, D), BF16)
    kt = jax.ShapeDtypeStruct((T // S, N_HEADS, S // KEY_TILE, HEAD_DV, KEY_TILE), BF16)
    outs = [tok, kt, tok, jax.ShapeDtypeStruct((T, D), F32), tok, tok]
    tok_spec = pl.BlockSpec((tm, D), row)
    kt_spec = pl.BlockSpec((1, N_HEADS, tm // KEY_TILE, HEAD_DV, KEY_TILE),
                           lambda i: (i // per_seq, 0, i % per_seq, 0, 0))
    return pl.pallas_call(
        _inproj_kernel,
        grid=(T // tm,),
        in_specs=[pl.BlockSpec((tm, D), row),
                  pl.BlockSpec((1, 6, D), lambda i: (i // per_seq, 0, 0)),
                  _const_spec((1, D)),
                  _const_spec(w_in_b.shape),
                  _const_spec(gsum.shape),
                  _const_spec(gspread.shape),
                  _const_spec((1, D)),
                  _const_spec((1, D))],
        out_specs=[tok_spec, kt_spec, tok_spec, tok_spec, tok_spec, tok_spec],
        out_shape=outs,
        compiler_params=_params("arbitrary"),
        name="inproj",
    )(x2, mod3, norm1_g, w_in_b, gsum, gspread, qg, kg)


def _attn_kernel(q_ref, kt_ref, v_ref, ga_ref, win_ref, lam_ref, sg_ref, o_ref,
                 bias_ref, x_ref, m_ref, acc_ref):
    tq = q_ref.shape[0]
    n_kt, _, kt = kt_ref.shape[2:]
    n = win_ref.shape[2]

    @pl.when(pl.program_id(2) == 0)
    def _():
        xb = jnp.broadcast_to(win_ref[0, 0], (tq, n))
        rolled = pltpu.roll(xb, n - tq, 1, stride=1, stride_axis=0)
        for j in range(n_kt):
            bias_ref[j] = rolled[:, j * kt:(j + 1) * kt]

    lp = lam_ref[...]
    lam = (jnp.exp(jnp.sum(lp[0:1] * lp[1:2], axis=-1, keepdims=True))
           - jnp.exp(jnp.sum(lp[2:3] * lp[3:4], axis=-1, keepdims=True)) + LAM_INIT)

    q = q_ref[...]
    first = lax.broadcasted_iota(jnp.int32, (1, HEAD_DV), 1) < HEAD_DK
    zero = jnp.zeros_like(q)
    qq = jnp.concatenate([jnp.where(first, q, zero), jnp.where(first, zero, q)], axis=0)

    for j in range(n_kt):
        b = bias_ref[j]
        x = jnp.dot(qq, kt_ref[0, 0, j], preferred_element_type=F32) + jnp.concatenate([b, b], axis=0)
        x_ref[j] = x
        t = x[:, 0:LANES]
        for c in range(1, kt // LANES):
            t = jnp.maximum(t, x[:, c * LANES:(c + 1) * LANES])
        m_ref[...] = t if j == 0 else jnp.maximum(m_ref[...], t)
    m = jnp.max(m_ref[...], axis=-1, keepdims=True)

    m_ref[...] = jnp.broadcast_to(m, m_ref.shape)

    def pv_tile(j, carry):
        mb = m_ref[...]
        mb = jnp.concatenate([mb] * (kt // LANES), axis=-1)
        ones_col = (lax.broadcasted_iota(jnp.int32, (kt, LANES), 1) == 0).astype(BF16)
        p = jnp.exp2((x_ref[j] - mb).astype(BF16))
        rows = pl.ds(pl.multiple_of(j * kt, kt), kt)
        v_ext = jnp.concatenate([v_ref[rows, :], ones_col], axis=-1)
        pv = jnp.concatenate([jnp.dot(p[:tq], v_ext, preferred_element_type=F32),
                              jnp.dot(p[tq:], v_ext, preferred_element_type=F32)], axis=0)
        acc_ref[...] = acc_ref[...] + pv
        return carry

    acc_ref[...] = jnp.zeros_like(acc_ref)
    lax.fori_loop(0, n_kt, pv_tile, 0, unroll=2)
    acc = acc_ref[...]
    o1, l1 = acc[:tq, :HEAD_DV], acc[:tq, HEAD_DV:HEAD_DV + 1]
    o2, l2 = acc[tq:, :HEAD_DV], acc[tq:, HEAD_DV:HEAD_DV + 1]
    o = o1 / l1 - lam * (o2 / l2)
    o = o * lax.rsqrt(jnp.mean(o * o, axis=-1, keepdims=True) + EPS) * sg_ref[...] * (1.0 - LAM_INIT)
    o_ref[...] = (o * ga_ref[...].astype(F32)).astype(BF16)


def _attention(q, kt, v, ga, win, lam_p, subln_g, B, S, tq):
    T, D = q.shape
    nq = S // tq
    n = win.shape[2]
    n_kt = S // KEY_TILE
    qmap = lambda h, i, b: (b * nq + i, h)
    return pl.pallas_call(
        _attn_kernel,
        grid=(N_HEADS, nq, B),
        in_specs=[pl.BlockSpec((tq, HEAD_DV), qmap),
                  pl.BlockSpec((1, 1, n_kt, HEAD_DV, KEY_TILE), lambda h, i, b: (b, h, 0, 0, 0)),
                  pl.BlockSpec((S, HEAD_DV), lambda h, i, b: (b, h)),
                  pl.BlockSpec((tq, HEAD_DV), qmap),
                  pl.BlockSpec((1, 1, n), lambda h, i, b: (h * nq + i, 0, 0)),
                  pl.BlockSpec((4, HEAD_DK), lambda h, i, b: (0, 0)),
                  pl.BlockSpec((1, HEAD_DV), lambda h, i, b: (0, 0))],
        out_specs=pl.BlockSpec((tq, HEAD_DV), qmap),
        out_shape=jax.ShapeDtypeStruct((T, D), BF16),
        scratch_shapes=[pltpu.VMEM((n_kt, tq, KEY_TILE), F32),
                        pltpu.VMEM((n_kt, 2 * tq, KEY_TILE), F32),
                        pltpu.VMEM((2 * tq, LANES), F32),
                        pltpu.VMEM((2 * tq, HEAD_DV + LANES), F32)],
        compiler_params=_params("arbitrary", "arbitrary", "arbitrary"),
        name="attn",
    )(q, kt, v, ga, win, lam_p, subln_g)


def _mix_kernel(S, x_ref, am_ref, gp_ref, p_ref, pprev_ref, pnext_ref, mod_ref, pw_ref, ps_ref,
                wo_ref, g2_ref, wr_ref, x1_ref, h2r_ref, lg_ref):
    tm, D = x_ref.shape
    gc = D // N_POOL_GROUPS
    halo = SUBLANES
    ne = tm + 2 * halo
    pos0 = (pl.program_id(0) % (S // tm)) * tm
    pos_e = pos0 - halo + lax.broadcasted_iota(jnp.int32, (ne, 1), 0)
    valid = (pos_e >= 0) & (pos_e < S)
    ext = jnp.concatenate([pprev_ref[...], p_ref[...], pnext_ref[...]], axis=0)
    ext = jnp.where(valid, ext, 0.0)
    pos = pos0 + lax.broadcasted_iota(jnp.int32, (tm, 1), 0)

    merged = []
    for g, w in enumerate(POOL_WINDOWS):
        half = w // 2
        e = ext[:, g * gc:(g + 1) * gc]
        sw = e
        width = 1
        while width < w:
            sw = sw + pltpu.roll(sw, width, 0)
            width *= 2
        win = pltpu.roll(sw, ne - (half - 1), 0)[halo:halo + tm] if half > 1 else sw[halo:halo + tm]
        lo = jnp.clip(pos - half, 0, S - 1)
        hi = jnp.clip(pos + half - 1, 0, S - 1)
        cnt = (hi - lo + 1).astype(F32)
        mixed = win / cnt - p_ref[:, g * gc:(g + 1) * gc]
        pooled = jnp.dot(mixed.astype(BF16), pw_ref[g], preferred_element_type=F32)
        pooled = pooled * ps_ref[:, g * gc:(g + 1) * gc]
        merged.append(am_ref[:, g * gc:(g + 1) * gc].astype(F32)
                      + gp_ref[:, g * gc:(g + 1) * gc].astype(F32) * pooled)
    merged = jnp.concatenate(merged, axis=-1).astype(BF16)
    y = jnp.dot(merged, wo_ref[...], preferred_element_type=F32)
    x1 = x_ref[...] + mod_ref[0, 2:3, :] * y
    x1_ref[...] = x1
    h2 = x1 * lax.rsqrt(jnp.mean(x1 * x1, axis=-1, keepdims=True) + EPS) * g2_ref[...]
    h2 = h2 * (1.0 + mod_ref[0, 4:5, :]) + mod_ref[0, 3:4, :]
    _rows_store(h2r_ref, 0, h2)
    E = lg_ref.shape[0]
    nt = (((1,), (1,)), ((), ()))
    h_hi = h2.astype(BF16)
    h_lo = (h2 - h_hi.astype(F32)).astype(BF16)
    both = lax.dot_general(wr_ref[...], h_hi, nt, preferred_element_type=F32)
    lg_ref[...] = both[:E] + both[E:] + lax.dot_general(wr_ref[:E, :], h_lo, nt, preferred_element_type=F32)


def _mix(x2, am, gp, p_in, mod3, pool_w_b, pool_scale, w_out_b, norm2_g, w_router_t, S, tm):
    T, D = x2.shape
    per_seq = S // tm
    hb = tm // SUBLANES
    last = T // SUBLANES - 1
    row = lambda i: (i, 0)
    E = w_router_t.shape[0]
    w_hi = w_router_t.astype(BF16)
    w_router_t = jnp.concatenate([w_hi, (w_router_t - w_hi.astype(F32)).astype(BF16)], axis=0)
    return pl.pallas_call(
        functools.partial(_mix_kernel, S),
        grid=(T // tm,),
        in_specs=[pl.BlockSpec((tm, D), row),
                  pl.BlockSpec((tm, D), row),
                  pl.BlockSpec((tm, D), row),
                  pl.BlockSpec((tm, D), row),
                  pl.BlockSpec((SUBLANES, D), lambda i: (jnp.maximum(i * hb - 1, 0), 0)),
                  pl.BlockSpec((SUBLANES, D), lambda i: (jnp.minimum((i + 1) * hb, last), 0)),
                  pl.BlockSpec((1, 6, D), lambda i: (i // per_seq, 0, 0)),
                  _const_spec(pool_w_b.shape),
                  _const_spec((1, D)),
                  _const_spec((D, D)),
                  _const_spec((1, D)),
                  _const_spec((2 * E, D))],
        out_specs=[pl.BlockSpec((tm, D), row),
                   pl.BlockSpec((tm, SLAB, LANES), lambda i: (i, 0, 0)),
                   pl.BlockSpec((E, tm), lambda i: (0, i))],
        out_shape=[jax.ShapeDtypeStruct((T, D), F32),
                   jax.ShapeDtypeStruct((T, SLAB, LANES), jnp.uint32),
                   jax.ShapeDtypeStruct((E, T), F32)],
        compiler_params=_params("arbitrary"),
        name="mix",
    )(x2, am, gp, p_in, p_in, p_in, mod3, pool_w_b, pool_scale, w_out_b, norm2_g, w_router_t)


def _route_kernel(lg_ref, br_ref, tri_ref, eidx_ref, rank_ref, wts_ref, cnt_ref):
    E, tr = lg_ref.shape
    per = E // N_EXPERT_GROUPS

    @pl.when(pl.program_id(0) == 0)
    def _():
        cnt_ref[...] = jnp.zeros_like(cnt_ref)

    scores = jax.nn.sigmoid(lg_ref[...])
    biased = scores + br_ref[...]
    b3 = biased.reshape(N_EXPERT_GROUPS, per, tr)
    io_per = lax.broadcasted_iota(jnp.int32, b3.shape, 1)
    m1 = jnp.max(b3, axis=1, keepdims=True)
    i1 = jnp.min(jnp.where(b3 == m1, io_per, per), axis=1, keepdims=True)
    m2 = jnp.max(jnp.where(io_per == i1, NEG_INF, b3), axis=1, keepdims=True)
    gs = (m1 + m2)[:, 0, :]

    io_g = lax.broadcasted_iota(jnp.int32, gs.shape, 0)
    gsel = jnp.zeros(gs.shape, jnp.bool_)
    cur = gs
    for _ in range(TOPK_GROUPS):
        gm = jnp.max(cur, axis=0, keepdims=True)
        gi = jnp.min(jnp.where(cur == gm, io_g, N_EXPERT_GROUPS), axis=0, keepdims=True)
        pick = io_g == gi
        gsel = gsel | pick
        cur = jnp.where(pick, NEG_INF, cur)
    cur = jnp.where(gsel[:, None, :], b3, NEG_INF).reshape(E, tr)

    io_e = lax.broadcasted_iota(jnp.int32, (E, tr), 0)
    idxs, raw = [], []
    sel = jnp.zeros((E, tr), jnp.bool_)
    for _ in range(TOP_K):
        m = jnp.max(cur, axis=0, keepdims=True)
        idx = jnp.min(jnp.where(cur == m, io_e, E), axis=0, keepdims=True)
        pick = io_e == idx
        idxs.append(idx)
        raw.append(jnp.sum(jnp.where(pick, scores, 0.0), axis=0, keepdims=True))
        sel = sel | pick
        cur = jnp.where(pick, NEG_INF, cur)
    raw = jnp.concatenate(raw, axis=0)
    wts_ref[...] = raw / jnp.sum(raw, axis=0, keepdims=True) * ROUTED_SCALE
    eidx_ref[...] = jnp.concatenate(idxs, axis=0)

    self_f = sel.astype(F32)
    incl = jnp.dot(sel.astype(BF16), tri_ref[...], preferred_element_type=F32)
    before = cnt_ref[...] + incl - self_f
    ranks = [jnp.sum(jnp.where(io_e == idx, before, 0.0), axis=0, keepdims=True) for idx in idxs]
    rank_ref[...] = jnp.concatenate(ranks, axis=0).astype(jnp.int32)
    cnt_ref[...] = cnt_ref[...] + jnp.sum(self_f, axis=1, keepdims=True)


def _route(logits_t, b_router, tr):
    E, T = logits_t.shape
    tri = (jnp.arange(tr)[:, None] <= jnp.arange(tr)[None, :]).astype(BF16)
    col = lambda i: (0, i)
    return pl.pallas_call(
        _route_kernel,
        grid=(T // tr,),
        in_specs=[pl.BlockSpec((E, tr), col),
                  _const_spec((E, 1)),
                  _const_spec((tr, tr))],
        out_specs=[pl.BlockSpec((TOP_K, tr), col),
                   pl.BlockSpec((TOP_K, tr), col),
                   pl.BlockSpec((TOP_K, tr), col),
                   pl.BlockSpec((E, 1), lambda i: (0, 0))],
        out_shape=[jax.ShapeDtypeStruct((TOP_K, T), jnp.int32),
                   jax.ShapeDtypeStruct((TOP_K, T), jnp.int32),
                   jax.ShapeDtypeStruct((TOP_K, T), F32),
                   jax.ShapeDtypeStruct((E, 1), F32)],
        compiler_params=_params("arbitrary"),
        name="route",
    )(logits_t, b_router.reshape(E, 1), tri)


def _dest_kernel(eidx_ref, rank_ref, base_ref, dest_ref):
    K, tr = eidx_ref.shape
    E = base_ref.shape[0]
    io_e = lax.broadcasted_iota(jnp.int32, (E, tr), 0)
    base = base_ref[...]
    rows = [jnp.sum(jnp.where(io_e == eidx_ref[k:k + 1, :], base, 0.0), axis=0, keepdims=True)
            for k in range(K)]
    dest_ref[...] = jnp.concatenate(rows, axis=0).astype(jnp.int32) + rank_ref[...]


def _dest(eidx, rank, base, tr):
    K, T = eidx.shape
    col = lambda i: (0, i)
    return pl.pallas_call(
        _dest_kernel,
        grid=(T // tr,),
        in_specs=[pl.BlockSpec((K, tr), col), pl.BlockSpec((K, tr), col), _const_spec(base.shape)],
        out_specs=pl.BlockSpec((K, tr), col),
        out_shape=jax.ShapeDtypeStruct((K, T), jnp.int32),
        compiler_params=_params("arbitrary"),
        name="dest",
    )(eidx, rank, base)


def _dispatch_kernel(lastblk_ref, dest_ref, h_ref, wsg_ref, wsu_ref, wsd_ref, xs_ref, sh_ref, zbuf_ref, sem):
    td = h_ref.shape[0]

    @pl.when(pl.program_id(0) == 0)
    def _():
        zbuf_ref[...] = jnp.zeros_like(zbuf_ref)

        def zcopy(e):
            return pltpu.make_async_copy(zbuf_ref, xs_ref.at[pl.ds(lastblk_ref[e] * MOE_BLOCK, MOE_BLOCK)], sem)

        def start(e, c):
            @pl.when(lastblk_ref[e] >= 0)
            def _():
                zcopy(e).start()
            return c

        def wait(e, c):
            @pl.when(lastblk_ref[e] >= 0)
            def _():
                zcopy(e).wait()
            return c

        lax.fori_loop(0, N_EXPERTS, start, 0)
        lax.fori_loop(0, N_EXPERTS, wait, 0)

    def row_copy(t, k):
        return pltpu.make_async_copy(h_ref.at[t], xs_ref.at[dest_ref[k, t]], sem)

    def start_rows(t, c):
        for k in range(TOP_K):
            row_copy(t, k).start(priority=k % 2)
        return c

    def wait_rows():
        for _ in range(TOP_K):
            pltpu.make_async_copy(h_ref, xs_ref.at[pl.ds(0, td)], sem).wait()

    lax.fori_loop(0, td, start_rows, 0, unroll=4)

    h = _rows_load(h_ref, 0, td).astype(BF16)
    g = jnp.dot(h, wsg_ref[...], preferred_element_type=F32)
    u = jnp.dot(h, wsu_ref[...], preferred_element_type=F32)
    shared = jnp.dot((g * jax.nn.sigmoid(g) * u).astype(BF16), wsd_ref[...], preferred_element_type=F32)
    sh_ref[...] = shared.astype(BF16)

    wait_rows()


def _dispatch(lastblk, dest, h2r, wsg, wsu, wsd, n_pad, td):
    T = h2r.shape[0]
    D = wsg.shape[0]
    const = lambda shape: pl.BlockSpec(shape, lambda i, lb: (0,) * len(shape), pipeline_mode=pl.Buffered(1))
    return pl.pallas_call(
        _dispatch_kernel,
        grid_spec=pltpu.PrefetchScalarGridSpec(
            num_scalar_prefetch=1,
            grid=(T // td,),
            in_specs=[pl.BlockSpec((TOP_K, td), lambda i, lb: (0, i), memory_space=pltpu.SMEM),
                      pl.BlockSpec((td, SLAB, LANES), lambda i, lb: (i, 0, 0)),
                      const(wsg.shape), const(wsu.shape), const(wsd.shape)],
            out_specs=[pl.BlockSpec(memory_space=pl.ANY),
                       pl.BlockSpec((td, D), lambda i, lb: (i, 0))],
            scratch_shapes=[pltpu.VMEM((MOE_BLOCK, SLAB, LANES), jnp.uint32), pltpu.SemaphoreType.DMA]),
        out_shape=[jax.ShapeDtypeStruct((n_pad, SLAB, LANES), jnp.uint32),
                   jax.ShapeDtypeStruct((T, D), BF16)],
        compiler_params=_params("arbitrary"),
        name="dispatch",
    )(lastblk, dest, h2r, wsg, wsu, wsd)


def _expert_kernel(first_ref, nblk_ref, nused_ref, xs_ref, wg_ref, wu_ref, wd_ref, ys_ref,
                   xbuf_ref, ybuf_ref, sin, sout):
    e = pl.program_id(0)
    nused = nused_ref[0]

    def block(ref, b):
        return ref.at[pl.ds(b * MOE_BLOCK, MOE_BLOCK)]

    def copy_in(b):
        s = b % EXPERT_IN_BUFS
        return pltpu.make_async_copy(block(xs_ref, b), block(xbuf_ref, s), sin.at[s])

    def copy_out(b):
        s = b % EXPERT_OUT_BUFS
        return pltpu.make_async_copy(block(ybuf_ref, s), block(ys_ref, b), sout.at[s])

    @pl.when(e == 0)
    def _():
        for b in range(EXPERT_IN_BUFS - 1):
            @pl.when(b < nused)
            def _():
                copy_in(b).start()

    def one_block(i, carry):
        j = first_ref[e] + i

        @pl.when(j + EXPERT_IN_BUFS - 1 < nused)
        def _():
            copy_in(j + EXPERT_IN_BUFS - 1).start()

        copy_in(j).wait()

        @pl.when(j >= EXPERT_OUT_BUFS)
        def _():
            copy_out(j - EXPERT_OUT_BUFS).wait()

        x = _rows_load(xbuf_ref, (j % EXPERT_IN_BUFS) * MOE_BLOCK, MOE_BLOCK).astype(BF16)
        g = jnp.dot(x, wg_ref[0].astype(BF16), preferred_element_type=F32)
        u = jnp.dot(x, wu_ref[0].astype(BF16), preferred_element_type=F32)
        hmid = (g * jax.nn.sigmoid(g) * u).astype(BF16)
        y = jnp.dot(hmid, wd_ref[0].astype(BF16), preferred_element_type=F32)
        _rows_store(ybuf_ref, (j % EXPERT_OUT_BUFS) * MOE_BLOCK, y)
        copy_out(j).start()

        @pl.when(j == nused - 1)
        def _():
            for back in range(EXPERT_OUT_BUFS):
                @pl.when(j - back >= 0)
                def _():
                    copy_out(j - back).wait()
        return carry

    lax.fori_loop(0, nblk_ref[e], one_block, 0)


def _experts(blk_first, nblk, nused, xs, w_eg, w_eu, w_ed):
    E, D, DE = w_eg.shape
    wmap = lambda e, bf, nb, nu: (e, 0, 0)
    return pl.pallas_call(
        _expert_kernel,
        grid_spec=pltpu.PrefetchScalarGridSpec(
            num_scalar_prefetch=3,
            grid=(E,),
            in_specs=[pl.BlockSpec(memory_space=pl.ANY),
                      pl.BlockSpec((1, D, DE), wmap),
                      pl.BlockSpec((1, D, DE), wmap),
                      pl.BlockSpec((1, DE, D), wmap)],
            out_specs=pl.BlockSpec(memory_space=pl.ANY),
            scratch_shapes=[pltpu.VMEM((EXPERT_IN_BUFS * MOE_BLOCK, SLAB, LANES), jnp.uint32),
                            pltpu.VMEM((EXPERT_OUT_BUFS * MOE_BLOCK, SLAB, LANES), jnp.uint32),
                            pltpu.SemaphoreType.DMA((EXPERT_IN_BUFS,)),
                            pltpu.SemaphoreType.DMA((EXPERT_OUT_BUFS,))]),
        out_shape=jax.ShapeDtypeStruct(xs.shape, jnp.uint32),
        compiler_params=_params("arbitrary"),
        name="expert",
    )(blk_first, nblk, nused, xs, w_eg, w_eu, w_ed)


def _combine_kernel(dest_ref, dnext_ref, wt_ref, ys_ref, x1_ref, sh_ref, mod_ref, o_ref, buf_ref, sems):
    tc = x1_ref.shape[0]
    i = pl.program_id(0)
    n = pl.num_programs(0)
    slot = i % 2
    per_slot = TOP_K * tc

    def row_copy(d_ref, s, t, k):
        return pltpu.make_async_copy(ys_ref.at[d_ref[k, t]], buf_ref.at[s * per_slot + k * tc + t], sems.at[s])

    def start_rows(d_ref, s, t):
        for k in range(TOP_K):
            row_copy(d_ref, s, t, k).start(priority=k % 2)

    def start_tile(d_ref, s):
        def body(t, c):
            start_rows(d_ref, s, t)
            return c
        lax.fori_loop(0, tc, body, 0, unroll=4)

    @pl.when(i == 0)
    def _():
        start_tile(dest_ref, slot)

    @pl.when(i + 1 < n)
    def _():
        start_tile(dnext_ref, 1 - slot)

    pltpu.make_async_copy(ys_ref.at[pl.ds(0, per_slot)], buf_ref.at[pl.ds(slot * per_slot, per_slot)],
                          sems.at[slot]).wait()

    g2 = mod_ref[0, 5:6, :]
    o_ref[...] = x1_ref[...] + g2 * sh_ref[...].astype(F32)
    for k in range(TOP_K):
        w = wt_ref[:, k:k + 1] * g2
        o_ref[...] = o_ref[...] + _rows_load(buf_ref, slot * per_slot + k * tc, tc) * w


def _combine(dest, wts_t, ys, x1, sh, mod3, S, tc):
    T, D = x1.shape
    per_seq = S // tc
    last = T // tc - 1
    row = lambda i: (i, 0)
    return pl.pallas_call(
        _combine_kernel,
        grid=(T // tc,),
        in_specs=[pl.BlockSpec((TOP_K, tc), lambda i: (0, i), memory_space=pltpu.SMEM),
                  pl.BlockSpec((TOP_K, tc), lambda i: (0, jnp.minimum(i + 1, last)), memory_space=pltpu.SMEM),
                  pl.BlockSpec((tc, TOP_K), row),
                  pl.BlockSpec(memory_space=pl.ANY),
                  pl.BlockSpec((tc, D), row),
                  pl.BlockSpec((tc, D), row),
                  pl.BlockSpec((1, 6, D), lambda i: (i // per_seq, 0, 0))],
        out_specs=pl.BlockSpec((tc, D), row),
        out_shape=jax.ShapeDtypeStruct((T, D), F32),
        scratch_shapes=[pltpu.VMEM((2 * TOP_K * tc, SLAB, LANES), jnp.uint32), pltpu.SemaphoreType.DMA((2,))],
        compiler_params=_params("arbitrary"),
        name="combine",
    )(dest, dest, wts_t, ys, x1, sh, mod3)


def _tile(n, want):
    t = min(n, want)
    while n % t:
        t //= 2
    return t


def kernel(x, c, rel_bias_table, w_ada, b_ada, norm1_g, w_in, q_norm_g, k_norm_g, lambda_q1, lambda_k1,
           lambda_q2, lambda_k2, subln_g, pool_w, pool_scale, w_out, norm2_g, w_router, b_router,
           w_exp_gate, w_exp_up, w_exp_down, w_sh_gate, w_sh_up, w_sh_down):
    B, S, D = x.shape
    T = B * S
    depth = w_ada.shape[0]
    assert depth == 1, "LAM_INIT and the single-layer pipeline assume depth 1"
    assert D == N_HEADS * HEAD_DV and S % LANES == 0
    l = 0
    tm = _tile(S, 512)
    tq = _tile(S, 1024)

    x2 = x.reshape(T, D)
    mod3 = _ada(c, w_ada[l], b_ada[l]).reshape(B, 6, D)

    qg = (jnp.tile(q_norm_g[l], D // HEAD_DK) * (HEAD_DK ** -0.5 * LOG2E)).reshape(1, D)
    kg = jnp.tile(k_norm_g[l], D // HEAD_DK).reshape(1, D)
    q, k, v, p_in, ga, gp = _inproj(x2, mod3, norm1_g[l].reshape(1, D), w_in[l].astype(BF16),
                                    qg, kg, S, tm)

    strip = _bias_strip(rel_bias_table, S) * LOG2E
    nq = S // tq
    n = S + tq
    win = jnp.stack([strip[:, S - (i + 1) * tq: S - (i + 1) * tq + n] for i in range(nq)], axis=1)
    win = win.reshape(N_HEADS * nq, 1, n)
    lam_p = jnp.stack([lambda_q1[l], lambda_k1[l], lambda_q2[l], lambda_k2[l]])
    am = _attention(q, k, v, ga, win, lam_p, subln_g[l].reshape(1, HEAD_DV), B, S, tq)

    x1, h2r, logits_t = _mix(x2, am, gp, p_in, mod3, pool_w[l].astype(BF16),
                                  pool_scale[l].reshape(1, D), w_out[l].astype(BF16),
                                  norm2_g[l].reshape(1, D), w_router[l].T, S, tm)

    eidx, rank, wts, counts = _route(logits_t, b_router[l], _tile(T, 512))

    cnt = counts[:, 0].astype(jnp.int32)
    nblk = (cnt + MOE_BLOCK - 1) // MOE_BLOCK
    blk_end = jnp.cumsum(nblk)
    base = ((blk_end - nblk) * MOE_BLOCK).astype(F32).reshape(N_EXPERTS, 1)
    lastblk = jnp.where(nblk > 0, blk_end - 1, -1).astype(jnp.int32)
    n_blocks = -(-(T * TOP_K) // MOE_BLOCK) + N_EXPERTS
    blk_first = (blk_end - nblk).astype(jnp.int32)
    nused = blk_end[-1:].astype(jnp.int32)

    dest = _dest(eidx, rank, base, _tile(T, 512))
    xs, sh = _dispatch(lastblk, dest, h2r, w_sh_gate[l].astype(BF16), w_sh_up[l].astype(BF16),
                       w_sh_down[l].astype(BF16), n_blocks * MOE_BLOCK, _tile(S, 512))
    ys = _experts(blk_first, nblk.astype(jnp.int32), nused, xs, w_exp_gate[l], w_exp_up[l], w_exp_down[l])
    out = _combine(dest, wts.T, ys, x1, sh, mod3, S, _tile(S, 256))
    return out.reshape(B, S, D)
```

```python
import functools
import math

import jax
import jax.numpy as jnp
from jax import lax
from jax.experimental import pallas as pl
from jax.experimental.pallas import tpu as pltpu

N_HEADS = 8
HEAD_DK = 64
HEAD_DV = 2 * HEAD_DK
N_BUCKETS = 32
MAX_DISTANCE = 128
POOL_WINDOWS = (2, 4, 8, 16)
N_POOL_GROUPS = 4
N_EXPERTS = 256
TOP_K = 8
N_EXPERT_GROUPS = 8
TOPK_GROUPS = 4
ROUTED_SCALE = 2.5
MOE_BLOCK = 1024
EXPERT_IN_BUFS = 3
EXPERT_OUT_BUFS = 2
EPS = 1e-6
LAM_INIT = 0.8 - 0.6 * math.exp(-0.3 * 0)
LOG2E = math.log2(math.e)
KEY_TILE = 512

LANES = 128
SUBLANES = 8
SLAB = 4
VMEM_LIMIT = 56 * 1024 * 1024

F32 = jnp.float32
BF16 = jnp.bfloat16
HIGHEST = lax.Precision.HIGHEST
NEG_INF = float("-inf")


def _params(*sem):
    return pltpu.CompilerParams(dimension_semantics=sem, vmem_limit_bytes=VMEM_LIMIT)


def _const_spec(shape):
    nd = len(shape)
    return pl.BlockSpec(shape, lambda *_: (0,) * nd, pipeline_mode=pl.Buffered(1))


def _pack_rows(x):
    half = x.shape[1] // 2
    lo = lax.bitcast_convert_type(x[:, :half].astype(BF16).astype(F32), jnp.uint32) >> 16
    hi = lax.bitcast_convert_type(x[:, half:].astype(BF16).astype(F32), jnp.uint32)
    return hi | lo


def _unpack_rows(w):
    lo = lax.bitcast_convert_type(w << 16, F32)
    hi = lax.bitcast_convert_type(w & jnp.uint32(0xFFFF0000), F32)
    return jnp.concatenate([lo, hi], axis=-1)


def _rows_load(ref, first, n):
    flat = ref.reshape(ref.shape[0] * SLAB, LANES)
    return _unpack_rows(jnp.concatenate(
        [flat[pl.ds(first * SLAB + c, n, stride=SLAB), :] for c in range(SLAB)], axis=-1))


def _rows_store(ref, first, val):
    flat = ref.reshape(ref.shape[0] * SLAB, LANES)
    n = val.shape[0]
    w = _pack_rows(val)
    for c in range(SLAB):
        flat[pl.ds(first * SLAB + c, n, stride=SLAB), :] = w[:, c * LANES:(c + 1) * LANES]


def _ada_kernel(c_ref, w_ref, b_ref, o_ref):
    c = c_ref[...]
    s = c * jax.nn.sigmoid(c)
    o_ref[...] = jnp.dot(s, w_ref[...], precision=HIGHEST, preferred_element_type=F32) + b_ref[...]


def _ada(c, w_ada, b_ada):
    B, D = c.shape
    n = w_ada.shape[1] // D
    return pl.pallas_call(
        _ada_kernel,
        grid=(n,),
        in_specs=[pl.BlockSpec((B, D), lambda j: (0, 0)),
                  pl.BlockSpec((D, D), lambda j: (0, j)),
                  pl.BlockSpec((1, D), lambda j: (0, j))],
        out_specs=pl.BlockSpec((B, D), lambda j: (0, j)),
        out_shape=jax.ShapeDtypeStruct((B, n * D), F32),
        compiler_params=_params("arbitrary"),
        name="ada",
    )(c, w_ada, b_ada.reshape(1, -1))


def _bias_kernel(bk_ref, tab_ref, o_ref):
    bk = bk_ref[...]
    onehot = (lax.broadcasted_iota(jnp.int32, (N_BUCKETS, bk.shape[1]), 0) == bk).astype(F32)
    o_ref[...] = lax.dot_general(tab_ref[...], onehot, (((0,), (0,)), ((), ())),
                                 precision=HIGHEST, preferred_element_type=F32)


def _t5_buckets(rel):
    nb = N_BUCKETS // 2
    max_exact = nb // 2
    n = jnp.abs(rel)
    large = max_exact + (jnp.log(jnp.maximum(n, 1).astype(jnp.float32) / max_exact)
                         / math.log(MAX_DISTANCE / max_exact) * (nb - max_exact)).astype(jnp.int32)
    large = jnp.minimum(large, nb - 1)
    return jnp.where(rel > 0, nb, 0) + jnp.where(n < max_exact, n, large)


def _bias_strip(rel_table, S):
    L = 2 * S
    buckets = _t5_buckets(jnp.arange(L, dtype=jnp.int32) - S).reshape(1, L)
    return pl.pallas_call(
        _bias_kernel,
        out_shape=jax.ShapeDtypeStruct((N_HEADS, L), F32),
        name="bias",
    )(buckets, rel_table)


def _inproj_kernel(x_ref, mod_ref, g1_ref, w_ref, gsum_ref, gspread_ref, qg_ref, kg_ref,
                   q_ref, kt_ref, v_ref, p_ref, ga_ref, gp_ref):
    D = x_ref.shape[1]
    x = x_ref[...]
    h = x * lax.rsqrt(jnp.mean(x * x, axis=-1, keepdims=True) + EPS) * g1_ref[...]
    h = h * (1.0 + mod_ref[0, 1:2, :]) + mod_ref[0, 0:1, :]
    hb = h.astype(BF16)

    def chunk(c):
        return jnp.dot(hb, w_ref[:, c * D:(c + 1) * D], preferred_element_type=F32)

    def head_norm(y, g_ref):
        ss = jnp.dot((y * y).astype(BF16), gsum_ref[...], preferred_element_type=F32)
        r = lax.rsqrt(ss * (1.0 / HEAD_DK) + EPS)
        r_hi = r.astype(BF16)
        r_lo = (r - r_hi.astype(F32)).astype(BF16)
        scale = jnp.dot(jnp.concatenate([r_hi, r_lo], axis=-1), gspread_ref[...],
                        preferred_element_type=F32)
        return y * scale * g_ref[...]

    q_ref[...] = head_norm(chunk(0), qg_ref).astype(BF16)
    kn = head_norm(chunk(1), kg_ref)
    for hd in range(N_HEADS):
        for t in range(x_ref.shape[0] // KEY_TILE):
            blk = kn[t * KEY_TILE:(t + 1) * KEY_TILE, hd * HEAD_DV:(hd + 1) * HEAD_DV]
            kt_ref[0, hd, t] = blk.T.astype(BF16)
    v_ref[...] = chunk(2).astype(BF16)
    p_ref[...] = chunk(3)
    ga_ref[...] = jax.nn.sigmoid(chunk(4)).astype(BF16)
    gp_ref[...] = jax.nn.sigmoid(chunk(5)).astype(BF16)


def _inproj(x2, mod3, norm1_g, w_in_b, qg, kg, S, tm):
    T, D = x2.shape
    grp = jnp.arange(D) // HEAD_DK
    gsum = (grp[:, None] == jnp.arange(LANES)[None, :]).astype(BF16)
    gspread = jnp.concatenate([gsum.T, gsum.T], axis=0)
    per_seq = S // tm
    row = lambda i: (i, 0)
    tok = jax.ShapeDtypeStruct((T, D), BF16)
    kt = jax.ShapeDtypeStruct((T // S, N_HEADS, S // KEY_TILE, HEAD_DV, KEY_TILE), BF16)
    outs = [tok, kt, tok, jax.ShapeDtypeStruct((T, D), F32), tok, tok]
    tok_spec = pl.BlockSpec((tm, D), row)
    kt_spec = pl.BlockSpec((1, N_HEADS, tm // KEY_TILE, HEAD_DV, KEY_TILE),
                           lambda i: (i // per_seq, 0, i % per_seq, 0, 0))
    return pl.pallas_call(
        _inproj_kernel,
        grid=(T // tm,),
        in_specs=[pl.BlockSpec((tm, D), row),
                  pl.BlockSpec((1, 6, D), lambda i: (i // per_seq, 0, 0)),
                  _const_spec((1, D)),
                  _const_spec(w_in_b.shape),
                  _const_spec(gsum.shape),
                  _const_spec(gspread.shape),
                  _const_spec((1, D)),
                  _const_spec((1, D))],
        out_specs=[tok_spec, kt_spec, tok_spec, tok_spec, tok_spec, tok_spec],
        out_shape=outs,
        compiler_params=_params("arbitrary"),
        name="inproj",
    )(x2, mod3, norm1_g, w_in_b, gsum, gspread, qg, kg)


def _attn_kernel(q_ref, kt_ref, v_ref, ga_ref, win_ref, lam_ref, sg_ref, o_ref,
                 bias_ref, x_ref, m_ref, acc_ref):
    tq = q_ref.shape[0]
    n_kt, _, kt = kt_ref.shape[2:]
    n = win_ref.shape[2]

    @pl.when(pl.program_id(2) == 0)
    def _():
        xb = jnp.broadcast_to(win_ref[0, 0], (tq, n))
        rolled = pltpu.roll(xb, n - tq, 1, stride=1, stride_axis=0)
        for j in range(n_kt):
            bias_ref[j] = rolled[:, j * kt:(j + 1) * kt]

    lp = lam_ref[...]
    lam = (jnp.exp(jnp.sum(lp[0:1] * lp[1:2], axis=-1, keepdims=True))
           - jnp.exp(jnp.sum(lp[2:3] * lp[3:4], axis=-1, keepdims=True)) + LAM_INIT)

    q = q_ref[...]
    first = lax.broadcasted_iota(jnp.int32, (1, HEAD_DV), 1) < HEAD_DK
    zero = jnp.zeros_like(q)
    qq = jnp.concatenate([jnp.where(first, q, zero), jnp.where(first, zero, q)], axis=0)

    for j in range(n_kt):
        b = bias_ref[j]
        x = jnp.dot(qq, kt_ref[0, 0, j], preferred_element_type=F32) + jnp.concatenate([b, b], axis=0)
        x_ref[j] = x
        t = x[:, 0:LANES]
        for c in range(1, kt // LANES):
            t = jnp.maximum(t, x[:, c * LANES:(c + 1) * LANES])
        m_ref[...] = t if j == 0 else jnp.maximum(m_ref[...], t)
    m = jnp.max(m_ref[...], axis=-1, keepdims=True)

    m_ref[...] = jnp.broadcast_to(m, m_ref.shape)

    def pv_tile(j, carry):
        mb = m_ref[...]
        mb = jnp.concatenate([mb] * (kt // LANES), axis=-1)
        ones_col = (lax.broadcasted_iota(jnp.int32, (kt, LANES), 1) == 0).astype(BF16)
        p = jnp.exp2((x_ref[j] - mb).astype(BF16))
        rows = pl.ds(pl.multiple_of(j * kt, kt), kt)
        v_ext = jnp.concatenate([v_ref[rows, :], ones_col], axis=-1)
        pv = jnp.concatenate([jnp.dot(p[:tq], v_ext, preferred_element_type=F32),
                              jnp.dot(p[tq:], v_ext, preferred_element_type=F32)], axis=0)
        acc_ref[...] = acc_ref[...] + pv
        return carry

    acc_ref[...] = jnp.zeros_like(acc_ref)
    lax.fori_loop(0, n_kt, pv_tile, 0, unroll=2)
    acc = acc_ref[...]
    o1, l1 = acc[:tq, :HEAD_DV], acc[:tq, HEAD_DV:HEAD_DV + 1]
    o2, l2 = acc[tq:, :HEAD_DV], acc[tq:, HEAD_DV:HEAD_DV + 1]
    o = o1 / l1 - lam * (o2 / l2)
    o = o * lax.rsqrt(jnp.mean(o * o, axis=-1, keepdims=True) + EPS) * sg_ref[...] * (1.0 - LAM_INIT)
    o_ref[...] = (o * ga_ref[...].astype(F32)).astype(BF16)


def _attention(q, kt, v, ga, win, lam_p, subln_g, B, S, tq):
    T, D = q.shape
    nq = S // tq
    n = win.shape[2]
    n_kt = S // KEY_TILE
    qmap = lambda h, i, b: (b * nq + i, h)
    return pl.pallas_call(
        _attn_kernel,
        grid=(N_HEADS, nq, B),
        in_specs=[pl.BlockSpec((tq, HEAD_DV), qmap),
                  pl.BlockSpec((1, 1, n_kt, HEAD_DV, KEY_TILE), lambda h, i, b: (b, h, 0, 0, 0)),
                  pl.BlockSpec((S, HEAD_DV), lambda h, i, b: (b, h)),
                  pl.BlockSpec((tq, HEAD_DV), qmap),
                  pl.BlockSpec((1, 1, n), lambda h, i, b: (h * nq + i, 0, 0)),
                  pl.BlockSpec((4, HEAD_DK), lambda h, i, b: (0, 0)),
                  pl.BlockSpec((1, HEAD_DV), lambda h, i, b: (0, 0))],
        out_specs=pl.BlockSpec((tq, HEAD_DV), qmap),
        out_shape=jax.ShapeDtypeStruct((T, D), BF16),
        scratch_shapes=[pltpu.VMEM((n_kt, tq, KEY_TILE), F32),
                        pltpu.VMEM((n_kt, 2 * tq, KEY_TILE), F32),
                        pltpu.VMEM((2 * tq, LANES), F32),
                        pltpu.VMEM((2 * tq, HEAD_DV + LANES), F32)],
        compiler_params=_params("arbitrary", "arbitrary", "arbitrary"),
        name="attn",
    )(q, kt, v, ga, win, lam_p, subln_g)


def _mix_kernel(S, x_ref, am_ref, gp_ref, p_ref, pprev_ref, pnext_ref, mod_ref, pw_ref, ps_ref,
                wo_ref, g2_ref, wr_ref, x1_ref, h2r_ref, lg_ref):
    tm, D = x_ref.shape
    gc = D // N_POOL_GROUPS
    halo = SUBLANES
    ne = tm + 2 * halo
    pos0 = (pl.program_id(0) % (S // tm)) * tm
    pos_e = pos0 - halo + lax.broadcasted_iota(jnp.int32, (ne, 1), 0)
    valid = (pos_e >= 0) & (pos_e < S)
    ext = jnp.concatenate([pprev_ref[...], p_ref[...], pnext_ref[...]], axis=0)
    ext = jnp.where(valid, ext, 0.0)
    pos = pos0 + lax.broadcasted_iota(jnp.int32, (tm, 1), 0)

    merged = []
    for g, w in enumerate(POOL_WINDOWS):
        half = w // 2
        e = ext[:, g * gc:(g + 1) * gc]
        sw = e
        width = 1
        while width < w:
            sw = sw + pltpu.roll(sw, width, 0)
            width *= 2
        win = pltpu.roll(sw, ne - (half - 1), 0)[halo:halo + tm] if half > 1 else sw[halo:halo + tm]
        lo = jnp.clip(pos - half, 0, S - 1)
        hi = jnp.clip(pos + half - 1, 0, S - 1)
        cnt = (hi - lo + 1).astype(F32)
        mixed = win / cnt - p_ref[:, g * gc:(g + 1) * gc]
        pooled = jnp.dot(mixed.astype(BF16), pw_ref[g], preferred_element_type=F32)
        pooled = pooled * ps_ref[:, g * gc:(g + 1) * gc]
        merged.append(am_ref[:, g * gc:(g + 1) * gc].astype(F32)
                      + gp_ref[:, g * gc:(g + 1) * gc].astype(F32) * pooled)
    merged = jnp.concatenate(merged, axis=-1).astype(BF16)
    y = jnp.dot(merged, wo_ref[...], preferred_element_type=F32)
    x1 = x_ref[...] + mod_ref[0, 2:3, :] * y
    x1_ref[...] = x1
    h2 = x1 * lax.rsqrt(jnp.mean(x1 * x1, axis=-1, keepdims=True) + EPS) * g2_ref[...]
    h2 = h2 * (1.0 + mod_ref[0, 4:5, :]) + mod_ref[0, 3:4, :]
    _rows_store(h2r_ref, 0, h2)
    E = lg_ref.shape[0]
    nt = (((1,), (1,)), ((), ()))
    h_hi = h2.astype(BF16)
    h_lo = (h2 - h_hi.astype(F32)).astype(BF16)
    both = lax.dot_general(wr_ref[...], h_hi, nt, preferred_element_type=F32)
    lg_ref[...] = both[:E] + both[E:] + lax.dot_general(wr_ref[:E, :], h_lo, nt, preferred_element_type=F32)


def _mix(x2, am, gp, p_in, mod3, pool_w_b, pool_scale, w_out_b, norm2_g, w_router_t, S, tm):
    T, D = x2.shape
    per_seq = S // tm
    hb = tm // SUBLANES
    last = T // SUBLANES - 1
    row = lambda i: (i, 0)
    E = w_router_t.shape[0]
    w_hi = w_router_t.astype(BF16)
    w_router_t = jnp.concatenate([w_hi, (w_router_t - w_hi.astype(F32)).astype(BF16)], axis=0)
    return pl.pallas_call(
        functools.partial(_mix_kernel, S),
        grid=(T // tm,),
        in_specs=[pl.BlockSpec((tm, D), row),
                  pl.BlockSpec((tm, D), row),
                  pl.BlockSpec((tm, D), row),
                  pl.BlockSpec((tm, D), row),
                  pl.BlockSpec((SUBLANES, D), lambda i: (jnp.maximum(i * hb - 1, 0), 0)),
                  pl.BlockSpec((SUBLANES, D), lambda i: (jnp.minimum((i + 1) * hb, last), 0)),
                  pl.BlockSpec((1, 6, D), lambda i: (i // per_seq, 0, 0)),
                  _const_spec(pool_w_b.shape),
                  _const_spec((1, D)),
                  _const_spec((D, D)),
                  _const_spec((1, D)),
                  _const_spec((2 * E, D))],
        out_specs=[pl.BlockSpec((tm, D), row),
                   pl.BlockSpec((tm, SLAB, LANES), lambda i: (i, 0, 0)),
                   pl.BlockSpec((E, tm), lambda i: (0, i))],
        out_shape=[jax.ShapeDtypeStruct((T, D), F32),
                   jax.ShapeDtypeStruct((T, SLAB, LANES), jnp.uint32),
                   jax.ShapeDtypeStruct((E, T), F32)],
        compiler_params=_params("arbitrary"),
        name="mix",
    )(x2, am, gp, p_in, p_in, p_in, mod3, pool_w_b, pool_scale, w_out_b, norm2_g, w_router_t)


def _route_kernel(lg_ref, br_ref, tri_ref, eidx_ref, rank_ref, wts_ref, cnt_ref):
    E, tr = lg_ref.shape
    per = E // N_EXPERT_GROUPS

    @pl.when(pl.program_id(0) == 0)
    def _():
        cnt_ref[...] = jnp.zeros_like(cnt_ref)

    scores = jax.nn.sigmoid(lg_ref[...])
    biased = scores + br_ref[...]
    b3 = biased.reshape(N_EXPERT_GROUPS, per, tr)
    io_per = lax.broadcasted_iota(jnp.int32, b3.shape, 1)
    m1 = jnp.max(b3, axis=1, keepdims=True)
    i1 = jnp.min(jnp.where(b3 == m1, io_per, per), axis=1, keepdims=True)
    m2 = jnp.max(jnp.where(io_per == i1, NEG_INF, b3), axis=1, keepdims=True)
    gs = (m1 + m2)[:, 0, :]

    io_g = lax.broadcasted_iota(jnp.int32, gs.shape, 0)
    gsel = jnp.zeros(gs.shape, jnp.bool_)
    cur = gs
    for _ in range(TOPK_GROUPS):
        gm = jnp.max(cur, axis=0, keepdims=True)
        gi = jnp.min(jnp.where(cur == gm, io_g, N_EXPERT_GROUPS), axis=0, keepdims=True)
        pick = io_g == gi
        gsel = gsel | pick
        cur = jnp.where(pick, NEG_INF, cur)
    cur = jnp.where(gsel[:, None, :], b3, NEG_INF).reshape(E, tr)

    io_e = lax.broadcasted_iota(jnp.int32, (E, tr), 0)
    idxs, raw = [], []
    sel = jnp.zeros((E, tr), jnp.bool_)
    for _ in range(TOP_K):
        m = jnp.max(cur, axis=0, keepdims=True)
        idx = jnp.min(jnp.where(cur == m, io_e, E), axis=0, keepdims=True)
        pick = io_e == idx
        idxs.append(idx)
        raw.append(jnp.sum(jnp.where(pick, scores, 0.0), axis=0, keepdims=True))
        sel = sel | pick
        cur = jnp.where(pick, NEG_INF, cur)
    raw = jnp.concatenate(raw, axis=0)
    wts_ref[...] = raw / jnp.sum(raw, axis=0, keepdims=True) * ROUTED_SCALE
    eidx_ref[...] = jnp.concatenate(idxs, axis=0)

    self_f = sel.astype(F32)
    incl = jnp.dot(sel.astype(BF16), tri_ref[...], preferred_element_type=F32)
    before = cnt_ref[...] + incl - self_f
    ranks = [jnp.sum(jnp.where(io_e == idx, before, 0.0), axis=0, keepdims=True) for idx in idxs]
    rank_ref[...] = jnp.concatenate(ranks, axis=0).astype(jnp.int32)
    cnt_ref[...] = cnt_ref[...] + jnp.sum(self_f, axis=1, keepdims=True)


def _route(logits_t, b_router, tr):
    E, T = logits_t.shape
    tri = (jnp.arange(tr)[:, None] <= jnp.arange(tr)[None, :]).astype(BF16)
    col = lambda i: (0, i)
    return pl.pallas_call(
        _route_kernel,
        grid=(T // tr,),
        in_specs=[pl.BlockSpec((E, tr), col),
                  _const_spec((E, 1)),
                  _const_spec((tr, tr))],
        out_specs=[pl.BlockSpec((TOP_K, tr), col),
                   pl.BlockSpec((TOP_K, tr), col),
                   pl.BlockSpec((TOP_K, tr), col),
                   pl.BlockSpec((E, 1), lambda i: (0, 0))],
        out_shape=[jax.ShapeDtypeStruct((TOP_K, T), jnp.int32),
                   jax.ShapeDtypeStruct((TOP_K, T), jnp.int32),
                   jax.ShapeDtypeStruct((TOP_K, T), F32),
                   jax.ShapeDtypeStruct((E, 1), F32)],
        compiler_params=_params("arbitrary"),
        name="route",
    )(logits_t, b_router.reshape(E, 1), tri)


def _dest_kernel(eidx_ref, rank_ref, base_ref, dest_ref):
    K, tr = eidx_ref.shape
    E = base_ref.shape[0]
    io_e = lax.broadcasted_iota(jnp.int32, (E, tr), 0)
    base = base_ref[...]
    rows = [jnp.sum(jnp.where(io_e == eidx_ref[k:k + 1, :], base, 0.0), axis=0, keepdims=True)
            for k in range(K)]
    dest_ref[...] = jnp.concatenate(rows, axis=0).astype(jnp.int32) + rank_ref[...]


def _dest(eidx, rank, base, tr):
    K, T = eidx.shape
    col = lambda i: (0, i)
    return pl.pallas_call(
        _dest_kernel,
        grid=(T // tr,),
        in_specs=[pl.BlockSpec((K, tr), col), pl.BlockSpec((K, tr), col), _const_spec(base.shape)],
        out_specs=pl.BlockSpec((K, tr), col),
        out_shape=jax.ShapeDtypeStruct((K, T), jnp.int32),
        compiler_params=_params("arbitrary"),
        name="dest",
    )(eidx, rank, base)


def _dispatch_kernel(lastblk_ref, dest_ref, h_ref, wsg_ref, wsu_ref, wsd_ref, xs_ref, sh_ref, zbuf_ref, sem):
    td = h_ref.shape[0]

    @pl.when(pl.program_id(0) == 0)
    def _():
        zbuf_ref[...] = jnp.zeros_like(zbuf_ref)

        def zcopy(e):
            return pltpu.make_async_copy(zbuf_ref, xs_ref.at[pl.ds(lastblk_ref[e] * MOE_BLOCK, MOE_BLOCK)], sem)

        def start(e, c):
            @pl.when(lastblk_ref[e] >= 0)
            def _():
                zcopy(e).start()
            return c

        def wait(e, c):
            @pl.when(lastblk_ref[e] >= 0)
            def _():
                zcopy(e).wait()
            return c

        lax.fori_loop(0, N_EXPERTS, start, 0)
        lax.fori_loop(0, N_EXPERTS, wait, 0)

    def row_copy(t, k):
        return pltpu.make_async_copy(h_ref.at[t], xs_ref.at[dest_ref[k, t]], sem)

    def start_rows(t, c):
        for k in range(TOP_K):
            row_copy(t, k).start(priority=k % 2)
        return c

    def wait_rows():
        for _ in range(TOP_K):
            pltpu.make_async_copy(h_ref, xs_ref.at[pl.ds(0, td)], sem).wait()

    lax.fori_loop(0, td, start_rows, 0, unroll=4)

    h = _rows_load(h_ref, 0, td).astype(BF16)
    g = jnp.dot(h, wsg_ref[...], preferred_element_type=F32)
    u = jnp.dot(h, wsu_ref[...], preferred_element_type=F32)
    shared = jnp.dot((g * jax.nn.sigmoid(g) * u).astype(BF16), wsd_ref[...], preferred_element_type=F32)
    sh_ref[...] = shared.astype(BF16)

    wait_rows()


def _dispatch(lastblk, dest, h2r, wsg, wsu, wsd, n_pad, td):
    T = h2r.shape[0]
    D = wsg.shape[0]
    const = lambda shape: pl.BlockSpec(shape, lambda i, lb: (0,) * len(shape), pipeline_mode=pl.Buffered(1))
    return pl.pallas_call(
        _dispatch_kernel,
        grid_spec=pltpu.PrefetchScalarGridSpec(
            num_scalar_prefetch=1,
            grid=(T // td,),
            in_specs=[pl.BlockSpec((TOP_K, td), lambda i, lb: (0, i), memory_space=pltpu.SMEM),
                      pl.BlockSpec((td, SLAB, LANES), lambda i, lb: (i, 0, 0)),
                      const(wsg.shape), const(wsu.shape), const(wsd.shape)],
            out_specs=[pl.BlockSpec(memory_space=pl.ANY),
                       pl.BlockSpec((td, D), lambda i, lb: (i, 0))],
            scratch_shapes=[pltpu.VMEM((MOE_BLOCK, SLAB, LANES), jnp.uint32), pltpu.SemaphoreType.DMA]),
        out_shape=[jax.ShapeDtypeStruct((n_pad, SLAB, LANES), jnp.uint32),
                   jax.ShapeDtypeStruct((T, D), BF16)],
        compiler_params=_params("arbitrary"),
        name="dispatch",
    )(lastblk, dest, h2r, wsg, wsu, wsd)


def _expert_kernel(first_ref, nblk_ref, nused_ref, xs_ref, wg_ref, wu_ref, wd_ref, ys_ref,
                   xbuf_ref, ybuf_ref, sin, sout):
    e = pl.program_id(0)
    nused = nused_ref[0]

    def block(ref, b):
        return ref.at[pl.ds(b * MOE_BLOCK, MOE_BLOCK)]

    def copy_in(b):
        s = b % EXPERT_IN_BUFS
        return pltpu.make_async_copy(block(xs_ref, b), block(xbuf_ref, s), sin.at[s])

    def copy_out(b):
        s = b % EXPERT_OUT_BUFS
        return pltpu.make_async_copy(block(ybuf_ref, s), block(ys_ref, b), sout.at[s])

    @pl.when(e == 0)
    def _():
        for b in range(EXPERT_IN_BUFS - 1):
            @pl.when(b < nused)
            def _():
                copy_in(b).start()

    def one_block(i, carry):
        j = first_ref[e] + i

        @pl.when(j + EXPERT_IN_BUFS - 1 < nused)
        def _():
            copy_in(j + EXPERT_IN_BUFS - 1).start()

        copy_in(j).wait()

        @pl.when(j >= EXPERT_OUT_BUFS)
        def _():
            copy_out(j - EXPERT_OUT_BUFS).wait()

        x = _rows_load(xbuf_ref, (j % EXPERT_IN_BUFS) * MOE_BLOCK, MOE_BLOCK).astype(BF16)
        g = jnp.dot(x, wg_ref[0].astype(BF16), preferred_element_type=F32)
        u = jnp.dot(x, wu_ref[0].astype(BF16), preferred_element_type=F32)
        hmid = (g * jax.nn.sigmoid(g) * u).astype(BF16)
        y = jnp.dot(hmid, wd_ref[0].astype(BF16), preferred_element_type=F32)
        _rows_store(ybuf_ref, (j % EXPERT_OUT_BUFS) * MOE_BLOCK, y)
        copy_out(j).start()

        @pl.when(j == nused - 1)
        def _():
            for back in range(EXPERT_OUT_BUFS):
                @pl.when(j - back >= 0)
                def _():
                    copy_out(j - back).wait()
        return carry

    lax.fori_loop(0, nblk_ref[e], one_block, 0)


def _experts(blk_first, nblk, nused, xs, w_eg, w_eu, w_ed):
    E, D, DE = w_eg.shape
    wmap = lambda e, bf, nb, nu: (e, 0, 0)
    return pl.pallas_call(
        _expert_kernel,
        grid_spec=pltpu.PrefetchScalarGridSpec(
            num_scalar_prefetch=3,
            grid=(E,),
            in_specs=[pl.BlockSpec(memory_space=pl.ANY),
                      pl.BlockSpec((1, D, DE), wmap),
                      pl.BlockSpec((1, D, DE), wmap),
                      pl.BlockSpec((1, DE, D), wmap)],
            out_specs=pl.BlockSpec(memory_space=pl.ANY),
            scratch_shapes=[pltpu.VMEM((EXPERT_IN_BUFS * MOE_BLOCK, SLAB, LANES), jnp.uint32),
                            pltpu.VMEM((EXPERT_OUT_BUFS * MOE_BLOCK, SLAB, LANES), jnp.uint32),
                            pltpu.SemaphoreType.DMA((EXPERT_IN_BUFS,)),
                            pltpu.SemaphoreType.DMA((EXPERT_OUT_BUFS,))]),
        out_shape=jax.ShapeDtypeStruct(xs.shape, jnp.uint32),
        compiler_params=_params("arbitrary"),
        name="expert",
    )(blk_first, nblk, nused, xs, w_eg, w_eu, w_ed)


def _combine_kernel(dest_ref, dnext_ref, wt_ref, ys_ref, x1_ref, sh_ref, mod_ref, o_ref, buf_ref, sems):
    tc = x1_ref.shape[0]
    i = pl.program_id(0)
    n = pl.num_programs(0)
    slot = i % 2
    per_slot = TOP_K * tc

    def row_copy(d_ref, s, t, k):
        return pltpu.make_async_copy(ys_ref.at[d_ref[k, t]], buf_ref.at[s * per_slot + k * tc + t], sems.at[s])

    def start_rows(d_ref, s, t):
        for k in range(TOP_K):
            row_copy(d_ref, s, t, k).start(priority=k % 2)

    def start_tile(d_ref, s):
        def body(t, c):
            start_rows(d_ref, s, t)
            return c
        lax.fori_loop(0, tc, body, 0, unroll=4)

    @pl.when(i == 0)
    def _():
        start_tile(dest_ref, slot)

    @pl.when(i + 1 < n)
    def _():
        start_tile(dnext_ref, 1 - slot)

    pltpu.make_async_copy(ys_ref.at[pl.ds(0, per_slot)], buf_ref.at[pl.ds(slot * per_slot, per_slot)],
                          sems.at[slot]).wait()

    g2 = mod_ref[0, 5:6, :]
    o_ref[...] = x1_ref[...] + g2 * sh_ref[...].astype(F32)
    for k in range(TOP_K):
        w = wt_ref[:, k:k + 1] * g2
        o_ref[...] = o_ref[...] + _rows_load(buf_ref, slot * per_slot + k * tc, tc) * w


def _combine(dest, wts_t, ys, x1, sh, mod3, S, tc):
    T, D = x1.shape
    per_seq = S // tc
    last = T // tc - 1
    row = lambda i: (i, 0)
    return pl.pallas_call(
        _combine_kernel,
        grid=(T // tc,),
        in_specs=[pl.BlockSpec((TOP_K, tc), lambda i: (0, i), memory_space=pltpu.SMEM),
                  pl.BlockSpec((TOP_K, tc), lambda i: (0, jnp.minimum(i + 1, last)), memory_space=pltpu.SMEM),
                  pl.BlockSpec((tc, TOP_K), row),
                  pl.BlockSpec(memory_space=pl.ANY),
                  pl.BlockSpec((tc, D), row),
                  pl.BlockSpec((tc, D), row),
                  pl.BlockSpec((1, 6, D), lambda i: (i // per_seq, 0, 0))],
        out_specs=pl.BlockSpec((tc, D), row),
        out_shape=jax.ShapeDtypeStruct((T, D), F32),
        scratch_shapes=[pltpu.VMEM((2 * TOP_K * tc, SLAB, LANES), jnp.uint32), pltpu.SemaphoreType.DMA((2,))],
        compiler_params=_params("arbitrary"),
        name="combine",
    )(dest, dest, wts_t, ys, x1, sh, mod3)


def _tile(n, want):
    t = min(n, want)
    while n % t:
        t //= 2
    return t


def kernel(x, c, rel_bias_table, w_ada, b_ada, norm1_g, w_in, q_norm_g, k_norm_g, lambda_q1, lambda_k1,
           lambda_q2, lambda_k2, subln_g, pool_w, pool_scale, w_out, norm2_g, w_router, b_router,
           w_exp_gate, w_exp_up, w_exp_down, w_sh_gate, w_sh_up, w_sh_down):
    B, S, D = x.shape
    T = B * S
    depth = w_ada.shape[0]
    assert depth == 1, "LAM_INIT and the single-layer pipeline assume depth 1"
    assert D == N_HEADS * HEAD_DV and S % LANES == 0
    l = 0
    tm = _tile(S, 512)
    tq = _tile(S, 1024)

    x2 = x.reshape(T, D)
    mod3 = _ada(c, w_ada[l], b_ada[l]).reshape(B, 6, D)

    qg = (jnp.tile(q_norm_g[l], D // HEAD_DK) * (HEAD_DK ** -0.5 * LOG2E)).reshape(1, D)
    kg = jnp.tile(k_norm_g[l], D // HEAD_DK).reshape(1, D)
    q, k, v, p_in, ga, gp = _inproj(x2, mod3, norm1_g[l].reshape(1, D), w_in[l].astype(BF16),
                                    qg, kg, S, tm)

    strip = _bias_strip(rel_bias_table, S) * LOG2E
    nq = S // tq
    n = S + tq
    win = jnp.stack([strip[:, S - (i + 1) * tq: S - (i + 1) * tq + n] for i in range(nq)], axis=1)
    win = win.reshape(N_HEADS * nq, 1, n)
    lam_p = jnp.stack([lambda_q1[l], lambda_k1[l], lambda_q2[l], lambda_k2[l]])
    am = _attention(q, k, v, ga, win, lam_p, subln_g[l].reshape(1, HEAD_DV), B, S, tq)

    x1, h2r, logits_t = _mix(x2, am, gp, p_in, mod3, pool_w[l].astype(BF16),
                                  pool_scale[l].reshape(1, D), w_out[l].astype(BF16),
                                  norm2_g[l].reshape(1, D), w_router[l].T, S, tm)

    eidx, rank, wts, counts = _route(logits_t, b_router[l], _tile(T, 512))

    cnt = counts[:, 0].astype(jnp.int32)
    nblk = (cnt + MOE_BLOCK - 1) // MOE_BLOCK
    blk_end = jnp.cumsum(nblk)
    base = ((blk_end - nblk) * MOE_BLOCK).astype(F32).reshape(N_EXPERTS, 1)
    lastblk = jnp.where(nblk > 0, blk_end - 1, -1).astype(jnp.int32)
    n_blocks = -(-(T * TOP_K) // MOE_BLOCK) + N_EXPERTS
    blk_first = (blk_end - nblk).astype(jnp.int32)
    nused = blk_end[-1:].astype(jnp.int32)

    dest = _dest(eidx, rank, base, _tile(T, 512))
    xs, sh = _dispatch(lastblk, dest, h2r, w_sh_gate[l].astype(BF16), w_sh_up[l].astype(BF16),
                       w_sh_down[l].astype(BF16), n_blocks * MOE_BLOCK, _tile(S, 512))
    ys = _experts(blk_first, nblk.astype(jnp.int32), nused, xs, w_exp_gate[l], w_exp_up[l], w_exp_down[l])
    out = _combine(dest, wts.T, ys, x1, sh, mod3, S, _tile(S, 256))
    return out.reshape(B, S, D)
```

```python
import functools
import math

import jax
import jax.numpy as jnp
from jax import lax
from jax.experimental import pallas as pl
from jax.experimental.pallas import tpu as pltpu

N_HEADS = 8
HEAD_DK = 64
HEAD_DV = 2 * HEAD_DK
N_BUCKETS = 32
MAX_DISTANCE = 128
POOL_WINDOWS = (2, 4, 8, 16)
N_POOL_GROUPS = 4
N_EXPERTS = 256
TOP_K = 8
N_EXPERT_GROUPS = 8
TOPK_GROUPS = 4
ROUTED_SCALE = 2.5
MOE_BLOCK = 1024
ZERO_CHUNK = 256
EXPERT_IN_BUFS = 3
EXPERT_OUT_BUFS = 2
EPS = 1e-6
LAM_INIT = 0.8 - 0.6 * math.exp(-0.3 * 0)
LOG2E = math.log2(math.e)
KEY_TILE = 512

LANES = 128
SUBLANES = 8
SLAB = 4
VMEM_LIMIT = 56 * 1024 * 1024

F32 = jnp.float32
BF16 = jnp.bfloat16
HIGHEST = lax.Precision.HIGHEST
NEG_INF = float("-inf")


def _params(*sem):
    return pltpu.CompilerParams(dimension_semantics=sem, vmem_limit_bytes=VMEM_LIMIT)


def _const_spec(shape):
    nd = len(shape)
    return pl.BlockSpec(shape, lambda *_: (0,) * nd, pipeline_mode=pl.Buffered(1))


def _pack_rows(x):
    half = x.shape[1] // 2
    lo = lax.bitcast_convert_type(x[:, :half].astype(BF16).astype(F32), jnp.uint32) >> 16
    hi = lax.bitcast_convert_type(x[:, half:].astype(BF16).astype(F32), jnp.uint32)
    return hi | lo


def _unpack_rows(w):
    lo = lax.bitcast_convert_type(w << 16, F32)
    hi = lax.bitcast_convert_type(w & jnp.uint32(0xFFFF0000), F32)
    return jnp.concatenate([lo, hi], axis=-1)


def _rows_load(ref, first, n):
    flat = ref.reshape(ref.shape[0] * SLAB, LANES)
    return _unpack_rows(jnp.concatenate(
        [flat[pl.ds(first * SLAB + c, n, stride=SLAB), :] for c in range(SLAB)], axis=-1))


def _rows_store(ref, first, val):
    flat = ref.reshape(ref.shape[0] * SLAB, LANES)
    n = val.shape[0]
    w = _pack_rows(val)
    for c in range(SLAB):
        flat[pl.ds(first * SLAB + c, n, stride=SLAB), :] = w[:, c * LANES:(c + 1) * LANES]


def _ada_kernel(c_ref, w_ref, b_ref, o_ref):
    c = c_ref[...]
    s = c * jax.nn.sigmoid(c)
    o_ref[...] = jnp.dot(s, w_ref[...], precision=HIGHEST, preferred_element_type=F32) + b_ref[...]


def _ada(c, w_ada, b_ada):
    B, D = c.shape
    n = w_ada.shape[1] // D
    return pl.pallas_call(
        _ada_kernel,
        grid=(n,),
        in_specs=[pl.BlockSpec((B, D), lambda j: (0, 0)),
                  pl.BlockSpec((D, D), lambda j: (0, j)),
                  pl.BlockSpec((1, D), lambda j: (0, j))],
        out_specs=pl.BlockSpec((B, D), lambda j: (0, j)),
        out_shape=jax.ShapeDtypeStruct((B, n * D), F32),
        compiler_params=_params("arbitrary"),
        name="ada",
    )(c, w_ada, b_ada.reshape(1, -1))


def _bias_kernel(bk_ref, tab_ref, o_ref):
    bk = bk_ref[...]
    onehot = (lax.broadcasted_iota(jnp.int32, (N_BUCKETS, bk.shape[1]), 0) == bk).astype(F32)
    o_ref[...] = lax.dot_general(tab_ref[...], onehot, (((0,), (0,)), ((), ())),
                                 precision=HIGHEST, preferred_element_type=F32)


def _t5_buckets(rel):
    nb = N_BUCKETS // 2
    max_exact = nb // 2
    n = jnp.abs(rel)
    large = max_exact + (jnp.log(jnp.maximum(n, 1).astype(jnp.float32) / max_exact)
                         / math.log(MAX_DISTANCE / max_exact) * (nb - max_exact)).astype(jnp.int32)
    large = jnp.minimum(large, nb - 1)
    return jnp.where(rel > 0, nb, 0) + jnp.where(n < max_exact, n, large)


def _bias_strip(rel_table, S):
    L = 2 * S
    buckets = _t5_buckets(jnp.arange(L, dtype=jnp.int32) - S).reshape(1, L)
    return pl.pallas_call(
        _bias_kernel,
        out_shape=jax.ShapeDtypeStruct((N_HEADS, L), F32),
        name="bias",
    )(buckets, rel_table)


def _inproj_kernel(x_ref, mod_ref, g1_ref, w_ref, gsum_ref, gspread_ref, qg_ref, kg_ref,
                   q_ref, kt_ref, v_ref, p_ref, ga_ref, gp_ref):
    D = x_ref.shape[1]
    x = x_ref[...]
    h = x * lax.rsqrt(jnp.mean(x * x, axis=-1, keepdims=True) + EPS) * g1_ref[...]
    h = h * (1.0 + mod_ref[0, 1:2, :]) + mod_ref[0, 0:1, :]
    hb = h.astype(BF16)

    def chunk(c):
        return jnp.dot(hb, w_ref[:, c * D:(c + 1) * D], preferred_element_type=F32)

    def head_norm(y, g_ref):
        ss = jnp.dot((y * y).astype(BF16), gsum_ref[...], preferred_element_type=F32)
        r = lax.rsqrt(ss * (1.0 / HEAD_DK) + EPS)
        r_hi = r.astype(BF16)
        r_lo = (r - r_hi.astype(F32)).astype(BF16)
        scale = jnp.dot(jnp.concatenate([r_hi, r_lo], axis=-1), gspread_ref[...],
                        preferred_element_type=F32)
        return y * scale * g_ref[...]

    q_ref[...] = head_norm(chunk(0), qg_ref).astype(BF16)
    kn = head_norm(chunk(1), kg_ref)
    for hd in range(N_HEADS):
        for t in range(x_ref.shape[0] // KEY_TILE):
            blk = kn[t * KEY_TILE:(t + 1) * KEY_TILE, hd * HEAD_DV:(hd + 1) * HEAD_DV]
            kt_ref[0, hd, t] = blk.T.astype(BF16)
    v_ref[...] = chunk(2).astype(BF16)
    p_ref[...] = chunk(3)
    ga_ref[...] = jax.nn.sigmoid(chunk(4)).astype(BF16)
    gp_ref[...] = jax.nn.sigmoid(chunk(5)).astype(BF16)


def _inproj(x2, mod3, norm1_g, w_in_b, qg, kg, S, tm):
    T, D = x2.shape
    grp = jnp.arange(D) // HEAD_DK
    gsum = (grp[:, None] == jnp.arange(LANES)[None, :]).astype(BF16)
    gspread = jnp.concatenate([gsum.T, gsum.T], axis=0)
    per_seq = S // tm
    row = lambda i: (i, 0)
    tok = jax.ShapeDtypeStruct((T, D), BF16)
    kt = jax.ShapeDtypeStruct((T // S, N_HEADS, S // KEY_TILE, HEAD_DV, KEY_TILE), BF16)
    outs = [tok, kt, tok, jax.ShapeDtypeStruct((T, D), F32), tok, tok]
    tok_spec = pl.BlockSpec((tm, D), row)
    kt_spec = pl.BlockSpec((1, N_HEADS, tm // KEY_TILE, HEAD_DV, KEY_TILE),
                           lambda i: (i // per_seq, 0, i % per_seq, 0, 0))
    return pl.pallas_call(
        _inproj_kernel,
        grid=(T // tm,),
        in_specs=[pl.BlockSpec((tm, D), row),
                  pl.BlockSpec((1, 6, D), lambda i: (i // per_seq, 0, 0)),
                  _const_spec((1, D)),
                  _const_spec(w_in_b.shape),
                  _const_spec(gsum.shape),
                  _const_spec(gspread.shape),
                  _const_spec((1, D)),
                  _const_spec((1, D))],
        out_specs=[tok_spec, kt_spec, tok_spec, tok_spec, tok_spec, tok_spec],
        out_shape=outs,
        compiler_params=_params("arbitrary"),
        name="inproj",
    )(x2, mod3, norm1_g, w_in_b, gsum, gspread, qg, kg)


def _attn_kernel(q_ref, kt_ref, v_ref, ga_ref, win_ref, lam_ref, sg_ref, o_ref,
                 bias_ref, x_ref, m_ref, acc_ref):
    tq = q_ref.shape[0]
    n_kt, _, kt = kt_ref.shape[2:]
    n = win_ref.shape[2]

    @pl.when(pl.program_id(2) == 0)
    def _():
        xb = jnp.broadcast_to(win_ref[0, 0], (tq, n))
        rolled = pltpu.roll(xb, n - tq, 1, stride=1, stride_axis=0)
        for j in range(n_kt):
            bias_ref[j] = rolled[:, j * kt:(j + 1) * kt]

    lp = lam_ref[...]
    lam = (jnp.exp(jnp.sum(lp[0:1] * lp[1:2], axis=-1, keepdims=True))
           - jnp.exp(jnp.sum(lp[2:3] * lp[3:4], axis=-1, keepdims=True)) + LAM_INIT)

    q = q_ref[...]
    first = lax.broadcasted_iota(jnp.int32, (1, HEAD_DV), 1) < HEAD_DK
    zero = jnp.zeros_like(q)
    qq = jnp.concatenate([jnp.where(first, q, zero), jnp.where(first, zero, q)], axis=0)

    for j in range(n_kt):
        b = bias_ref[j]
        x = jnp.dot(qq, kt_ref[0, 0, j], preferred_element_type=F32) + jnp.concatenate([b, b], axis=0)
        x_ref[j] = x
        t = x[:, 0:LANES]
        for c in range(1, kt // LANES):
            t = jnp.maximum(t, x[:, c * LANES:(c + 1) * LANES])
        m_ref[...] = t if j == 0 else jnp.maximum(m_ref[...], t)
    m = jnp.max(m_ref[...], axis=-1, keepdims=True)

    m_ref[...] = jnp.broadcast_to(m, m_ref.shape)

    def pv_tile(j, carry):
        mb = m_ref[...]
        mb = jnp.concatenate([mb] * (kt // LANES), axis=-1)
        ones_col = (lax.broadcasted_iota(jnp.int32, (kt, LANES), 1) == 0).astype(BF16)
        p = jnp.exp2((x_ref[j] - mb).astype(BF16))
        rows = pl.ds(pl.multiple_of(j * kt, kt), kt)
        v_ext = jnp.concatenate([v_ref[rows, :], ones_col], axis=-1)
        pv = jnp.concatenate([jnp.dot(p[:tq], v_ext, preferred_element_type=F32),
                              jnp.dot(p[tq:], v_ext, preferred_element_type=F32)], axis=0)
        acc_ref[...] = acc_ref[...] + pv
        return carry

    acc_ref[...] = jnp.zeros_like(acc_ref)
    lax.fori_loop(0, n_kt, pv_tile, 0, unroll=2)
    acc = acc_ref[...]
    o1, l1 = acc[:tq, :HEAD_DV], acc[:tq, HEAD_DV:HEAD_DV + 1]
    o2, l2 = acc[tq:, :HEAD_DV], acc[tq:, HEAD_DV:HEAD_DV + 1]
    o = o1 / l1 - lam * (o2 / l2)
    o = o * lax.rsqrt(jnp.mean(o * o, axis=-1, keepdims=True) + EPS) * sg_ref[...] * (1.0 - LAM_INIT)
    o_ref[...] = (o * ga_ref[...].astype(F32)).astype(BF16)


def _attention(q, kt, v, ga, win, lam_p, subln_g, B, S, tq):
    T, D = q.shape
    nq = S // tq
    n = win.shape[2]
    n_kt = S // KEY_TILE
    qmap = lambda h, i, b: (b * nq + i, h)
    return pl.pallas_call(
        _attn_kernel,
        grid=(N_HEADS, nq, B),
        in_specs=[pl.BlockSpec((tq, HEAD_DV), qmap),
                  pl.BlockSpec((1, 1, n_kt, HEAD_DV, KEY_TILE), lambda h, i, b: (b, h, 0, 0, 0)),
                  pl.BlockSpec((S, HEAD_DV), lambda h, i, b: (b, h)),
                  pl.BlockSpec((tq, HEAD_DV), qmap),
                  pl.BlockSpec((1, 1, n), lambda h, i, b: (h * nq + i, 0, 0)),
                  pl.BlockSpec((4, HEAD_DK), lambda h, i, b: (0, 0)),
                  pl.BlockSpec((1, HEAD_DV), lambda h, i, b: (0, 0))],
        out_specs=pl.BlockSpec((tq, HEAD_DV), qmap),
        out_shape=jax.ShapeDtypeStruct((T, D), BF16),
        scratch_shapes=[pltpu.VMEM((n_kt, tq, KEY_TILE), F32),
                        pltpu.VMEM((n_kt, 2 * tq, KEY_TILE), F32),
                        pltpu.VMEM((2 * tq, LANES), F32),
                        pltpu.VMEM((2 * tq, HEAD_DV + LANES), F32)],
        compiler_params=_params("arbitrary", "arbitrary", "arbitrary"),
        name="attn",
    )(q, kt, v, ga, win, lam_p, subln_g)


def _mix_kernel(S, x_ref, am_ref, gp_ref, p_ref, pprev_ref, pnext_ref, mod_ref, pw_ref, ps_ref,
                wo_ref, g2_ref, wr_ref, x1_ref, h2r_ref, lg_ref):
    tm, D = x_ref.shape
    gc = D // N_POOL_GROUPS
    halo = SUBLANES
    ne = tm + 2 * halo
    pos0 = (pl.program_id(0) % (S // tm)) * tm
    pos_e = pos0 - halo + lax.broadcasted_iota(jnp.int32, (ne, 1), 0)
    valid = (pos_e >= 0) & (pos_e < S)
    ext = jnp.concatenate([pprev_ref[...], p_ref[...], pnext_ref[...]], axis=0)
    ext = jnp.where(valid, ext, 0.0)
    pos = pos0 + lax.broadcasted_iota(jnp.int32, (tm, 1), 0)

    merged = []
    for g, w in enumerate(POOL_WINDOWS):
        half = w // 2
        e = ext[:, g * gc:(g + 1) * gc]
        sw = e
        width = 1
        while width < w:
            sw = sw + pltpu.roll(sw, width, 0)
            width *= 2
        win = pltpu.roll(sw, ne - (half - 1), 0)[halo:halo + tm] if half > 1 else sw[halo:halo + tm]
        lo = jnp.clip(pos - half, 0, S - 1)
        hi = jnp.clip(pos + half - 1, 0, S - 1)
        cnt = (hi - lo + 1).astype(F32)
        mixed = win / cnt - p_ref[:, g * gc:(g + 1) * gc]
        pooled = jnp.dot(mixed.astype(BF16), pw_ref[g], preferred_element_type=F32)
        pooled = pooled * ps_ref[:, g * gc:(g + 1) * gc]
        merged.append(am_ref[:, g * gc:(g + 1) * gc].astype(F32)
                      + gp_ref[:, g * gc:(g + 1) * gc].astype(F32) * pooled)
    merged = jnp.concatenate(merged, axis=-1).astype(BF16)
    y = jnp.dot(merged, wo_ref[...], preferred_element_type=F32)
    x1 = x_ref[...] + mod_ref[0, 2:3, :] * y
    x1_ref[...] = x1
    h2 = x1 * lax.rsqrt(jnp.mean(x1 * x1, axis=-1, keepdims=True) + EPS) * g2_ref[...]
    h2 = h2 * (1.0 + mod_ref[0, 4:5, :]) + mod_ref[0, 3:4, :]
    _rows_store(h2r_ref, 0, h2)
    E = lg_ref.shape[0]
    nt = (((1,), (1,)), ((), ()))
    h_hi = h2.astype(BF16)
    h_lo = (h2 - h_hi.astype(F32)).astype(BF16)
    both = lax.dot_general(wr_ref[...], h_hi, nt, preferred_element_type=F32)
    lg_ref[...] = both[:E] + both[E:] + lax.dot_general(wr_ref[:E, :], h_lo, nt, preferred_element_type=F32)


def _mix(x2, am, gp, p_in, mod3, pool_w_b, pool_scale, w_out_b, norm2_g, w_router_t, S, tm):
    T, D = x2.shape
    per_seq = S // tm
    hb = tm // SUBLANES
    last = T // SUBLANES - 1
    row = lambda i: (i, 0)
    E = w_router_t.shape[0]
    w_hi = w_router_t.astype(BF16)
    w_router_t = jnp.concatenate([w_hi, (w_router_t - w_hi.astype(F32)).astype(BF16)], axis=0)
    return pl.pallas_call(
        functools.partial(_mix_kernel, S),
        grid=(T // tm,),
        in_specs=[pl.BlockSpec((tm, D), row),
                  pl.BlockSpec((tm, D), row),
                  pl.BlockSpec((tm, D), row),
                  pl.BlockSpec((tm, D), row),
                  pl.BlockSpec((SUBLANES, D), lambda i: (jnp.maximum(i * hb - 1, 0), 0)),
                  pl.BlockSpec((SUBLANES, D), lambda i: (jnp.minimum((i + 1) * hb, last), 0)),
                  pl.BlockSpec((1, 6, D), lambda i: (i // per_seq, 0, 0)),
                  _const_spec(pool_w_b.shape),
                  _const_spec((1, D)),
                  _const_spec((D, D)),
                  _const_spec((1, D)),
                  _const_spec((2 * E, D))],
        out_specs=[pl.BlockSpec((tm, D), row),
                   pl.BlockSpec((tm, SLAB, LANES), lambda i: (i, 0, 0)),
                   pl.BlockSpec((E, tm), lambda i: (0, i))],
        out_shape=[jax.ShapeDtypeStruct((T, D), F32),
                   jax.ShapeDtypeStruct((T, SLAB, LANES), jnp.uint32),
                   jax.ShapeDtypeStruct((E, T), F32)],
        compiler_params=_params("arbitrary"),
        name="mix",
    )(x2, am, gp, p_in, p_in, p_in, mod3, pool_w_b, pool_scale, w_out_b, norm2_g, w_router_t)


def _route_kernel(lg_ref, br_ref, tri_ref, eidx_ref, rank_ref, wts_ref, cnt_ref):
    E, tr = lg_ref.shape
    per = E // N_EXPERT_GROUPS

    @pl.when(pl.program_id(0) == 0)
    def _():
        cnt_ref[...] = jnp.zeros_like(cnt_ref)

    scores = jax.nn.sigmoid(lg_ref[...])
    biased = scores + br_ref[...]
    b3 = biased.reshape(N_EXPERT_GROUPS, per, tr)
    io_per = lax.broadcasted_iota(jnp.int32, b3.shape, 1)
    m1 = jnp.max(b3, axis=1, keepdims=True)
    i1 = jnp.min(jnp.where(b3 == m1, io_per, per), axis=1, keepdims=True)
    m2 = jnp.max(jnp.where(io_per == i1, NEG_INF, b3), axis=1, keepdims=True)
    gs = (m1 + m2)[:, 0, :]

    io_g = lax.broadcasted_iota(jnp.int32, gs.shape, 0)
    gsel = jnp.zeros(gs.shape, jnp.bool_)
    cur = gs
    for _ in range(TOPK_GROUPS):
        gm = jnp.max(cur, axis=0, keepdims=True)
        gi = jnp.min(jnp.where(cur == gm, io_g, N_EXPERT_GROUPS), axis=0, keepdims=True)
        pick = io_g == gi
        gsel = gsel | pick
        cur = jnp.where(pick, NEG_INF, cur)
    cur = jnp.where(gsel[:, None, :], b3, NEG_INF).reshape(E, tr)

    io_e = lax.broadcasted_iota(jnp.int32, (E, tr), 0)
    idxs, raw = [], []
    sel = jnp.zeros((E, tr), jnp.bool_)
    for _ in range(TOP_K):
        m = jnp.max(cur, axis=0, keepdims=True)
        idx = jnp.min(jnp.where(cur == m, io_e, E), axis=0, keepdims=True)
        pick = io_e == idx
        idxs.append(idx)
        raw.append(jnp.sum(jnp.where(pick, scores, 0.0), axis=0, keepdims=True))
        sel = sel | pick
        cur = jnp.where(pick, NEG_INF, cur)
    raw = jnp.concatenate(raw, axis=0)
    wts_ref[...] = raw / jnp.sum(raw, axis=0, keepdims=True) * ROUTED_SCALE
    eidx_ref[...] = jnp.concatenate(idxs, axis=0)

    self_f = sel.astype(F32)
    incl = jnp.dot(sel.astype(BF16), tri_ref[...], preferred_element_type=F32)
    before = cnt_ref[...] + incl - self_f
    ranks = [jnp.sum(jnp.where(io_e == idx, before, 0.0), axis=0, keepdims=True) for idx in idxs]
    rank_ref[...] = jnp.concatenate(ranks, axis=0).astype(jnp.int32)
    cnt_ref[...] = cnt_ref[...] + jnp.sum(self_f, axis=1, keepdims=True)


def _route(logits_t, b_router, tr):
    E, T = logits_t.shape
    tri = (jnp.arange(tr)[:, None] <= jnp.arange(tr)[None, :]).astype(BF16)
    col = lambda i: (0, i)
    return pl.pallas_call(
        _route_kernel,
        grid=(T // tr,),
        in_specs=[pl.BlockSpec((E, tr), col),
                  _const_spec((E, 1)),
                  _const_spec((tr, tr))],
        out_specs=[pl.BlockSpec((TOP_K, tr), col),
                   pl.BlockSpec((TOP_K, tr), col),
                   pl.BlockSpec((TOP_K, tr), col),
                   pl.BlockSpec((E, 1), lambda i: (0, 0))],
        out_shape=[jax.ShapeDtypeStruct((TOP_K, T), jnp.int32),
                   jax.ShapeDtypeStruct((TOP_K, T), jnp.int32),
                   jax.ShapeDtypeStruct((TOP_K, T), F32),
                   jax.ShapeDtypeStruct((E, 1), F32)],
        compiler_params=_params("arbitrary"),
        name="route",
    )(logits_t, b_router.reshape(E, 1), tri)


def _dest_kernel(eidx_ref, rank_ref, base_ref, dest_ref):
    K, tr = eidx_ref.shape
    E = base_ref.shape[0]
    io_e = lax.broadcasted_iota(jnp.int32, (E, tr), 0)
    base = base_ref[...]
    rows = [jnp.sum(jnp.where(io_e == eidx_ref[k:k + 1, :], base, 0.0), axis=0, keepdims=True)
            for k in range(K)]
    dest_ref[...] = jnp.concatenate(rows, axis=0).astype(jnp.int32) + rank_ref[...]


def _dest(eidx, rank, base, tr):
    K, T = eidx.shape
    col = lambda i: (0, i)
    return pl.pallas_call(
        _dest_kernel,
        grid=(T // tr,),
        in_specs=[pl.BlockSpec((K, tr), col), pl.BlockSpec((K, tr), col), _const_spec(base.shape)],
        out_specs=pl.BlockSpec((K, tr), col),
        out_shape=jax.ShapeDtypeStruct((K, T), jnp.int32),
        compiler_params=_params("arbitrary"),
        name="dest",
    )(eidx, rank, base)


def _dispatch_kernel(zfirst_ref, zcount_ref, dest_ref, h_ref, wsg_ref, wsu_ref, wsd_ref, xs_ref, sh_ref,
                     zbuf_ref, sem):
    td = h_ref.shape[0]

    @pl.when(pl.program_id(0) == 0)
    def _():
        zbuf_ref[...] = jnp.zeros_like(zbuf_ref)

        def zcopy(e, q):
            return pltpu.make_async_copy(zbuf_ref, xs_ref.at[pl.ds((zfirst_ref[e] + q) * ZERO_CHUNK, ZERO_CHUNK)], sem)

        def start(e, c):
            for q in range(MOE_BLOCK // ZERO_CHUNK):
                @pl.when(q < zcount_ref[e])
                def _():
                    zcopy(e, q).start()
            return c

        def wait(e, c):
            for q in range(MOE_BLOCK // ZERO_CHUNK):
                @pl.when(q < zcount_ref[e])
                def _():
                    zcopy(e, q).wait()
            return c

        lax.fori_loop(0, N_EXPERTS, start, 0)
        lax.fori_loop(0, N_EXPERTS, wait, 0)

    def row_copy(t, k):
        return pltpu.make_async_copy(h_ref.at[t], xs_ref.at[dest_ref[k, t]], sem)

    def start_rows(t, c):
        for k in range(TOP_K):
            row_copy(t, k).start(priority=k % 2)
        return c

    def wait_rows():
        for _ in range(TOP_K):
            pltpu.make_async_copy(h_ref, xs_ref.at[pl.ds(0, td)], sem).wait()

    lax.fori_loop(0, td, start_rows, 0, unroll=4)

    h = _rows_load(h_ref, 0, td).astype(BF16)
    g = jnp.dot(h, wsg_ref[...], preferred_element_type=F32)
    u = jnp.dot(h, wsu_ref[...], preferred_element_type=F32)
    shared = jnp.dot((g * jax.nn.sigmoid(g) * u).astype(BF16), wsd_ref[...], preferred_element_type=F32)
    sh_ref[...] = shared.astype(BF16)

    wait_rows()


def _dispatch(zfirst, zcount, dest, h2r, wsg, wsu, wsd, n_pad, td):
    T = h2r.shape[0]
    D = wsg.shape[0]
    const = lambda shape: pl.BlockSpec(shape, lambda i, zf, zc: (0,) * len(shape), pipeline_mode=pl.Buffered(1))
    return pl.pallas_call(
        _dispatch_kernel,
        grid_spec=pltpu.PrefetchScalarGridSpec(
            num_scalar_prefetch=2,
            grid=(T // td,),
            in_specs=[pl.BlockSpec((TOP_K, td), lambda i, zf, zc: (0, i), memory_space=pltpu.SMEM),
                      pl.BlockSpec((td, SLAB, LANES), lambda i, zf, zc: (i, 0, 0)),
                      const(wsg.shape), const(wsu.shape), const(wsd.shape)],
            out_specs=[pl.BlockSpec(memory_space=pl.ANY),
                       pl.BlockSpec((td, D), lambda i, zf, zc: (i, 0))],
            scratch_shapes=[pltpu.VMEM((ZERO_CHUNK, SLAB, LANES), jnp.uint32), pltpu.SemaphoreType.DMA]),
        out_shape=[jax.ShapeDtypeStruct((n_pad, SLAB, LANES), jnp.uint32),
                   jax.ShapeDtypeStruct((T, D), BF16)],
        compiler_params=_params("arbitrary"),
        name="dispatch",
    )(zfirst, zcount, dest, h2r, wsg, wsu, wsd)


def _expert_kernel(first_ref, nblk_ref, nused_ref, xs_ref, wg_ref, wu_ref, wd_ref, ys_ref,
                   xbuf_ref, ybuf_ref, sin, sout):
    e = pl.program_id(0)
    nused = nused_ref[0]

    def block(ref, b):
        return ref.at[pl.ds(b * MOE_BLOCK, MOE_BLOCK)]

    def copy_in(b):
        s = b % EXPERT_IN_BUFS
        return pltpu.make_async_copy(block(xs_ref, b), block(xbuf_ref, s), sin.at[s])

    def copy_out(b):
        s = b % EXPERT_OUT_BUFS
        return pltpu.make_async_copy(block(ybuf_ref, s), block(ys_ref, b), sout.at[s])

    @pl.when(e == 0)
    def _():
        for b in range(EXPERT_IN_BUFS - 1):
            @pl.when(b < nused)
            def _():
                copy_in(b).start()

    def one_block(i, carry):
        j = first_ref[e] + i

        @pl.when(j + EXPERT_IN_BUFS - 1 < nused)
        def _():
            copy_in(j + EXPERT_IN_BUFS - 1).start()

        copy_in(j).wait()

        @pl.when(j >= EXPERT_OUT_BUFS)
        def _():
            copy_out(j - EXPERT_OUT_BUFS).wait()

        x = _rows_load(xbuf_ref, (j % EXPERT_IN_BUFS) * MOE_BLOCK, MOE_BLOCK).astype(BF16)
        g = jnp.dot(x, wg_ref[0].astype(BF16), preferred_element_type=F32)
        u = jnp.dot(x, wu_ref[0].astype(BF16), preferred_element_type=F32)
        hmid = (g * jax.nn.sigmoid(g) * u).astype(BF16)
        y = jnp.dot(hmid, wd_ref[0].astype(BF16), preferred_element_type=F32)
        _rows_store(ybuf_ref, (j % EXPERT_OUT_BUFS) * MOE_BLOCK, y)
        copy_out(j).start()

        @pl.when(j == nused - 1)
        def _():
            for back in range(EXPERT_OUT_BUFS):
                @pl.when(j - back >= 0)
                def _():
                    copy_out(j - back).wait()
        return carry

    lax.fori_loop(0, nblk_ref[e], one_block, 0)


def _experts(blk_first, nblk, nused, xs, w_eg, w_eu, w_ed):
    E, D, DE = w_eg.shape
    wmap = lambda e, bf, nb, nu: (e, 0, 0)
    return pl.pallas_call(
        _expert_kernel,
        grid_spec=pltpu.PrefetchScalarGridSpec(
            num_scalar_prefetch=3,
            grid=(E,),
            in_specs=[pl.BlockSpec(memory_space=pl.ANY),
                      pl.BlockSpec((1, D, DE), wmap),
                      pl.BlockSpec((1, D, DE), wmap),
                      pl.BlockSpec((1, DE, D), wmap)],
            out_specs=pl.BlockSpec(memory_space=pl.ANY),
            scratch_shapes=[pltpu.VMEM((EXPERT_IN_BUFS * MOE_BLOCK, SLAB, LANES), jnp.uint32),
                            pltpu.VMEM((EXPERT_OUT_BUFS * MOE_BLOCK, SLAB, LANES), jnp.uint32),
                            pltpu.SemaphoreType.DMA((EXPERT_IN_BUFS,)),
                            pltpu.SemaphoreType.DMA((EXPERT_OUT_BUFS,))]),
        out_shape=jax.ShapeDtypeStruct(xs.shape, jnp.uint32),
        compiler_params=_params("arbitrary"),
        name="expert",
    )(blk_first, nblk, nused, xs, w_eg, w_eu, w_ed)


def _combine_kernel(dest_ref, dnext_ref, wt_ref, ys_ref, x1_ref, sh_ref, mod_ref, o_ref, buf_ref, sems):
    tc = x1_ref.shape[0]
    i = pl.program_id(0)
    n = pl.num_programs(0)
    slot = i % 2
    per_slot = TOP_K * tc

    def row_copy(d_ref, s, t, k):
        return pltpu.make_async_copy(ys_ref.at[d_ref[k, t]], buf_ref.at[s * per_slot + k * tc + t], sems.at[s])

    def start_rows(d_ref, s, t):
        for k in range(TOP_K):
            row_copy(d_ref, s, t, k).start(priority=k % 2)

    def start_tile(d_ref, s):
        def body(t, c):
            start_rows(d_ref, s, t)
            return c
        lax.fori_loop(0, tc, body, 0, unroll=16)

    @pl.when(i == 0)
    def _():
        start_tile(dest_ref, slot)

    @pl.when(i + 1 < n)
    def _():
        start_tile(dnext_ref, 1 - slot)

    pltpu.make_async_copy(ys_ref.at[pl.ds(0, per_slot)], buf_ref.at[pl.ds(slot * per_slot, per_slot)],
                          sems.at[slot]).wait()

    g2 = mod_ref[0, 5:6, :]
    o_ref[...] = x1_ref[...] + g2 * sh_ref[...].astype(F32)
    for k in range(TOP_K):
        w = wt_ref[:, k:k + 1] * g2
        o_ref[...] = o_ref[...] + _rows_load(buf_ref, slot * per_slot + k * tc, tc) * w


def _combine(dest, wts_t, ys, x1, sh, mod3, S, tc):
    T, D = x1.shape
    per_seq = S // tc
    last = T // tc - 1
    row = lambda i: (i, 0)
    return pl.pallas_call(
        _combine_kernel,
        grid=(T // tc,),
        in_specs=[pl.BlockSpec((TOP_K, tc), lambda i: (0, i), memory_space=pltpu.SMEM),
                  pl.BlockSpec((TOP_K, tc), lambda i: (0, jnp.minimum(i + 1, last)), memory_space=pltpu.SMEM),
                  pl.BlockSpec((tc, TOP_K), row),
                  pl.BlockSpec(memory_space=pl.ANY),
                  pl.BlockSpec((tc, D), row),
                  pl.BlockSpec((tc, D), row),
                  pl.BlockSpec((1, 6, D), lambda i: (i // per_seq, 0, 0))],
        out_specs=pl.BlockSpec((tc, D), row),
        out_shape=jax.ShapeDtypeStruct((T, D), F32),
        scratch_shapes=[pltpu.VMEM((2 * TOP_K * tc, SLAB, LANES), jnp.uint32), pltpu.SemaphoreType.DMA((2,))],
        compiler_params=_params("arbitrary"),
        name="combine",
    )(dest, dest, wts_t, ys, x1, sh, mod3)


def _tile(n, want):
    t = min(n, want)
    while n % t:
        t //= 2
    return t


def kernel(x, c, rel_bias_table, w_ada, b_ada, norm1_g, w_in, q_norm_g, k_norm_g, lambda_q1, lambda_k1,
           lambda_q2, lambda_k2, subln_g, pool_w, pool_scale, w_out, norm2_g, w_router, b_router,
           w_exp_gate, w_exp_up, w_exp_down, w_sh_gate, w_sh_up, w_sh_down):
    B, S, D = x.shape
    T = B * S
    depth = w_ada.shape[0]
    assert depth == 1, "LAM_INIT and the single-layer pipeline assume depth 1"
    assert D == N_HEADS * HEAD_DV and S % LANES == 0
    l = 0
    tm = _tile(S, 512)
    tq = _tile(S, 1024)

    x2 = x.reshape(T, D)
    mod3 = _ada(c, w_ada[l], b_ada[l]).reshape(B, 6, D)

    qg = (jnp.tile(q_norm_g[l], D // HEAD_DK) * (HEAD_DK ** -0.5 * LOG2E)).reshape(1, D)
    kg = jnp.tile(k_norm_g[l], D // HEAD_DK).reshape(1, D)
    q, k, v, p_in, ga, gp = _inproj(x2, mod3, norm1_g[l].reshape(1, D), w_in[l].astype(BF16),
                                    qg, kg, S, tm)

    strip = _bias_strip(rel_bias_table, S) * LOG2E
    nq = S // tq
    n = S + tq
    win = jnp.stack([strip[:, S - (i + 1) * tq: S - (i + 1) * tq + n] for i in range(nq)], axis=1)
    win = win.reshape(N_HEADS * nq, 1, n)
    lam_p = jnp.stack([lambda_q1[l], lambda_k1[l], lambda_q2[l], lambda_k2[l]])
    am = _attention(q, k, v, ga, win, lam_p, subln_g[l].reshape(1, HEAD_DV), B, S, tq)

    x1, h2r, logits_t = _mix(x2, am, gp, p_in, mod3, pool_w[l].astype(BF16),
                                  pool_scale[l].reshape(1, D), w_out[l].astype(BF16),
                                  norm2_g[l].reshape(1, D), w_router[l].T, S, tm)

    eidx, rank, wts, counts = _route(logits_t, b_router[l], _tile(T, 512))

    cnt = counts[:, 0].astype(jnp.int32)
    nblk = (cnt + MOE_BLOCK - 1) // MOE_BLOCK
    blk_end = jnp.cumsum(nblk)
    base = ((blk_end - nblk) * MOE_BLOCK).astype(F32).reshape(N_EXPERTS, 1)
    per_blk = MOE_BLOCK // ZERO_CHUNK
    filled = (cnt - (nblk - 1) * MOE_BLOCK) // ZERO_CHUNK
    zfirst = ((blk_end - 1) * per_blk + filled).astype(jnp.int32)
    zcount = jnp.where(nblk > 0, per_blk - filled, 0).astype(jnp.int32)
    n_blocks = -(-(T * TOP_K) // MOE_BLOCK) + N_EXPERTS
    blk_first = (blk_end - nblk).astype(jnp.int32)
    nused = blk_end[-1:].astype(jnp.int32)

    dest = _dest(eidx, rank, base, _tile(T, 512))
    xs, sh = _dispatch(zfirst, zcount, dest, h2r, w_sh_gate[l].astype(BF16), w_sh_up[l].astype(BF16),
                       w_sh_down[l].astype(BF16), n_blocks * MOE_BLOCK, _tile(S, 512))
    ys = _experts(blk_first, nblk.astype(jnp.int32), nused, xs, w_exp_gate[l], w_exp_up[l], w_exp_down[l])
    out = _combine(dest, wts.T, ys, x1, sh, mod3, S, _tile(S, 256))
    return out.reshape(B, S, D)
```
